```python
import jax, jax.numpy as jnp
from jax import lax
import numpy as np

D_MODEL = 1024
BATCH = 8
SEQ = 4096
DEPTH = 2

CTX_LEN = 256
GRID_W = 64
A_CHUNK = 2 * GRID_W
A_WIDTH = D_MODEL
A_GROUPS = 8
A_GROUP_DIM = A_WIDTH // A_GROUPS
GLA_HEADS = 4
GLA_DK = D_MODEL // 2
GLA_DV = D_MODEL
GLA_HEAD_K = GLA_DK // GLA_HEADS
GLA_HEAD_V = GLA_DV // GLA_HEADS
GLA_GATE_RANK = 16
GLA_TAU = 16.0
GLA_CHUNK = 64
N_EXPERTS = 16
N_EXPERT_GROUPS = 4
EXPERTS_PER_GROUP = N_EXPERTS // N_EXPERT_GROUPS
TOP_K = 2
D_EXPERT = 3 * D_MODEL // 2
MOE_BLOCK = 128
EPS = 1e-6
IN_SIZES = (2 * A_WIDTH, GLA_DK, GLA_DK, GLA_DV, GLA_DV, GLA_GATE_RANK, GLA_GATE_RANK, 2 * D_MODEL)
IN_COLS = sum(IN_SIZES)

kernel_name = "hybrid_gmlp_gla_moe_diffusion_block"


def rms_norm(x, g):
    xf = x.astype(jnp.float32)
    y = xf * lax.rsqrt(jnp.mean(xf * xf, axis=-1, keepdims=True) + EPS)
    return (y * g.astype(jnp.float32)).astype(x.dtype)


def layer_norm(x, g, b):
    xf = x.astype(jnp.float32)
    mu = jnp.mean(xf, axis=-1, keepdims=True)
    xc = xf - mu
    y = xc * lax.rsqrt(jnp.mean(xc * xc, axis=-1, keepdims=True) + EPS)
    return (y * g.astype(jnp.float32) + b.astype(jnp.float32)).astype(x.dtype)


def split_cols(p):
    idx, acc = [], 0
    for s in IN_SIZES[:-1]:
        acc += s
        idx.append(acc)
    return jnp.split(p, idx, axis=-1)


def spatial_gating(z, ln_g, ln_b, w_s, b_s):
    u, v = jnp.split(z, 2, axis=-1)
    v = layer_norm(v, ln_g, ln_b)
    bsz, length, width = v.shape
    vr = v.reshape(bsz, length // A_CHUNK, A_CHUNK, A_GROUPS, A_GROUP_DIM)
    mixed = jnp.einsum('gts,bnsgc->bntgc', w_s, vr) + b_s.T[:, :, None]
    return u * mixed.reshape(bsz, length, width)


def gla_chunk_scan(q, k, v, log_a, s0):
    bsz, nh, length, _ = q.shape
    dv = v.shape[-1]
    n = length // GLA_CHUNK
    cs = lambda t: t.astype(jnp.float32).reshape(bsz, nh, n, GLA_CHUNK, t.shape[-1])
    q, k, v, log_a = cs(q), cs(k), cs(v), cs(log_a)
    b = jnp.cumsum(log_a, axis=-2)
    b_last = b[..., -1:, :]
    q_in = q * jnp.exp(b)
    k_in = k * jnp.exp(-b)
    k_st = k * jnp.exp(b_last - b)
    mask = jnp.tril(jnp.ones((GLA_CHUNK, GLA_CHUNK), dtype=bool))
    att = jnp.where(mask, jnp.einsum('bhntk,bhnsk->bhnts', q_in, k_in), 0.0)
    o_intra = jnp.einsum('bhnts,bhnsv->bhntv', att, v)

    def step(state, xs):
        qc, kc, vc, dc = xs
        o = jnp.einsum('bhtk,bhkv->bhtv', qc, state)
        state = dc[..., 0, :, None] * state + jnp.einsum('bhsk,bhsv->bhkv', kc, vc)
        return state, o

    xs = tuple(jnp.moveaxis(t, 2, 0) for t in (q_in, k_st, v, jnp.exp(b_last)))
    s_fin, o_inter = lax.scan(step, s0, xs)
    o = o_intra + jnp.moveaxis(o_inter, 0, 2)
    return o.reshape(bsz, nh, length, dv), s_fin


def gla_bidirectional(q, k, v, la_f, la_b, s_f, s_b):
    o_f, s_f = gla_chunk_scan(q, k, v, la_f, s_f)
    flip = lambda t: jnp.flip(t, axis=2)
    o_b, s_b = gla_chunk_scan(flip(q), flip(k), flip(v), flip(la_b), s_b)
    return o_f + flip(o_b), s_f, s_b


def token_mixer(h_x, h_c, w_in, w_gate2, b_gate2, gla_g, sgu_ln_g, sgu_ln_b, sgu_w, sgu_b,
                w_a, w_b, b_branch, w_out, with_ctx):
    parts_x = split_cols(h_x @ w_in)
    parts_c = split_cols(h_c @ w_in)

    def heads(t):
        bsz, length, width = t.shape
        return t.reshape(bsz, length, GLA_HEADS, width // GLA_HEADS).transpose(0, 2, 1, 3)

    def gla_inputs(parts):
        _, q, k, v, _, lr_f, lr_b, _ = parts
        la = [heads(jax.nn.log_sigmoid((lr @ w_gate2[d] + b_gate2[d]).astype(jnp.float32)) / GLA_TAU)
              for d, lr in enumerate((lr_f, lr_b))]
        return heads(q) * GLA_HEAD_K ** -0.5, heads(k), heads(v), la[0], la[1]

    zero = jnp.zeros((h_c.shape[0], GLA_HEADS, GLA_HEAD_K, GLA_HEAD_V), jnp.float32)
    o_c, s_f, s_b = gla_bidirectional(*gla_inputs(parts_c), zero, zero)
    o_x, _, _ = gla_bidirectional(*gla_inputs(parts_x), s_f, s_b)

    def merge(parts, o):
        z_a, _, _, _, r, _, _, gates = parts
        y_a = spatial_gating(jax.nn.gelu(z_a), sgu_ln_g, sgu_ln_b, sgu_w, sgu_b) @ w_a
        o_n = o * lax.rsqrt(jnp.mean(o * o, axis=-1, keepdims=True) + EPS)
        bsz, _, length, _ = o.shape
        o_n = o_n.transpose(0, 2, 1, 3).reshape(bsz, length, GLA_DV) * gla_g
        y_b = (o_n.astype(r.dtype) * jax.nn.silu(r)) @ w_b
        g_a, g_b = jnp.split(jax.nn.sigmoid(gates + b_branch), 2, axis=-1)
        return (g_a * y_a + g_b * y_b) @ w_out

    y_x = merge(parts_x, o_x)
    y_c = merge(parts_c, o_c) if with_ctx else None
    return y_x, y_c


def moe_ffn(h, w_router, b_router, w_g, w_u, w_d):
    n_tok, d = h.shape
    logits = h.astype(jnp.float32) @ w_router.astype(jnp.float32) + b_router.astype(jnp.float32)
    probs = jax.nn.softmax(logits, axis=-1).reshape(n_tok, N_EXPERT_GROUPS, EXPERTS_PER_GROUP)
    group_score = lax.top_k(probs, TOP_K)[0].sum(-1)
    g_sel = jnp.argmax(group_score, axis=-1)
    in_group = jnp.take_along_axis(probs, g_sel[:, None, None], axis=1)[:, 0]
    top_w, top_i = lax.top_k(in_group, TOP_K)
    weight = top_w / jnp.sum(top_w, axis=-1, keepdims=True)
    expert = (g_sel[:, None] * EXPERTS_PER_GROUP + top_i).reshape(-1).astype(jnp.int32)

    n_slot = n_tok * TOP_K
    token = jnp.repeat(jnp.arange(n_tok, dtype=jnp.int32), TOP_K)
    order = jnp.argsort(expert)
    e_sorted = expert[order]
    counts = jnp.bincount(expert, length=N_EXPERTS).astype(jnp.int32)
    starts = jnp.cumsum(counts) - counts
    padded = (counts + MOE_BLOCK - 1) // MOE_BLOCK * MOE_BLOCK
    p_ends = jnp.cumsum(padded)
    p_starts = p_ends - padded
    dest_sorted = p_starts[e_sorted] + jnp.arange(n_slot, dtype=jnp.int32) - starts[e_sorted]
    n_blocks = -(-n_slot // MOE_BLOCK) + N_EXPERTS
    buf_tok = jnp.full((n_blocks * MOE_BLOCK,), n_tok, jnp.int32).at[dest_sorted].set(token[order])
    block_e = jnp.minimum(jnp.searchsorted(p_ends, jnp.arange(n_blocks, dtype=jnp.int32) * MOE_BLOCK,
                                           side='right'), N_EXPERTS - 1)
    h_pad = jnp.concatenate([h, jnp.zeros((1, d), h.dtype)], axis=0)
    x_blk = h_pad[buf_tok].reshape(n_blocks, MOE_BLOCK, d)

    def expert_block(args):
        xb, e = args
        return (jax.nn.silu(xb @ w_g[e]) * (xb @ w_u[e])) @ w_d[e]

    y_blk = lax.map(expert_block, (x_blk, block_e)).reshape(-1, d)
    dest = jnp.zeros((n_slot,), jnp.int32).at[order].set(dest_sorted)
    y = y_blk[dest].reshape(n_tok, TOP_K, d) * weight[..., None].astype(h.dtype)
    return jnp.sum(y, axis=1)


def setup_inputs(seed: int = 0) -> dict:
    key = jax.random.key(seed)
    ks = iter(jax.random.split(key, 32))
    nrm = lambda shape, std: jax.random.normal(next(ks), shape, jnp.float32) * std
    L = DEPTH
    return {
        "x": nrm((BATCH, SEQ, D_MODEL), 1.0),
        "c": nrm((BATCH, D_MODEL), 1.0),
        "ctx": nrm((BATCH, CTX_LEN, D_MODEL), 1.0),
        "c_ctx": nrm((D_MODEL,), 1.0),
        "w_mod": nrm((L, D_MODEL, 6 * D_MODEL), 0.5 * D_MODEL ** -0.5),
        "b_mod": nrm((L, 6 * D_MODEL), 0.01),
        "g_norm1": 1.0 + nrm((L, D_MODEL), 0.01),
        "g_norm2": 1.0 + nrm((L, D_MODEL), 0.01),
        "w_in": nrm((L, D_MODEL, IN_COLS), D_MODEL ** -0.5),
        "w_gate2": nrm((L, 2, GLA_GATE_RANK, GLA_DK), GLA_GATE_RANK ** -0.5),
        "b_gate2": nrm((L, 2, GLA_DK), 0.1),
        "gla_norm_g": 1.0 + nrm((L, GLA_DV), 0.01),
        "sgu_ln_g": 1.0 + nrm((L, A_WIDTH), 0.01),
        "sgu_ln_b": nrm((L, A_WIDTH), 0.01),
        "sgu_w": nrm((L, A_GROUPS, A_CHUNK, A_CHUNK), A_CHUNK ** -0.5),
        "sgu_b": 1.0 + nrm((L, A_GROUPS, A_CHUNK), 0.01),
        "w_branch_a": nrm((L, A_WIDTH, D_MODEL), A_WIDTH ** -0.5),
        "w_branch_b": nrm((L, GLA_DV, D_MODEL), GLA_DV ** -0.5),
        "b_branch": nrm((L, 2 * D_MODEL), 0.01),
        "w_out": nrm((L, D_MODEL, D_MODEL), D_MODEL ** -0.5),
        "w_router": nrm((D_MODEL, N_EXPERTS), D_MODEL ** -0.5),
        "b_router": nrm((N_EXPERTS,), 0.01),
        "w_exp_gate": nrm((L, N_EXPERTS, D_MODEL, D_EXPERT), D_MODEL ** -0.5),
        "w_exp_up": nrm((L, N_EXPERTS, D_MODEL, D_EXPERT), D_MODEL ** -0.5),
        "w_exp_down": nrm((L, N_EXPERTS, D_EXPERT, D_MODEL), D_EXPERT ** -0.5),
        "g_final": 1.0 + nrm((D_MODEL,), 0.01),
    }


def reference(x, c, ctx, c_ctx, w_mod, b_mod, g_norm1, g_norm2, w_in, w_gate2, b_gate2, gla_norm_g,
              sgu_ln_g, sgu_ln_b, sgu_w, sgu_b, w_branch_a, w_branch_b, b_branch, w_out,
              w_router, b_router, w_exp_gate, w_exp_up, w_exp_down, g_final):
    bsz, seq, d = x.shape
    n_lat = bsz * seq
    for layer in range(DEPTH):
        with_ctx = layer < DEPTH - 1
        mod_x = (jax.nn.silu(c) @ w_mod[layer] + b_mod[layer])[:, None, :]
        mod_c = jax.nn.silu(c_ctx) @ w_mod[layer] + b_mod[layer]
        sh1, sc1, gt1, sh2, sc2, gt2 = jnp.split(mod_x, 6, axis=-1)
        csh1, csc1, cgt1, csh2, csc2, cgt2 = jnp.split(mod_c, 6, axis=-1)

        h_x = rms_norm(x, g_norm1[layer]) * (1.0 + sc1) + sh1
        h_c = rms_norm(ctx, g_norm1[layer]) * (1.0 + csc1) + csh1
        y_x, y_c = token_mixer(h_x, h_c, w_in[layer], w_gate2[layer], b_gate2[layer], gla_norm_g[layer],
                               sgu_ln_g[layer], sgu_ln_b[layer], sgu_w[layer], sgu_b[layer],
                               w_branch_a[layer], w_branch_b[layer], b_branch[layer], w_out[layer], with_ctx)
        x = x + gt1 * y_x

        tokens = (rms_norm(x, g_norm2[layer]) * (1.0 + sc2) + sh2).reshape(-1, d)
        if with_ctx:
            ctx = ctx + cgt1 * y_c
            h_c2 = rms_norm(ctx, g_norm2[layer]) * (1.0 + csc2) + csh2
            tokens = jnp.concatenate([tokens, h_c2.reshape(-1, d)], axis=0)
        y = moe_ffn(tokens, w_router, b_router, w_exp_gate[layer], w_exp_up[layer], w_exp_down[layer])
        x = x + gt2 * y[:n_lat].reshape(bsz, seq, d)
        if with_ctx:
            ctx = ctx + cgt2 * y[n_lat:].reshape(ctx.shape)
    return rms_norm(x, g_final)
```

```python
import functools

import jax
import jax.numpy as jnp
from jax import lax
from jax.experimental import pallas as pl
from jax.experimental.pallas import tpu as pltpu

F32 = jnp.float32
BF16 = jnp.bfloat16
HIGHEST = lax.Precision.HIGHEST

A_CHUNK = 128
A_GROUPS = 8
GLA_HEADS = 4
GLA_GATE_RANK = 16
GLA_TAU = 16.0
GLA_CHUNK = 64
N_EXPERTS = 16
N_EXPERT_GROUPS = 4
EXPERTS_PER_GROUP = N_EXPERTS // N_EXPERT_GROUPS
EPS = 1e-6

LANES = 128
TM = 256
TB = 256
MOD_ROWS = 16
VMEM_LIMIT = 56 * 1024 * 1024


def _dot(a, b):
    return jnp.dot(a, b, preferred_element_type=F32)


def _const_spec(shape):
    nd = len(shape)
    return pl.BlockSpec(shape, lambda *_: (0,) * nd)


def _params(sem):
    return pltpu.CompilerParams(dimension_semantics=sem, vmem_limit_bytes=VMEM_LIMIT)


def _mod_kernel(cc_ref, w_ref, b_ref, o_ref):
    cc = cc_ref[...]
    s = cc * jax.nn.sigmoid(cc)
    o_ref[0] = jnp.dot(s, w_ref[0], preferred_element_type=F32, precision=HIGHEST) + b_ref[0]


def _modulation(cc, w_mod, b_mod):
    n_layer, d, six_d = w_mod.shape
    return pl.pallas_call(
        _mod_kernel,
        grid=(n_layer, six_d // d),
        in_specs=[
            pl.BlockSpec((MOD_ROWS, d), lambda l, j: (0, 0)),
            pl.BlockSpec((1, d, d), lambda l, j: (l, 0, j)),
            pl.BlockSpec((1, 1, d), lambda l, j: (l, 0, j)),
        ],
        out_specs=pl.BlockSpec((1, MOD_ROWS, d), lambda l, j: (l, 0, j)),
        out_shape=jax.ShapeDtypeStruct((n_layer, MOD_ROWS, six_d), F32),
        compiler_params=_params(("arbitrary", "arbitrary")),
        name="modulation",
    )(cc, w_mod, b_mod.reshape(n_layer, 1, six_d))


def _gelu_tanh(x):
    c = 0.7978845608028654
    return 0.5 * x * (1.0 + jnp.tanh(c * (x + 0.044715 * (x * x * x))))


def _log_sigmoid(z):
    return jnp.minimum(z, 0.0) - jnp.log1p(jnp.exp(-jnp.abs(z)))


def _inproj_kernel(x_ref, mod_ref, g1_ref, wmain_ref, wlr_ref, wgates_ref, wg2_ref, bg2_ref,
                   lng_ref, lnb_ref, sguw_ref, sgub_ref, wa_ref, bbr_ref,
                   yag_ref, gb_ref, q_ref, k_ref, v_ref, sr_ref, la_ref, sa_ref):
    d = x_ref.shape[1]
    dk = q_ref.shape[1]
    x = x_ref[...]
    sh1 = mod_ref[:, 0:d]
    sc1 = mod_ref[:, d:2 * d]
    h = x * lax.rsqrt(jnp.mean(x * x, axis=-1, keepdims=True) + EPS) * g1_ref[...]
    hb = (h * (1.0 + sc1) + sh1).astype(BF16)

    u = _gelu_tanh(_dot(hb, wmain_ref[:, 0:d]))
    vv = _gelu_tanh(_dot(hb, wmain_ref[:, d:2 * d]))
    mu = jnp.mean(vv, axis=-1, keepdims=True)
    vc = vv - mu
    vn = vc * lax.rsqrt(jnp.mean(vc * vc, axis=-1, keepdims=True) + EPS) * lng_ref[...] + lnb_ref[...]
    vnb = vn.astype(BF16)
    gdim = d // A_GROUPS
    for n in range(x.shape[0] // A_CHUNK):
        rs = slice(n * A_CHUNK, (n + 1) * A_CHUNK)
        for g in range(A_GROUPS):
            cs = slice(g * gdim, (g + 1) * gdim)
            mixed = _dot(sguw_ref[g], vnb[rs, cs]) + sgub_ref[:, cs]
            sa_ref[rs, cs] = (u[rs, cs] * mixed).astype(BF16)
    ya = _dot(sa_ref[...], wa_ref[...])
    gates = _dot(hb, wgates_ref[...]) + bbr_ref[...]
    yag_ref[...] = (jax.nn.sigmoid(gates[:, 0:d]) * ya).astype(BF16)
    gb_ref[...] = jax.nn.sigmoid(gates[:, d:2 * d]).astype(BF16)

    o = 2 * d
    head_k = dk // GLA_HEADS
    q_ref[...] = (_dot(hb, wmain_ref[:, o:o + dk]) * (head_k ** -0.5)).astype(BF16)
    k_ref[...] = _dot(hb, wmain_ref[:, o + dk:o + 2 * dk]).astype(BF16)
    o = o + 2 * dk
    v_ref[...] = _dot(hb, wmain_ref[:, o:o + d]).astype(BF16)
    r = _dot(hb, wmain_ref[:, o + d:o + 2 * d])
    sr_ref[...] = (r * jax.nn.sigmoid(r)).astype(BF16)
    lr = _dot(hb, wlr_ref[...]).astype(BF16)
    z = _dot(lr, wg2_ref[...]) + bg2_ref[...]
    la_ref[...] = _log_sigmoid(z) * (1.0 / GLA_TAU)


def _inproj(x_all, mod3, g1, wmain, wlr, wgates, wg2, bg2, lng, lnb, sguw, sgub, wa, bbr, n_batch):
    nt, d = x_all.shape
    dk = wg2.shape[1] // 2
    nj = nt // n_batch // TM
    ctx_row = n_batch

    def row(b, j):
        return (b * nj + j, 0)

    def modrow(b, j):
        return (jnp.where(j == 0, ctx_row, b), 0, 0)

    tile = lambda w: pl.BlockSpec((TM, w), row)
    outs = [(d, BF16), (d, BF16), (dk, BF16), (dk, BF16), (d, BF16), (d, BF16), (2 * dk, F32)]
    return pl.pallas_call(
        _inproj_kernel,
        grid=(n_batch, nj),
        in_specs=[tile(d), pl.BlockSpec((None, 1, mod3.shape[2]), modrow)]
        + [_const_spec(a.shape) for a in (g1, wmain, wlr, wgates, wg2, bg2, lng, lnb, sguw, sgub, wa, bbr)],
        out_specs=[tile(w) for w, _ in outs],
        out_shape=[jax.ShapeDtypeStruct((nt, w), t) for w, t in outs],
        scratch_shapes=[pltpu.VMEM((TM, d), BF16)],
        compiler_params=_params(("arbitrary", "arbitrary")),
        name="inproj",
    )(x_all, mod3, g1, wmain, wlr, wgates, wg2, bg2, lng, lnb, sguw, sgub, wa, bbr)


def _gla_tile(q_ref, k_ref, v_ref, la_ref, s_ref, o_ref, reverse):
    rows, dk = q_ref.shape
    dv = v_ref.shape[1]
    hk = dk // GLA_HEADS
    hv = dv // GLA_HEADS
    c = GLA_CHUNK
    ri = lax.broadcasted_iota(jnp.int32, (c, c), 0)
    ci = lax.broadcasted_iota(jnp.int32, (c, c), 1)
    tri = (ci >= ri) if reverse else (ci <= ri)
    tri_f = tri.astype(F32)
    chunks = range(rows // c)
    for n in (reversed(chunks) if reverse else chunks):
        rs = slice(n * c, (n + 1) * c)
        la = la_ref[rs, :]
        b = jnp.dot(tri_f, la, preferred_element_type=F32, precision=HIGHEST)
        b_end = b[0:1, :] if reverse else b[c - 1:c, :]
        q = q_ref[rs, :].astype(F32)
        k = k_ref[rs, :].astype(F32)
        q_in = (q * jnp.exp(b)).astype(BF16)
        k_in = (k * jnp.exp(-b)).astype(BF16)
        k_st = (k * jnp.exp(b_end - b)).astype(BF16)
        decay = jnp.exp(b_end)
        for h in range(GLA_HEADS):
            ks = slice(h * hk, (h + 1) * hk)
            vs = slice(h * hv, (h + 1) * hv)
            vh = v_ref[rs, vs]
            att = lax.dot_general(q_in[:, ks], k_in[:, ks], (((1,), (1,)), ((), ())),
                                  preferred_element_type=F32)
            att = jnp.where(tri, att, 0.0).astype(BF16)
            state = s_ref[h]
            o_ref[rs, vs] = _dot(att, vh) + _dot(q_in[:, ks], state.astype(BF16))
            dcol = jnp.transpose(jnp.broadcast_to(decay[:, ks], (hk, hk)))
            dmat = jnp.concatenate([dcol] * (hv // hk), axis=1)
            kv = lax.dot_general(k_st[:, ks], vh, (((0,), (0,)), ((), ())), preferred_element_type=F32)
            s_ref[h] = dmat * state + kv


def _gla_bwd_kernel(q_ref, k_ref, v_ref, la_ref, ob_ref, s_ref, o_sc):
    @pl.when(pl.program_id(1) == 0)
    def _():
        s_ref[...] = jnp.zeros_like(s_ref)

    _gla_tile(q_ref, k_ref, v_ref, la_ref, s_ref, o_sc, reverse=True)
    ob_ref[...] = o_sc[...].astype(BF16)


def _gla_bwd(q, k, v, la, n_batch):
    nt, dk = q.shape
    dv = v.shape[1]
    nj = nt // n_batch // TM

    def row(b, jj):
        return (b * nj + jnp.where(jj == 0, 0, nj - jj), 0)

    def row_la(b, jj):
        return (b * nj + jnp.where(jj == 0, 0, nj - jj), 1)

    return pl.pallas_call(
        _gla_bwd_kernel,
        grid=(n_batch, nj),
        in_specs=[pl.BlockSpec((TM, dk), row), pl.BlockSpec((TM, dk), row), pl.BlockSpec((TM, dv), row),
                  pl.BlockSpec((TM, dk), row_la)],
        out_specs=pl.BlockSpec((TM, dv), row),
        out_shape=jax.ShapeDtypeStruct((nt, dv), BF16),
        scratch_shapes=[pltpu.VMEM((GLA_HEADS, dk // GLA_HEADS, dv // GLA_HEADS), F32),
                        pltpu.VMEM((TM, dv), F32)],
        compiler_params=_params(("arbitrary", "arbitrary")),
        name="gla_bwd",
    )(q, k, v, la)


def _route(logits):
    lane = lax.broadcasted_iota(jnp.int32, logits.shape, 1)
    valid = lane < N_EXPERTS
    lg = jnp.where(valid, logits, -1e30)
    ex = jnp.where(valid, jnp.exp(lg - jnp.max(lg, axis=-1, keepdims=True)), 0.0)
    p = ex / jnp.sum(ex, axis=-1, keepdims=True)
    grp = lane // EXPERTS_PER_GROUP
    none = -1.0
    far = 2 * LANES
    best = None
    for g in range(N_EXPERT_GROUPS):
        pg = jnp.where(grp == g, p, none)
        m1 = jnp.max(pg, axis=-1, keepdims=True)
        i1 = jnp.min(jnp.where(pg == m1, lane, far), axis=-1, keepdims=True)
        pg2 = jnp.where(lane == i1, none, pg)
        m2 = jnp.max(pg2, axis=-1, keepdims=True)
        i2 = jnp.min(jnp.where(pg2 == m2, lane, far), axis=-1, keepdims=True)
        cand = (m1 + m2, m1, i1, m2, i2)
        if best is None:
            best = cand
        else:
            better = cand[0] > best[0]
            best = tuple(jnp.where(better, c, o) for c, o in zip(cand, best))
    _, m1, i1, m2, i2 = best
    tot = m1 + m2
    slab = jnp.where(lane == 0, i1.astype(F32), 0.0)
    slab = jnp.where(lane == 1, i2.astype(F32), slab)
    slab = jnp.where(lane == 2, m1 / tot, slab)
    slab = jnp.where(lane == 3, m2 / tot, slab)
    return slab


def _merge_kernel(x_ref, mod_ref, q_ref, k_ref, v_ref, la_ref, ob_ref, yag_ref, gb_ref, sr_ref,
                  glag_ref, wb_ref, wout_ref, g2_ref, wr_ref, br_ref,
                  x1_ref, h2_ref, route_ref, s_ref, o_sc):
    @pl.when(pl.program_id(1) == 0)
    def _():
        s_ref[...] = jnp.zeros_like(s_ref)

    _gla_tile(q_ref, k_ref, v_ref, la_ref, s_ref, o_sc, reverse=False)

    d = x_ref.shape[1]
    hv = d // GLA_HEADS
    o = o_sc[...] + ob_ref[...].astype(F32)
    parts = []
    for h in range(GLA_HEADS):
        oh = o[:, h * hv:(h + 1) * hv]
        parts.append(oh * lax.rsqrt(jnp.mean(oh * oh, axis=-1, keepdims=True) + EPS))
    on = jnp.concatenate(parts, axis=1) * glag_ref[...]
    yb = _dot((on * sr_ref[...].astype(F32)).astype(BF16), wb_ref[...])
    m = yag_ref[...].astype(F32) + gb_ref[...].astype(F32) * yb
    y = _dot(m.astype(BF16), wout_ref[...])
    gt1 = mod_ref[:, 2 * d:3 * d]
    sh2 = mod_ref[:, 3 * d:4 * d]
    sc2 = mod_ref[:, 4 * d:5 * d]
    x1 = x_ref[...] + gt1 * y
    x1_ref[...] = x1
    h2 = x1 * lax.rsqrt(jnp.mean(x1 * x1, axis=-1, keepdims=True) + EPS) * g2_ref[...]
    h2 = h2 * (1.0 + sc2) + sh2
    h2_ref[...] = h2
    logits = jnp.dot(h2, wr_ref[...], preferred_element_type=F32, precision=HIGHEST) + br_ref[...]
    route_ref[...] = _route(logits)


def _merge(x_all, mod3, q, k, v, la, ob, yag, gb, sr, glag, wb, wout, g2, wr, br, n_batch):
    nt, d = x_all.shape
    dk = q.shape[1]
    nj = nt // n_batch // TM
    ctx_row = n_batch

    def row(b, j):
        return (b * nj + j, 0)

    def modrow(b, j):
        return (jnp.where(j == 0, ctx_row, b), 0, 0)

    tile = lambda w: pl.BlockSpec((TM, w), row)
    return pl.pallas_call(
        _merge_kernel,
        grid=(n_batch, nj),
        in_specs=[tile(d), pl.BlockSpec((None, 1, mod3.shape[2]), modrow),
                  tile(dk), tile(dk), tile(d), tile(dk), tile(d), tile(d), tile(d), tile(d)]
        + [_const_spec(a.shape) for a in (glag, wb, wout, g2, wr, br)],
        out_specs=[tile(d), tile(d), tile(LANES)],
        out_shape=[jax.ShapeDtypeStruct((nt, d), F32), jax.ShapeDtypeStruct((nt, d), F32),
                   jax.ShapeDtypeStruct((nt, LANES), F32)],
        scratch_shapes=[pltpu.VMEM((GLA_HEADS, dk // GLA_HEADS, d // GLA_HEADS), F32),
                        pltpu.VMEM((TM, d), F32)],
        compiler_params=_params(("arbitrary", "arbitrary")),
        name="merge",
    )(x_all, mod3, q, k, v, la, ob, yag, gb, sr, glag, wb, wout, g2, wr, br)


def _rank_kernel(route_ref, rank_ref, cnt_ref, base_sc):
    @pl.when(pl.program_id(0) == 0)
    def _():
        base_sc[...] = jnp.zeros_like(base_sc)

    rows = route_ref.shape[0]
    lane = lax.broadcasted_iota(jnp.int32, (rows, LANES), 1)
    lane_f = lane.astype(F32)
    oh0 = jnp.where(lane_f == route_ref[:, 0:1], 1.0, 0.0)
    oh1 = jnp.where(lane_f == route_ref[:, 1:2], 1.0, 0.0)
    ri = lax.broadcasted_iota(jnp.int32, (rows, rows), 0)
    ci = lax.broadcasted_iota(jnp.int32, (rows, rows), 1)
    before = jnp.where(ci < ri, 1.0, 0.0).astype(BF16)
    c0 = _dot(before, oh0.astype(BF16))
    c1 = _dot(before, oh1.astype(BF16))
    base = base_sc[0:1, :]
    tot0 = jnp.sum(oh0, axis=0, keepdims=True)
    tot1 = jnp.sum(oh1, axis=0, keepdims=True)
    r0 = jnp.sum(oh0 * (c0 + base), axis=-1, keepdims=True)
    r1 = jnp.sum(oh1 * (c1 + base + tot0), axis=-1, keepdims=True)
    new_base = base + tot0 + tot1
    base_sc[...] = jnp.broadcast_to(new_base, base_sc.shape)
    cnt_ref[...] = jnp.broadcast_to(new_base, cnt_ref.shape)
    rank_ref[...] = jnp.where(lane == 0, r0, jnp.where(lane == 1, r1, 0.0))


def _rank(route):
    nt = route.shape[0]
    return pl.pallas_call(
        _rank_kernel,
        grid=(nt // TM,),
        in_specs=[pl.BlockSpec((TM, LANES), lambda i: (i, 0))],
        out_specs=[pl.BlockSpec((TM, LANES), lambda i: (i, 0)), pl.BlockSpec((8, LANES), lambda i: (0, 0))],
        out_shape=[jax.ShapeDtypeStruct((nt, LANES), F32), jax.ShapeDtypeStruct((8, LANES), F32)],
        scratch_shapes=[pltpu.VMEM((8, LANES), F32)],
        compiler_params=_params(("arbitrary",)),
        name="rank",
    )(route)


def _issue_row_gather(idx_ref, n_rows, src_hbm, dst, sem):
    def body(r, carry):
        t = idx_ref[0, 0, r]
        pltpu.make_async_copy(src_hbm.at[pl.ds(t, 1)], dst.at[pl.ds(r, 1)], sem).start()
        return carry

    lax.fori_loop(0, n_rows, body, 0)


def _wait_row_gather(n_rows, src_hbm, dst, sem):
    pltpu.make_async_copy(src_hbm.at[pl.ds(0, n_rows)], dst, sem).wait()


def _moe_kernel(be_ref, tokc_ref, tokn_ref, h_hbm, wg_ref, wu_ref, wd_ref, y_ref, xbuf, sem):
    del be_ref
    i = pl.program_id(0)
    nb = pl.num_programs(0)
    slot = lax.rem(i, 2)

    @pl.when(i == 0)
    def _():
        _issue_row_gather(tokc_ref, TB, h_hbm, xbuf.at[0], sem.at[0])

    @pl.when(i + 1 < nb)
    def _():
        _issue_row_gather(tokn_ref, TB, h_hbm, xbuf.at[1 - slot], sem.at[1 - slot])

    _wait_row_gather(TB, h_hbm, xbuf.at[slot], sem.at[slot])
    x = xbuf[slot].astype(BF16)
    g = _dot(x, wg_ref[0])
    u = _dot(x, wu_ref[0])
    a = (g * jax.nn.sigmoid(g) * u).astype(BF16)
    y_ref[...] = _dot(a, wd_ref[0])


def _moe(block_e, buf_tok3, h2, wg, wu, wd):
    nb = buf_tok3.shape[0]
    d = h2.shape[1]
    de = wg.shape[2]
    idx_spec = lambda f: pl.BlockSpec((1, 1, TB), f, memory_space=pltpu.SMEM)
    return pl.pallas_call(
        _moe_kernel,
        grid_spec=pltpu.PrefetchScalarGridSpec(
            num_scalar_prefetch=1,
            grid=(nb,),
            in_specs=[idx_spec(lambda i, be: (i, 0, 0)),
                      idx_spec(lambda i, be: (jnp.minimum(i + 1, nb - 1), 0, 0)),
                      pl.BlockSpec(memory_space=pl.ANY),
                      pl.BlockSpec((1, d, de), lambda i, be: (be[i], 0, 0)),
                      pl.BlockSpec((1, d, de), lambda i, be: (be[i], 0, 0)),
                      pl.BlockSpec((1, de, d), lambda i, be: (be[i], 0, 0))],
            out_specs=pl.BlockSpec((TB, d), lambda i, be: (i, 0)),
            scratch_shapes=[pltpu.VMEM((2, TB, d), F32), pltpu.SemaphoreType.DMA((2,))],
        ),
        out_shape=jax.ShapeDtypeStruct((nb * TB, d), F32),
        compiler_params=_params(("arbitrary",)),
        name="moe",
    )(block_e, buf_tok3, buf_tok3, h2, wg, wu, wd)


def _combine_kernel(destc_ref, destn_ref, y_hbm, x1_ref, route_ref, mod_ref, gf_ref, o_ref, ybuf, sem,
                    *, final_norm):
    i = pl.program_id(0)
    nb = pl.num_programs(0)
    slot = lax.rem(i, 2)
    rows, d = x1_ref.shape

    @pl.when(i == 0)
    def _():
        _issue_row_gather(destc_ref, 2 * rows, y_hbm, ybuf.at[0], sem.at[0])

    @pl.when(i + 1 < nb)
    def _():
        _issue_row_gather(destn_ref, 2 * rows, y_hbm, ybuf.at[1 - slot], sem.at[1 - slot])

    _wait_row_gather(2 * rows, y_hbm, ybuf.at[slot], sem.at[slot])
    w0 = route_ref[:, 2:3]
    w1 = route_ref[:, 3:4]
    y = w0 * ybuf[slot, 0:rows, :] + w1 * ybuf[slot, rows:2 * rows, :]
    gt2 = mod_ref[:, 5 * d:6 * d]
    x2 = x1_ref[...] + gt2 * y
    if final_norm:
        x2 = x2 * lax.rsqrt(jnp.mean(x2 * x2, axis=-1, keepdims=True) + EPS) * gf_ref[...]
    o_ref[...] = x2


def _combine(dest3, y_sorted, x1, route, mod3, gf, n_batch, final_norm):
    nt, d = x1.shape
    nj = nt // n_batch // TM
    n_tiles = nt // TM
    ctx_row = n_batch

    def modrow(i):
        return (jnp.where(i % nj == 0, ctx_row, i // nj), 0, 0)

    if final_norm:
        out_rows = nt - n_batch * TM
        out_map = lambda i: ((i // nj) * (nj - 1) + jnp.maximum(i % nj - 1, 0), 0)
    else:
        out_rows = nt
        out_map = lambda i: (i, 0)
    idx_spec = lambda f: pl.BlockSpec((1, 1, 2 * TM), f, memory_space=pltpu.SMEM)
    return pl.pallas_call(
        functools.partial(_combine_kernel, final_norm=final_norm),
        grid=(n_tiles,),
        in_specs=[idx_spec(lambda i: (i, 0, 0)),
                  idx_spec(lambda i: (jnp.minimum(i + 1, n_tiles - 1), 0, 0)),
                  pl.BlockSpec(memory_space=pl.ANY),
                  pl.BlockSpec((TM, d), lambda i: (i, 0)),
                  pl.BlockSpec((TM, LANES), lambda i: (i, 0)),
                  pl.BlockSpec((None, 1, mod3.shape[2]), modrow),
                  _const_spec(gf.shape)],
        out_specs=pl.BlockSpec((TM, d), out_map),
        out_shape=jax.ShapeDtypeStruct((out_rows, d), F32),
        scratch_shapes=[pltpu.VMEM((2, 2 * TM, d), F32), pltpu.SemaphoreType.DMA((2,))],
        compiler_params=_params(("arbitrary",)),
        name="combine",
    )(dest3, dest3, y_sorted, x1, route, mod3, gf)


def _moe_plan(route, rank, cnt):
    nt = route.shape[0]
    e0 = route[:, 0].astype(jnp.int32)
    e1 = route[:, 1].astype(jnp.int32)
    counts = cnt[0, :N_EXPERTS].astype(jnp.int32)
    padded = (counts + TB - 1) // TB * TB
    p_ends = jnp.cumsum(padded)
    p_starts = p_ends - padded
    dest0 = p_starts[e0] + rank[:, 0].astype(jnp.int32)
    dest1 = p_starts[e1] + rank[:, 1].astype(jnp.int32)
    nb = (2 * nt) // TB + N_EXPERTS
    tok = jnp.arange(nt, dtype=jnp.int32)
    buf_tok = jnp.zeros((nb * TB,), jnp.int32).at[dest0].set(tok).at[dest1].set(tok)
    block_e = jnp.minimum(
        jnp.searchsorted(p_ends, jnp.arange(nb, dtype=jnp.int32) * TB, side="right"), N_EXPERTS - 1
    ).astype(jnp.int32)
    dest3 = jnp.concatenate([dest0.reshape(-1, TM), dest1.reshape(-1, TM)], axis=1).reshape(-1, 1, 2 * TM)
    return block_e, buf_tok.reshape(nb, 1, TB), dest3


def kernel(x, c, ctx, c_ctx, w_mod, b_mod, g_norm1, g_norm2, w_in, w_gate2, b_gate2, gla_norm_g,
           sgu_ln_g, sgu_ln_b, sgu_w, sgu_b, w_branch_a, w_branch_b, b_branch, w_out,
           w_router, b_router, w_exp_gate, w_exp_up, w_exp_down, g_final):
    n_batch, seq, d = x.shape
    ctx_len = ctx.shape[1]
    depth = w_mod.shape[0]
    dk = w_gate2.shape[3]
    rank_lr = w_gate2.shape[2]
    assert ctx_len == TM and seq % TM == 0 and n_batch < MOD_ROWS

    x_all = jnp.concatenate([ctx, x], axis=1).reshape(-1, d)
    cc = jnp.zeros((MOD_ROWS, d), F32).at[:n_batch].set(c).at[n_batch].set(c_ctx)
    mod = _modulation(cc, w_mod, b_mod)

    wr = jnp.zeros((d, LANES), F32).at[:, :N_EXPERTS].set(w_router)
    br = jnp.zeros((1, LANES), F32).at[0, :N_EXPERTS].set(b_router)
    row = lambda a: a.reshape(1, -1)
    n_main = 2 * d + 2 * dk + 2 * d

    out = None
    for l in range(depth):
        last = l == depth - 1
        mod3 = mod[l].reshape(MOD_ROWS, 1, 6 * d)
        wmain = w_in[l][:, :n_main].astype(BF16)
        wlr = w_in[l][:, n_main:n_main + 2 * rank_lr].astype(BF16)
        wgates = w_in[l][:, n_main + 2 * rank_lr:].astype(BF16)
        wg2 = jnp.zeros((2 * rank_lr, 2 * dk), F32)
        wg2 = wg2.at[:rank_lr, :dk].set(w_gate2[l, 0]).at[rank_lr:, dk:].set(w_gate2[l, 1]).astype(BF16)
        bg2 = b_gate2[l].reshape(1, 2 * dk)
        sgub = jnp.repeat(sgu_b[l].T, d // A_GROUPS, axis=1)

        yag, gb, q, k, v, sr, la = _inproj(
            x_all, mod3, row(g_norm1[l]), wmain, wlr, wgates, wg2, bg2, row(sgu_ln_g[l]), row(sgu_ln_b[l]),
            sgu_w[l].astype(BF16), sgub, w_branch_a[l].astype(BF16), row(b_branch[l]), n_batch)
        ob = _gla_bwd(q, k, v, la, n_batch)
        x1, h2, route = _merge(
            x_all, mod3, q, k, v, la, ob, yag, gb, sr, row(gla_norm_g[l]), w_branch_b[l].astype(BF16),
            w_out[l].astype(BF16), row(g_norm2[l]), wr, br, n_batch)
        rank, cnt = _rank(route)
        block_e, buf_tok3, dest3 = _moe_plan(route, rank, cnt)
        y_sorted = _moe(block_e, buf_tok3, h2, w_exp_gate[l].astype(BF16), w_exp_up[l].astype(BF16),
                        w_exp_down[l].astype(BF16))
        res = _combine(dest3, y_sorted, x1, route, mod3, row(g_final), n_batch, final_norm=last)
        if last:
            out = res.reshape(n_batch, seq, d)
        else:
            x_all = res
    return out
```

```python
import functools

import jax
import jax.numpy as jnp
from jax import lax
from jax.experimental import pallas as pl
from jax.experimental.pallas import tpu as pltpu

F32 = jnp.float32
BF16 = jnp.bfloat16
HIGHEST = lax.Precision.HIGHEST

A_CHUNK = 128
A_GROUPS = 8
GLA_HEADS = 4
GLA_TAU = 16.0
GLA_CHUNK = 64
N_EXPERTS = 16
N_EXPERT_GROUPS = 4
EXPERTS_PER_GROUP = N_EXPERTS // N_EXPERT_GROUPS
TOP_K = 2
EPS = 1e-6

LANES = 128
SUBLANES = 8
TM = 256
TB = 256
MOD_ROWS = 16
VMEM_LIMIT = 56 * 1024 * 1024


def _dot(a, b):
    return jnp.dot(a, b, preferred_element_type=F32)


def _const_spec(shape):
    nd = len(shape)
    return pl.BlockSpec(shape, lambda *_: (0,) * nd)


def _params(sem):
    return pltpu.CompilerParams(dimension_semantics=sem, vmem_limit_bytes=VMEM_LIMIT)


def _mod_kernel(cc_ref, w_ref, b_ref, o_ref):
    cc = cc_ref[...]
    s = cc * jax.nn.sigmoid(cc)
    o_ref[0] = jnp.dot(s, w_ref[0], preferred_element_type=F32, precision=HIGHEST) + b_ref[0]


def _modulation(cc, w_mod, b_mod):
    n_layer, d, six_d = w_mod.shape
    return pl.pallas_call(
        _mod_kernel,
        grid=(n_layer, six_d // d),
        in_specs=[
            pl.BlockSpec((MOD_ROWS, d), lambda l, j: (0, 0)),
            pl.BlockSpec((1, d, d), lambda l, j: (l, 0, j)),
            pl.BlockSpec((1, 1, d), lambda l, j: (l, 0, j)),
        ],
        out_specs=pl.BlockSpec((1, MOD_ROWS, d), lambda l, j: (l, 0, j)),
        out_shape=jax.ShapeDtypeStruct((n_layer, MOD_ROWS, six_d), F32),
        compiler_params=_params(("arbitrary", "arbitrary")),
        name="modulation",
    )(cc, w_mod, b_mod.reshape(n_layer, 1, six_d))


def _gelu_tanh(x):
    c = 0.7978845608028654
    return 0.5 * x * (1.0 + jnp.tanh(c * (x + 0.044715 * (x * x * x))))


def _log_sigmoid(z):
    return jnp.minimum(z, 0.0) - jnp.log1p(jnp.exp(-jnp.abs(z)))


def _inproj_kernel(x_ref, mod_ref, g1_ref, wmain_ref, wlr_ref, wgates_ref, wg2_ref, bg2_ref,
                   lng_ref, lnb_ref, sguw_ref, sgub_ref, wa_ref, bbr_ref,
                   yag_ref, gb_ref, q_ref, k_ref, v_ref, sr_ref, la_ref, sa_ref):
    d = x_ref.shape[1]
    dk = q_ref.shape[1]
    x = x_ref[...]
    sh1 = mod_ref[:, 0:d]
    sc1 = mod_ref[:, d:2 * d]
    h = x * lax.rsqrt(jnp.mean(x * x, axis=-1, keepdims=True) + EPS) * g1_ref[...]
    hb = (h * (1.0 + sc1) + sh1).astype(BF16)

    u = _gelu_tanh(_dot(hb, wmain_ref[:, 0:d]))
    vv = _gelu_tanh(_dot(hb, wmain_ref[:, d:2 * d]))
    mu = jnp.mean(vv, axis=-1, keepdims=True)
    vc = vv - mu
    vn = vc * lax.rsqrt(jnp.mean(vc * vc, axis=-1, keepdims=True) + EPS) * lng_ref[...] + lnb_ref[...]
    vnb = vn.astype(BF16)
    gdim = d // A_GROUPS
    for n in range(x.shape[0] // A_CHUNK):
        rs = slice(n * A_CHUNK, (n + 1) * A_CHUNK)
        for g in range(A_GROUPS):
            cs = slice(g * gdim, (g + 1) * gdim)
            mixed = _dot(sguw_ref[g], vnb[rs, cs]) + sgub_ref[:, cs]
            sa_ref[rs, cs] = (u[rs, cs] * mixed).astype(BF16)
    ya = _dot(sa_ref[...], wa_ref[...])
    gates = _dot(hb, wgates_ref[...]) + bbr_ref[...]
    yag_ref[...] = (jax.nn.sigmoid(gates[:, 0:d]) * ya).astype(BF16)
    gb_ref[...] = jax.nn.sigmoid(gates[:, d:2 * d]).astype(BF16)

    o = 2 * d
    head_k = dk // GLA_HEADS
    q_ref[...] = (_dot(hb, wmain_ref[:, o:o + dk]) * (head_k ** -0.5)).astype(BF16)
    k_ref[...] = _dot(hb, wmain_ref[:, o + dk:o + 2 * dk]).astype(BF16)
    o = o + 2 * dk
    v_ref[...] = _dot(hb, wmain_ref[:, o:o + d]).astype(BF16)
    r = _dot(hb, wmain_ref[:, o + d:o + 2 * d])
    sr_ref[...] = (r * jax.nn.sigmoid(r)).astype(BF16)
    lr = _dot(hb, wlr_ref[...]).astype(BF16)
    z = _dot(lr, wg2_ref[...]) + bg2_ref[...]
    la_ref[...] = _log_sigmoid(z) * (1.0 / GLA_TAU)


def _inproj(x_all, mod3, g1, wmain, wlr, wgates, wg2, bg2, lng, lnb, sguw, sgub, wa, bbr, n_batch):
    nt, d = x_all.shape
    dk = wg2.shape[1] // 2
    nj = nt // n_batch // TM
    ctx_row = n_batch

    def row(b, j):
        return (b * nj + j, 0)

    def modrow(b, j):
        return (jnp.where(j == 0, ctx_row, b), 0, 0)

    tile = lambda w: pl.BlockSpec((TM, w), row)
    outs = [(d, BF16), (d, BF16), (dk, BF16), (dk, BF16), (d, BF16), (d, BF16), (2 * dk, F32)]
    return pl.pallas_call(
        _inproj_kernel,
        grid=(n_batch, nj),
        in_specs=[tile(d), pl.BlockSpec((None, 1, mod3.shape[2]), modrow)]
        + [_const_spec(a.shape) for a in (g1, wmain, wlr, wgates, wg2, bg2, lng, lnb, sguw, sgub, wa, bbr)],
        out_specs=[tile(w) for w, _ in outs],
        out_shape=[jax.ShapeDtypeStruct((nt, w), t) for w, t in outs],
        scratch_shapes=[pltpu.VMEM((TM, d), BF16)],
        compiler_params=_params(("arbitrary", "arbitrary")),
        name="inproj",
    )(x_all, mod3, g1, wmain, wlr, wgates, wg2, bg2, lng, lnb, sguw, sgub, wa, bbr)


def _gla_tile(q_ref, k_ref, v_ref, la_ref, s_ref, o_ref, reverse):
    rows, dk = q_ref.shape
    dv = v_ref.shape[1]
    hk = dk // GLA_HEADS
    hv = dv // GLA_HEADS
    c = GLA_CHUNK
    ri = lax.broadcasted_iota(jnp.int32, (c, c), 0)
    ci = lax.broadcasted_iota(jnp.int32, (c, c), 1)
    tri = (ci >= ri) if reverse else (ci <= ri)
    tri_f = tri.astype(F32)
    chunks = range(rows // c)
    for n in (reversed(chunks) if reverse else chunks):
        rs = slice(n * c, (n + 1) * c)
        la = la_ref[rs, :]
        b = jnp.dot(tri_f, la, preferred_element_type=F32, precision=HIGHEST)
        b_end = b[0:1, :] if reverse else b[c - 1:c, :]
        q = q_ref[rs, :].astype(F32)
        k = k_ref[rs, :].astype(F32)
        q_in = (q * jnp.exp(b)).astype(BF16)
        k_in = (k * jnp.exp(-b)).astype(BF16)
        k_st = (k * jnp.exp(b_end - b)).astype(BF16)
        decay = jnp.exp(b_end)
        for h in range(GLA_HEADS):
            ks = slice(h * hk, (h + 1) * hk)
            vs = slice(h * hv, (h + 1) * hv)
            vh = v_ref[rs, vs]
            att = lax.dot_general(q_in[:, ks], k_in[:, ks], (((1,), (1,)), ((), ())),
                                  preferred_element_type=F32)
            att = jnp.where(tri, att, 0.0).astype(BF16)
            state = s_ref[h]
            o_ref[rs, vs] = _dot(att, vh) + _dot(q_in[:, ks], state.astype(BF16))
            dcol = jnp.transpose(jnp.broadcast_to(decay[:, ks], (hk, hk)))
            dmat = jnp.concatenate([dcol] * (hv // hk), axis=1)
            kv = lax.dot_general(k_st[:, ks], vh, (((0,), (0,)), ((), ())), preferred_element_type=F32)
            s_ref[h] = dmat * state + kv


def _gla_bwd_kernel(q_ref, k_ref, v_ref, la_ref, ob_ref, s_ref, o_sc):
    @pl.when(pl.program_id(1) == 0)
    def _():
        s_ref[...] = jnp.zeros_like(s_ref)

    _gla_tile(q_ref, k_ref, v_ref, la_ref, s_ref, o_sc, reverse=True)
    ob_ref[...] = o_sc[...].astype(BF16)


def _gla_bwd(q, k, v, la, n_batch):
    nt, dk = q.shape
    dv = v.shape[1]
    nj = nt // n_batch // TM

    def row(b, jj):
        return (b * nj + jnp.where(jj == 0, 0, nj - jj), 0)

    def row_la(b, jj):
        return (b * nj + jnp.where(jj == 0, 0, nj - jj), 1)

    return pl.pallas_call(
        _gla_bwd_kernel,
        grid=(n_batch, nj),
        in_specs=[pl.BlockSpec((TM, dk), row), pl.BlockSpec((TM, dk), row), pl.BlockSpec((TM, dv), row),
                  pl.BlockSpec((TM, dk), row_la)],
        out_specs=pl.BlockSpec((TM, dv), row),
        out_shape=jax.ShapeDtypeStruct((nt, dv), BF16),
        scratch_shapes=[pltpu.VMEM((GLA_HEADS, dk // GLA_HEADS, dv // GLA_HEADS), F32),
                        pltpu.VMEM((TM, dv), F32)],
        compiler_params=_params(("arbitrary", "arbitrary")),
        name="gla_bwd",
    )(q, k, v, la)


def _route(logits):
    lane = lax.broadcasted_iota(jnp.int32, logits.shape, 1)
    valid = lane < N_EXPERTS
    lg = jnp.where(valid, logits, -1e30)
    ex = jnp.where(valid, jnp.exp(lg - jnp.max(lg, axis=-1, keepdims=True)), 0.0)
    p = ex / jnp.sum(ex, axis=-1, keepdims=True)
    grp = lane // EXPERTS_PER_GROUP
    none = -1.0
    far = 2 * LANES
    best = None
    for g in range(N_EXPERT_GROUPS):
        pg = jnp.where(grp == g, p, none)
        m1 = jnp.max(pg, axis=-1, keepdims=True)
        i1 = jnp.min(jnp.where(pg == m1, lane, far), axis=-1, keepdims=True)
        pg2 = jnp.where(lane == i1, none, pg)
        m2 = jnp.max(pg2, axis=-1, keepdims=True)
        i2 = jnp.min(jnp.where(pg2 == m2, lane, far), axis=-1, keepdims=True)
        cand = (m1 + m2, m1, i1, m2, i2)
        if best is None:
            best = cand
        else:
            better = cand[0] > best[0]
            best = tuple(jnp.where(better, c, o) for c, o in zip(cand, best))
    _, m1, i1, m2, i2 = best
    tot = m1 + m2
    slab = jnp.where(lane == 0, i1.astype(F32), 0.0)
    slab = jnp.where(lane == 1, i2.astype(F32), slab)
    slab = jnp.where(lane == 2, m1 / tot, slab)
    slab = jnp.where(lane == 3, m2 / tot, slab)
    return slab


def _merge_kernel(x_ref, mod_ref, q_ref, k_ref, v_ref, la_ref, ob_ref, yag_ref, gb_ref, sr_ref,
                  glag_ref, wb_ref, wout_ref, g2_ref, wr_ref, br_ref,
                  x1_ref, h2_ref, route_ref, routet_ref, s_ref, o_sc):
    @pl.when(pl.program_id(1) == 0)
    def _():
        s_ref[...] = jnp.zeros_like(s_ref)

    _gla_tile(q_ref, k_ref, v_ref, la_ref, s_ref, o_sc, reverse=False)

    d = x_ref.shape[1]
    hv = d // GLA_HEADS
    o = o_sc[...] + ob_ref[...].astype(F32)
    parts = []
    for h in range(GLA_HEADS):
        oh = o[:, h * hv:(h + 1) * hv]
        parts.append(oh * lax.rsqrt(jnp.mean(oh * oh, axis=-1, keepdims=True) + EPS))
    on = jnp.concatenate(parts, axis=1) * glag_ref[...]
    yb = _dot((on * sr_ref[...].astype(F32)).astype(BF16), wb_ref[...])
    m = yag_ref[...].astype(F32) + gb_ref[...].astype(F32) * yb
    y = _dot(m.astype(BF16), wout_ref[...])
    gt1 = mod_ref[:, 2 * d:3 * d]
    sh2 = mod_ref[:, 3 * d:4 * d]
    sc2 = mod_ref[:, 4 * d:5 * d]
    x1 = x_ref[...] + gt1 * y
    x1_ref[...] = x1
    h2 = x1 * lax.rsqrt(jnp.mean(x1 * x1, axis=-1, keepdims=True) + EPS) * g2_ref[...]
    h2 = h2 * (1.0 + sc2) + sh2
    h2_ref[...] = h2
    logits = jnp.dot(h2, wr_ref[...], preferred_element_type=F32, precision=HIGHEST) + br_ref[...]
    slab = _route(logits)
    route_ref[...] = slab
    routet_ref[0] = jnp.transpose(slab)[0:SUBLANES, :]


def _merge(x_all, mod3, q, k, v, la, ob, yag, gb, sr, glag, wb, wout, g2, wr, br, n_batch):
    nt, d = x_all.shape
    dk = q.shape[1]
    nj = nt // n_batch // TM
    ctx_row = n_batch

    def row(b, j):
        return (b * nj + j, 0)

    def modrow(b, j):
        return (jnp.where(j == 0, ctx_row, b), 0, 0)

    tile = lambda w: pl.BlockSpec((TM, w), row)
    return pl.pallas_call(
        _merge_kernel,
        grid=(n_batch, nj),
        in_specs=[tile(d), pl.BlockSpec((None, 1, mod3.shape[2]), modrow),
                  tile(dk), tile(dk), tile(d), tile(dk), tile(d), tile(d), tile(d), tile(d)]
        + [_const_spec(a.shape) for a in (glag, wb, wout, g2, wr, br)],
        out_specs=[tile(d), tile(d), tile(LANES),
                   pl.BlockSpec((1, SUBLANES, TM), lambda b, j: (b * nj + j, 0, 0))],
        out_shape=[jax.ShapeDtypeStruct((nt, d), F32), jax.ShapeDtypeStruct((nt, d), F32),
                   jax.ShapeDtypeStruct((nt, LANES), F32),
                   jax.ShapeDtypeStruct((nt // TM, SUBLANES, TM), F32)],
        scratch_shapes=[pltpu.VMEM((GLA_HEADS, dk // GLA_HEADS, d // GLA_HEADS), F32),
                        pltpu.VMEM((TM, d), F32)],
        compiler_params=_params(("arbitrary", "arbitrary")),
        name="merge",
    )(x_all, mod3, q, k, v, la, ob, yag, gb, sr, glag, wb, wout, g2, wr, br)


def _plan_kernel(rt_ref, dest_ref, be_ref, cnt_ref, pst_ref, cnt_sc, run_sc, pst_sc):
    ph = pl.program_id(0)
    i = pl.program_id(1)
    rows = rt_ref.shape[2]
    sub = lax.broadcasted_iota(jnp.int32, (N_EXPERTS, rows), 0).astype(F32)
    oh0 = jnp.where(sub == rt_ref[0, 0:1, :], 1.0, 0.0)
    oh1 = jnp.where(sub == rt_ref[0, 1:2, :], 1.0, 0.0)
    tot0 = jnp.sum(oh0, axis=1, keepdims=True)
    tot1 = jnp.sum(oh1, axis=1, keepdims=True)
    wide = lambda col: jnp.broadcast_to(col, (N_EXPERTS, LANES))

    @pl.when(jnp.logical_and(ph == 0, i == 0))
    def _():
        cnt_sc[...] = jnp.zeros_like(cnt_sc)

    @pl.when(ph == 0)
    def _():
        cnt_sc[...] += wide(tot0 + tot1)

    @pl.when(jnp.logical_and(ph == 1, i == 0))
    def _():
        cnt = cnt_sc[...]
        padded = jnp.floor((cnt + (TB - 1)) * (1.0 / TB)) * TB
        ri = lax.broadcasted_iota(jnp.int32, (N_EXPERTS, N_EXPERTS), 0)
        ci = lax.broadcasted_iota(jnp.int32, (N_EXPERTS, N_EXPERTS), 1)
        p_end = jnp.dot(jnp.where(ci <= ri, 1.0, 0.0), padded, preferred_element_type=F32, precision=HIGHEST)
        pst_sc[...] = p_end - padded
        run_sc[...] = jnp.zeros_like(run_sc)
        cnt_ref[...] = cnt
        pst_ref[...] = p_end - padded
        starts = lax.broadcasted_iota(jnp.int32, (N_EXPERTS, be_ref.shape[1]), 1).astype(F32) * TB
        done = jnp.sum(jnp.where(p_end[:, 0:1] <= starts, 1.0, 0.0), axis=0, keepdims=True)
        be_ref[...] = jnp.minimum(done, N_EXPERTS - 1.0).astype(jnp.int32)

    @pl.when(ph == 1)
    def _():
        ri = lax.broadcasted_iota(jnp.int32, (rows, rows), 0)
        ci = lax.broadcasted_iota(jnp.int32, (rows, rows), 1)
        earlier = jnp.where(ri < ci, 1.0, 0.0).astype(BF16)
        c0 = _dot(oh0.astype(BF16), earlier)
        c1 = _dot(oh1.astype(BF16), earlier)
        base = pst_sc[:, 0:1] + run_sc[:, 0:1]
        d0 = jnp.sum(oh0 * (c0 + base), axis=0, keepdims=True)
        d1 = jnp.sum(oh1 * (c1 + base + tot0), axis=0, keepdims=True)
        run_sc[...] += wide(tot0 + tot1)
        dest_ref[0] = jnp.concatenate([d0, d1], axis=0).astype(jnp.int32)


def _plan(route_t, n_blocks):
    n_tiles = route_t.shape[0]
    nbp = -(-n_blocks // LANES) * LANES
    small = jax.ShapeDtypeStruct((N_EXPERTS, LANES), F32)
    return pl.pallas_call(
        _plan_kernel,
        grid=(2, n_tiles),
        in_specs=[pl.BlockSpec((1, SUBLANES, TM), lambda p, i: (i, 0, 0))],
        out_specs=[pl.BlockSpec((1, TOP_K, TM), lambda p, i: (p * i, 0, 0)),
                   _const_spec((1, nbp)), _const_spec(small.shape), _const_spec(small.shape)],
        out_shape=[jax.ShapeDtypeStruct((n_tiles, TOP_K, TM), jnp.int32),
                   jax.ShapeDtypeStruct((1, nbp), jnp.int32), small, small],
        scratch_shapes=[pltpu.VMEM((N_EXPERTS, LANES), F32)] * 3,
        compiler_params=_params(("arbitrary", "arbitrary")),
        name="plan",
    )(route_t)


def _sorted_rows(dest, block_e, cnt, pst, n_tok, n_blocks):
    cnt = cnt[:, 0].astype(jnp.int32)
    pst = pst[:, 0].astype(jnp.int32)
    pad = (cnt + TB - 1) // TB * TB - cnt
    pad_before = jnp.cumsum(pad) - pad
    be = block_e[0, :n_blocks]
    base = TOP_K * n_tok - pst[be] - cnt[be] + pad_before[be]
    init = base[:, None] + jnp.arange(n_blocks * TB, dtype=jnp.int32).reshape(n_blocks, TB)
    tok = jnp.arange(n_tok, dtype=jnp.int32).reshape(-1, 1, TM)
    slot_id = tok * TOP_K + jnp.arange(TOP_K, dtype=jnp.int32)[None, :, None]
    slot = init.reshape(-1).at[dest.reshape(-1)].set(slot_id.reshape(-1), unique_indices=True)
    src = jnp.where(slot < TOP_K * n_tok, slot // TOP_K, 0)
    return slot.reshape(n_blocks, 1, TB), src.reshape(n_blocks, 1, TB)


def _moe_kernel(be_ref, src_cur, src_nxt, slot_prv, slot_cur, h_hbm, wg_ref, wu_ref, wd_ref, y_hbm,
                x0, x1, y0, y1, gsem, ssem):
    del be_ref
    i = pl.program_id(0)
    last = pl.num_programs(0) - 1
    xs, ys = (x0, x1), (y0, y1)

    def gather_start(idx_ref, dst, sem):
        for r in range(TB):
            pltpu.make_async_copy(h_hbm.at[pl.ds(idx_ref[0, 0, r], 1)], dst.at[pl.ds(r, 1)], sem).start()

    def gather_wait(dst, sem):
        pltpu.make_async_copy(h_hbm.at[pl.ds(0, TB)], dst, sem).wait()

    def scatter_start(idx_ref, src, sem):
        for r in range(TB):
            pltpu.make_async_copy(src.at[pl.ds(r, 1)], y_hbm.at[pl.ds(idx_ref[0, 0, r], 1)], sem).start()

    def scatter_wait(src, sem):
        pltpu.make_async_copy(src, y_hbm.at[pl.ds(0, TB)], sem).wait()

    def experts(x_ref, y_ref):
        x = x_ref[...].astype(BF16)
        g = _dot(x, wg_ref[0])
        u = _dot(x, wu_ref[0])
        a = (g * jax.nn.sigmoid(g) * u).astype(BF16)
        y_ref[...] = _dot(a, wd_ref[0])

    @pl.when(i == 0)
    def _():
        gather_start(src_cur, x0, gsem.at[0])
        gather_wait(x0, gsem.at[0])
        gather_start(src_nxt, x1, gsem.at[1])
        experts(x0, y0)

    for p in range(2):
        @pl.when(jnp.logical_and(i > 0, lax.rem(i, 2) == p))
        def _(p=p):
            q = 1 - p
            gather_wait(xs[p], gsem.at[p])

            @pl.when(i >= 2)
            def _():
                scatter_wait(ys[p], ssem.at[p])

            gather_start(src_nxt, xs[q], gsem.at[q])
            scatter_start(slot_prv, ys[q], ssem.at[q])
            experts(xs[p], ys[p])

            @pl.when(i == last)
            def _():
                scatter_start(slot_cur, ys[p], ssem.at[p])
                scatter_wait(ys[q], ssem.at[q])
                scatter_wait(ys[p], ssem.at[p])
                gather_wait(xs[q], gsem.at[q])


def _moe(block_e, slot3, src3, h2, wg, wu, wd):
    nb = slot3.shape[0]
    d = h2.shape[1]
    de = wg.shape[2]
    idx_spec = lambda f: pl.BlockSpec((1, 1, TB), f, memory_space=pltpu.SMEM)
    cur = lambda i, be: (i, 0, 0)
    nxt = lambda i, be: (jnp.minimum(i + 1, nb - 1), 0, 0)
    prv = lambda i, be: (jnp.maximum(i - 1, 0), 0, 0)
    vm = pltpu.VMEM((TB, d), F32)
    return pl.pallas_call(
        _moe_kernel,
        grid_spec=pltpu.PrefetchScalarGridSpec(
            num_scalar_prefetch=1,
            grid=(nb,),
            in_specs=[idx_spec(cur), idx_spec(nxt), idx_spec(prv), idx_spec(cur),
                      pl.BlockSpec(memory_space=pl.ANY),
                      pl.BlockSpec((1, d, de), lambda i, be: (be[i], 0, 0)),
                      pl.BlockSpec((1, d, de), lambda i, be: (be[i], 0, 0)),
                      pl.BlockSpec((1, de, d), lambda i, be: (be[i], 0, 0))],
            out_specs=pl.BlockSpec(memory_space=pl.ANY),
            scratch_shapes=[vm, vm, vm, vm, pltpu.SemaphoreType.DMA((2,)), pltpu.SemaphoreType.DMA((2,))],
        ),
        out_shape=jax.ShapeDtypeStruct((nb * TB, d), F32),
        compiler_params=_params(("arbitrary",)),
        name="moe",
    )(block_e, src3, src3, slot3, slot3, h2, wg, wu, wd)


def _combine_kernel(y_ref, x1_ref, route_ref, mod_ref, gf_ref, o_ref, *, final_norm):
    d = x1_ref.shape[1]
    y = route_ref[:, 2:3] * y_ref[:, 0:d] + route_ref[:, 3:4] * y_ref[:, d:2 * d]
    x2 = x1_ref[...] + mod_ref[:, 5 * d:6 * d] * y
    if final_norm:
        x2 = x2 * lax.rsqrt(jnp.mean(x2 * x2, axis=-1, keepdims=True) + EPS) * gf_ref[...]
    o_ref[...] = x2


def _combine(y_slots, x1, route, mod3, gf, n_batch, final_norm):
    nt, d = x1.shape
    nj = nt // n_batch // TM
    ctx_row = n_batch

    def modrow(i):
        return (jnp.where(i % nj == 0, ctx_row, i // nj), 0, 0)

    if final_norm:
        out_rows = nt - n_batch * TM
        out_map = lambda i: ((i // nj) * (nj - 1) + jnp.maximum(i % nj - 1, 0), 0)
    else:
        out_rows = nt
        out_map = lambda i: (i, 0)
    return pl.pallas_call(
        functools.partial(_combine_kernel, final_norm=final_norm),
        grid=(nt // TM,),
        in_specs=[pl.BlockSpec((TM, TOP_K * d), lambda i: (i, 0)),
                  pl.BlockSpec((TM, d), lambda i: (i, 0)),
                  pl.BlockSpec((TM, LANES), lambda i: (i, 0)),
                  pl.BlockSpec((None, 1, mod3.shape[2]), modrow),
                  _const_spec(gf.shape)],
        out_specs=pl.BlockSpec((TM, d), out_map),
        out_shape=jax.ShapeDtypeStruct((out_rows, d), F32),
        compiler_params=_params(("arbitrary",)),
        name="combine",
    )(y_slots, x1, route, mod3, gf)


def kernel(x, c, ctx, c_ctx, w_mod, b_mod, g_norm1, g_norm2, w_in, w_gate2, b_gate2, gla_norm_g,
           sgu_ln_g, sgu_ln_b, sgu_w, sgu_b, w_branch_a, w_branch_b, b_branch, w_out,
           w_router, b_router, w_exp_gate, w_exp_up, w_exp_down, g_final):
    n_batch, seq, d = x.shape
    ctx_len = ctx.shape[1]
    depth = w_mod.shape[0]
    dk = w_gate2.shape[3]
    rank_lr = w_gate2.shape[2]
    assert ctx_len == TM and seq % TM == 0 and n_batch < MOD_ROWS

    x_all = jnp.concatenate([ctx, x], axis=1).reshape(-1, d)
    nt = x_all.shape[0]
    n_blocks = TOP_K * nt // TB + N_EXPERTS
    cc = jnp.zeros((MOD_ROWS, d), F32).at[:n_batch].set(c).at[n_batch].set(c_ctx)
    mod = _modulation(cc, w_mod, b_mod)

    wr = jnp.zeros((d, LANES), F32).at[:, :N_EXPERTS].set(w_router)
    br = jnp.zeros((1, LANES), F32).at[0, :N_EXPERTS].set(b_router)
    row = lambda a: a.reshape(1, -1)
    n_main = 2 * d + 2 * dk + 2 * d

    out = None
    for l in range(depth):
        last = l == depth - 1
        mod3 = mod[l].reshape(MOD_ROWS, 1, 6 * d)
        wmain = w_in[l][:, :n_main].astype(BF16)
        wlr = w_in[l][:, n_main:n_main + 2 * rank_lr].astype(BF16)
        wgates = w_in[l][:, n_main + 2 * rank_lr:].astype(BF16)
        wg2 = jnp.zeros((2 * rank_lr, 2 * dk), F32)
        wg2 = wg2.at[:rank_lr, :dk].set(w_gate2[l, 0]).at[rank_lr:, dk:].set(w_gate2[l, 1]).astype(BF16)
        bg2 = b_gate2[l].reshape(1, 2 * dk)
        sgub = jnp.repeat(sgu_b[l].T, d // A_GROUPS, axis=1)

        yag, gb, q, k, v, sr, la = _inproj(
            x_all, mod3, row(g_norm1[l]), wmain, wlr, wgates, wg2, bg2, row(sgu_ln_g[l]), row(sgu_ln_b[l]),
            sgu_w[l].astype(BF16), sgub, w_branch_a[l].astype(BF16), row(b_branch[l]), n_batch)
        ob = _gla_bwd(q, k, v, la, n_batch)
        x1, h2, route, route_t = _merge(
            x_all, mod3, q, k, v, la, ob, yag, gb, sr, row(gla_norm_g[l]), w_branch_b[l].astype(BF16),
            w_out[l].astype(BF16), row(g_norm2[l]), wr, br, n_batch)
        dest, block_e, cnt, pst = _plan(route_t, n_blocks)
        slot3, src3 = _sorted_rows(dest, block_e, cnt, pst, nt, n_blocks)
        y_rows = _moe(block_e.reshape(-1), slot3, src3, h2, w_exp_gate[l].astype(BF16),
                      w_exp_up[l].astype(BF16), w_exp_down[l].astype(BF16))
        y_slots = y_rows.reshape(-1, TOP_K * d)
        res = _combine(y_slots, x1, route, mod3, row(g_final), n_batch, final_norm=last)
        if last:
            out = res.reshape(n_batch, seq, d)
        else:
            x_all = res
    return out
```

```python
import functools

import jax
import jax.numpy as jnp
from jax import lax
from jax.experimental import pallas as pl
from jax.experimental.pallas import tpu as pltpu

F32 = jnp.float32
BF16 = jnp.bfloat16
HIGHEST = lax.Precision.HIGHEST

A_CHUNK = 128
A_GROUPS = 8
GLA_HEADS = 4
GLA_TAU = 16.0
GLA_CHUNK = 64
N_EXPERTS = 16
N_EXPERT_GROUPS = 4
EXPERTS_PER_GROUP = N_EXPERTS // N_EXPERT_GROUPS
TOP_K = 2
EPS = 1e-6

LANES = 128
SUBLANES = 8
TM = 256
TB = 256
MOD_ROWS = 16
VMEM_LIMIT = 56 * 1024 * 1024


def _dot(a, b):
    return jnp.dot(a, b, preferred_element_type=F32)


def _const_spec(shape):
    nd = len(shape)
    return pl.BlockSpec(shape, lambda *_: (0,) * nd)


def _params(sem):
    return pltpu.CompilerParams(dimension_semantics=sem, vmem_limit_bytes=VMEM_LIMIT)


def _stream_specs(joined, nj, d):
    if joined:
        ctx_map = lambda b, j: (b * nj, 0)
        lat_map = lambda b, j: (b * nj + jnp.maximum(j, 1), 0)
    else:
        ctx_map = lambda b, j: (b, 0)
        lat_map = lambda b, j: (b * (nj - 1) + jnp.maximum(j - 1, 0), 0)
    return pl.BlockSpec((TM, d), ctx_map), pl.BlockSpec((TM, d), lat_map)


def _mod_kernel(cc_ref, w_ref, b_ref, o_ref):
    cc = cc_ref[...]
    s = cc * jax.nn.sigmoid(cc)
    o_ref[0] = jnp.dot(s, w_ref[0], preferred_element_type=F32, precision=HIGHEST) + b_ref[0]


def _modulation(cc, w_mod, b_mod):
    n_layer, d, six_d = w_mod.shape
    return pl.pallas_call(
        _mod_kernel,
        grid=(n_layer, six_d // d),
        in_specs=[
            pl.BlockSpec((MOD_ROWS, d), lambda l, j: (0, 0)),
            pl.BlockSpec((1, d, d), lambda l, j: (l, 0, j)),
            pl.BlockSpec((1, 1, d), lambda l, j: (l, 0, j)),
        ],
        out_specs=pl.BlockSpec((1, MOD_ROWS, d), lambda l, j: (l, 0, j)),
        out_shape=jax.ShapeDtypeStruct((n_layer, MOD_ROWS, six_d), F32),
        compiler_params=_params(("arbitrary", "arbitrary")),
        name="modulation",
    )(cc, w_mod, b_mod.reshape(n_layer, 1, six_d))


def _gelu_tanh(x):
    c = 0.7978845608028654
    return 0.5 * x * (1.0 + jnp.tanh(c * (x + 0.044715 * (x * x * x))))


def _log_sigmoid(z):
    return jnp.minimum(z, 0.0) - jnp.log1p(jnp.exp(-jnp.abs(z)))


def _inproj_kernel(xc_ref, xl_ref, mod_ref, g1_ref, wmain_ref, wlr_ref, wgates_ref, wg2_ref, bg2_ref,
                   lng_ref, lnb_ref, sguw_ref, sgub_ref, wa_ref, bbr_ref,
                   yag_ref, gb_ref, q_ref, k_ref, v_ref, sr_ref, la_ref, sa_ref):
    d = xc_ref.shape[1]
    dk = q_ref.shape[1]
    x = jnp.where(pl.program_id(1) == 0, xc_ref[...], xl_ref[...])
    sh1 = mod_ref[:, 0:d]
    sc1 = mod_ref[:, d:2 * d]
    h = x * lax.rsqrt(jnp.mean(x * x, axis=-1, keepdims=True) + EPS) * g1_ref[...]
    hb = (h * (1.0 + sc1) + sh1).astype(BF16)

    u = _gelu_tanh(_dot(hb, wmain_ref[:, 0:d]))
    vv = _gelu_tanh(_dot(hb, wmain_ref[:, d:2 * d]))
    mu = jnp.mean(vv, axis=-1, keepdims=True)
    vc = vv - mu
    vn = vc * lax.rsqrt(jnp.mean(vc * vc, axis=-1, keepdims=True) + EPS) * lng_ref[...] + lnb_ref[...]
    vnb = vn.astype(BF16)
    gdim = d // A_GROUPS
    for n in range(x.shape[0] // A_CHUNK):
        rs = slice(n * A_CHUNK, (n + 1) * A_CHUNK)
        for g in range(A_GROUPS):
            cs = slice(g * gdim, (g + 1) * gdim)
            mixed = _dot(sguw_ref[g], vnb[rs, cs]) + sgub_ref[:, cs]
            sa_ref[rs, cs] = (u[rs, cs] * mixed).astype(BF16)
    ya = _dot(sa_ref[...], wa_ref[...])
    gates = _dot(hb, wgates_ref[...]) + bbr_ref[...]
    yag_ref[...] = (jax.nn.sigmoid(gates[:, 0:d]) * ya).astype(BF16)
    gb_ref[...] = jax.nn.sigmoid(gates[:, d:2 * d]).astype(BF16)

    o = 2 * d
    head_k = dk // GLA_HEADS
    q_ref[...] = (_dot(hb, wmain_ref[:, o:o + dk]) * (head_k ** -0.5)).astype(BF16)
    k_ref[...] = _dot(hb, wmain_ref[:, o + dk:o + 2 * dk]).astype(BF16)
    o = o + 2 * dk
    v_ref[...] = _dot(hb, wmain_ref[:, o:o + d]).astype(BF16)
    r = _dot(hb, wmain_ref[:, o + d:o + 2 * d])
    sr_ref[...] = (r * jax.nn.sigmoid(r)).astype(BF16)
    lr = _dot(hb, wlr_ref[...]).astype(BF16)
    z = _dot(lr, wg2_ref[...]) + bg2_ref[...]
    la_ref[...] = _log_sigmoid(z) * (1.0 / GLA_TAU)


def _inproj(xc, xl, joined, nt, mod3, g1, wmain, wlr, wgates, wg2, bg2, lng, lnb, sguw, sgub, wa, bbr, n_batch):
    d = xc.shape[1]
    dk = wg2.shape[1] // 2
    nj = nt // n_batch // TM
    ctx_row = n_batch

    def row(b, j):
        return (b * nj + j, 0)

    def modrow(b, j):
        return (jnp.where(j == 0, ctx_row, b), 0, 0)

    tile = lambda w: pl.BlockSpec((TM, w), row)
    outs = [(d, BF16), (d, BF16), (dk, BF16), (dk, BF16), (d, BF16), (d, BF16), (2 * dk, F32)]
    return pl.pallas_call(
        _inproj_kernel,
        grid=(n_batch, nj),
        in_specs=[*_stream_specs(joined, nj, d), pl.BlockSpec((None, 1, mod3.shape[2]), modrow)]
        + [_const_spec(a.shape) for a in (g1, wmain, wlr, wgates, wg2, bg2, lng, lnb, sguw, sgub, wa, bbr)],
        out_specs=[tile(w) for w, _ in outs],
        out_shape=[jax.ShapeDtypeStruct((nt, w), t) for w, t in outs],
        scratch_shapes=[pltpu.VMEM((TM, d), BF16)],
        compiler_params=_params(("arbitrary", "arbitrary")),
        name="inproj",
    )(xc, xl, mod3, g1, wmain, wlr, wgates, wg2, bg2, lng, lnb, sguw, sgub, wa, bbr)


def _gla_tile(q_ref, k_ref, v_ref, la_ref, s_ref, o_ref, reverse):
    rows, dk = q_ref.shape
    dv = v_ref.shape[1]
    hk = dk // GLA_HEADS
    hv = dv // GLA_HEADS
    c = GLA_CHUNK
    ri = lax.broadcasted_iota(jnp.int32, (c, c), 0)
    ci = lax.broadcasted_iota(jnp.int32, (c, c), 1)
    tri = (ci >= ri) if reverse else (ci <= ri)
    tri_f = tri.astype(F32)
    chunks = range(rows // c)
    for n in (reversed(chunks) if reverse else chunks):
        rs = slice(n * c, (n + 1) * c)
        la = la_ref[rs, :]
        b = jnp.dot(tri_f, la, preferred_element_type=F32, precision=HIGHEST)
        b_end = b[0:1, :] if reverse else b[c - 1:c, :]
        q = q_ref[rs, :].astype(F32)
        k = k_ref[rs, :].astype(F32)
        q_in = (q * jnp.exp(b)).astype(BF16)
        k_in = (k * jnp.exp(-b)).astype(BF16)
        k_st = (k * jnp.exp(b_end - b)).astype(BF16)
        decay = jnp.exp(b_end)
        for h in range(GLA_HEADS):
            ks = slice(h * hk, (h + 1) * hk)
            vs = slice(h * hv, (h + 1) * hv)
            vh = v_ref[rs, vs]
            att = lax.dot_general(q_in[:, ks], k_in[:, ks], (((1,), (1,)), ((), ())),
                                  preferred_element_type=F32)
            att = jnp.where(tri, att, 0.0).astype(BF16)
            state = s_ref[h]
            o_ref[rs, vs] = _dot(att, vh) + _dot(q_in[:, ks], state.astype(BF16))
            dcol = jnp.transpose(jnp.broadcast_to(decay[:, ks], (hk, hk)))
            dmat = jnp.concatenate([dcol] * (hv // hk), axis=1)
            kv = lax.dot_general(k_st[:, ks], vh, (((0,), (0,)), ((), ())), preferred_element_type=F32)
            s_ref[h] = dmat * state + kv


def _gla_bwd_kernel(q_ref, k_ref, v_ref, la_ref, ob_ref, s_ref, o_sc):
    @pl.when(pl.program_id(1) == 0)
    def _():
        s_ref[...] = jnp.zeros_like(s_ref)

    _gla_tile(q_ref, k_ref, v_ref, la_ref, s_ref, o_sc, reverse=True)
    ob_ref[...] = o_sc[...].astype(BF16)


def _gla_bwd(q, k, v, la, n_batch):
    nt, dk = q.shape
    dv = v.shape[1]
    nj = nt // n_batch // TM

    def row(b, jj):
        return (b * nj + jnp.where(jj == 0, 0, nj - jj), 0)

    def row_la(b, jj):
        return (b * nj + jnp.where(jj == 0, 0, nj - jj), 1)

    return pl.pallas_call(
        _gla_bwd_kernel,
        grid=(n_batch, nj),
        in_specs=[pl.BlockSpec((TM, dk), row), pl.BlockSpec((TM, dk), row), pl.BlockSpec((TM, dv), row),
                  pl.BlockSpec((TM, dk), row_la)],
        out_specs=pl.BlockSpec((TM, dv), row),
        out_shape=jax.ShapeDtypeStruct((nt, dv), BF16),
        scratch_shapes=[pltpu.VMEM((GLA_HEADS, dk // GLA_HEADS, dv // GLA_HEADS), F32),
                        pltpu.VMEM((TM, dv), F32)],
        compiler_params=_params(("arbitrary", "arbitrary")),
        name="gla_bwd",
    )(q, k, v, la)


def _route(logits):
    lane = lax.broadcasted_iota(jnp.int32, logits.shape, 1)
    valid = lane < N_EXPERTS
    lg = jnp.where(valid, logits, -1e30)
    ex = jnp.where(valid, jnp.exp(lg - jnp.max(lg, axis=-1, keepdims=True)), 0.0)
    p = ex / jnp.sum(ex, axis=-1, keepdims=True)
    grp = lane // EXPERTS_PER_GROUP
    none = -1.0
    far = 2 * LANES
    best = None
    for g in range(N_EXPERT_GROUPS):
        pg = jnp.where(grp == g, p, none)
        m1 = jnp.max(pg, axis=-1, keepdims=True)
        i1 = jnp.min(jnp.where(pg == m1, lane, far), axis=-1, keepdims=True)
        pg2 = jnp.where(lane == i1, none, pg)
        m2 = jnp.max(pg2, axis=-1, keepdims=True)
        i2 = jnp.min(jnp.where(pg2 == m2, lane, far), axis=-1, keepdims=True)
        cand = (m1 + m2, m1, i1, m2, i2)
        if best is None:
            best = cand
        else:
            better = cand[0] > best[0]
            best = tuple(jnp.where(better, c, o) for c, o in zip(cand, best))
    _, m1, i1, m2, i2 = best
    tot = m1 + m2
    slab = jnp.where(lane == 0, i1.astype(F32), 0.0)
    slab = jnp.where(lane == 1, i2.astype(F32), slab)
    slab = jnp.where(lane == 2, m1 / tot, slab)
    slab = jnp.where(lane == 3, m2 / tot, slab)
    return slab


def _merge_kernel(xc_ref, xl_ref, mod_ref, q_ref, k_ref, v_ref, la_ref, ob_ref, yag_ref, gb_ref, sr_ref,
                  glag_ref, wb_ref, wout_ref, g2_ref, wr_ref, br_ref,
                  x1_ref, h2_ref, route_ref, routet_ref, s_ref, o_sc):
    @pl.when(pl.program_id(1) == 0)
    def _():
        s_ref[...] = jnp.zeros_like(s_ref)

    _gla_tile(q_ref, k_ref, v_ref, la_ref, s_ref, o_sc, reverse=False)

    d = xc_ref.shape[1]
    hv = d // GLA_HEADS
    o = o_sc[...] + ob_ref[...].astype(F32)
    parts = []
    for h in range(GLA_HEADS):
        oh = o[:, h * hv:(h + 1) * hv]
        parts.append(oh * lax.rsqrt(jnp.mean(oh * oh, axis=-1, keepdims=True) + EPS))
    on = jnp.concatenate(parts, axis=1) * glag_ref[...]
    yb = _dot((on * sr_ref[...].astype(F32)).astype(BF16), wb_ref[...])
    m = yag_ref[...].astype(F32) + gb_ref[...].astype(F32) * yb
    y = _dot(m.astype(BF16), wout_ref[...])
    gt1 = mod_ref[:, 2 * d:3 * d]
    sh2 = mod_ref[:, 3 * d:4 * d]
    sc2 = mod_ref[:, 4 * d:5 * d]
    x1 = jnp.where(pl.program_id(1) == 0, xc_ref[...], xl_ref[...]) + gt1 * y
    x1_ref[...] = x1
    h2 = x1 * lax.rsqrt(jnp.mean(x1 * x1, axis=-1, keepdims=True) + EPS) * g2_ref[...]
    h2 = h2 * (1.0 + sc2) + sh2
    h2_ref[...] = h2
    logits = jnp.dot(h2, wr_ref[...], preferred_element_type=F32, precision=HIGHEST) + br_ref[...]
    slab = _route(logits)
    route_ref[...] = slab
    routet_ref[0] = jnp.transpose(slab)[0:SUBLANES, :]


def _merge(xc, xl, joined, nt, mod3, q, k, v, la, ob, yag, gb, sr, glag, wb, wout, g2, wr, br, n_batch):
    d = xc.shape[1]
    dk = q.shape[1]
    nj = nt // n_batch // TM
    ctx_row = n_batch

    def row(b, j):
        return (b * nj + j, 0)

    def modrow(b, j):
        return (jnp.where(j == 0, ctx_row, b), 0, 0)

    tile = lambda w: pl.BlockSpec((TM, w), row)
    return pl.pallas_call(
        _merge_kernel,
        grid=(n_batch, nj),
        in_specs=[*_stream_specs(joined, nj, d), pl.BlockSpec((None, 1, mod3.shape[2]), modrow),
                  tile(dk), tile(dk), tile(d), tile(dk), tile(d), tile(d), tile(d), tile(d)]
        + [_const_spec(a.shape) for a in (glag, wb, wout, g2, wr, br)],
        out_specs=[tile(d), tile(d), tile(LANES),
                   pl.BlockSpec((1, SUBLANES, TM), lambda b, j: (b * nj + j, 0, 0))],
        out_shape=[jax.ShapeDtypeStruct((nt, d), F32), jax.ShapeDtypeStruct((nt, d), F32),
                   jax.ShapeDtypeStruct((nt, LANES), F32),
                   jax.ShapeDtypeStruct((nt // TM, SUBLANES, TM), F32)],
        scratch_shapes=[pltpu.VMEM((GLA_HEADS, dk // GLA_HEADS, d // GLA_HEADS), F32),
                        pltpu.VMEM((TM, d), F32)],
        compiler_params=_params(("arbitrary", "arbitrary")),
        name="merge",
    )(xc, xl, mod3, q, k, v, la, ob, yag, gb, sr, glag, wb, wout, g2, wr, br)


def _plan_kernel(rt_ref, dest_ref, be_ref, cnt_ref, pst_ref, cnt_sc, run_sc, pst_sc):
    ph = pl.program_id(0)
    i = pl.program_id(1)
    rows = rt_ref.shape[2]
    sub = lax.broadcasted_iota(jnp.int32, (N_EXPERTS, rows), 0).astype(F32)
    oh0 = jnp.where(sub == rt_ref[0, 0:1, :], 1.0, 0.0)
    oh1 = jnp.where(sub == rt_ref[0, 1:2, :], 1.0, 0.0)
    tot0 = jnp.sum(oh0, axis=1, keepdims=True)
    tot1 = jnp.sum(oh1, axis=1, keepdims=True)
    wide = lambda col: jnp.broadcast_to(col, (N_EXPERTS, LANES))

    @pl.when(jnp.logical_and(ph == 0, i == 0))
    def _():
        cnt_sc[...] = jnp.zeros_like(cnt_sc)

    @pl.when(ph == 0)
    def _():
        cnt_sc[...] += wide(tot0 + tot1)

    @pl.when(jnp.logical_and(ph == 1, i == 0))
    def _():
        cnt = cnt_sc[...]
        padded = jnp.floor((cnt + (TB - 1)) * (1.0 / TB)) * TB
        ri = lax.broadcasted_iota(jnp.int32, (N_EXPERTS, N_EXPERTS), 0)
        ci = lax.broadcasted_iota(jnp.int32, (N_EXPERTS, N_EXPERTS), 1)
        p_end = jnp.dot(jnp.where(ci <= ri, 1.0, 0.0), padded, preferred_element_type=F32, precision=HIGHEST)
        pst_sc[...] = p_end - padded
        run_sc[...] = jnp.zeros_like(run_sc)
        cnt_ref[...] = cnt
        pst_ref[...] = p_end - padded
        starts = lax.broadcasted_iota(jnp.int32, (N_EXPERTS, be_ref.shape[1]), 1).astype(F32) * TB
        done = jnp.sum(jnp.where(p_end[:, 0:1] <= starts, 1.0, 0.0), axis=0, keepdims=True)
        be_ref[...] = jnp.minimum(done, N_EXPERTS - 1.0).astype(jnp.int32)

    @pl.when(ph == 1)
    def _():
        ri = lax.broadcasted_iota(jnp.int32, (rows, rows), 0)
        ci = lax.broadcasted_iota(jnp.int32, (rows, rows), 1)
        earlier = jnp.where(ri < ci, 1.0, 0.0).astype(BF16)
        c0 = _dot(oh0.astype(BF16), earlier)
        c1 = _dot(oh1.astype(BF16), earlier)
        base = pst_sc[:, 0:1] + run_sc[:, 0:1]
        d0 = jnp.sum(oh0 * (c0 + base), axis=0, keepdims=True)
        d1 = jnp.sum(oh1 * (c1 + base + tot0), axis=0, keepdims=True)
        run_sc[...] += wide(tot0 + tot1)
        dest_ref[0] = jnp.concatenate([d0, d1], axis=0).astype(jnp.int32)


def _plan(route_t, n_blocks):
    n_tiles = route_t.shape[0]
    nbp = -(-n_blocks // LANES) * LANES
    small = jax.ShapeDtypeStruct((N_EXPERTS, LANES), F32)
    return pl.pallas_call(
        _plan_kernel,
        grid=(2, n_tiles),
        in_specs=[pl.BlockSpec((1, SUBLANES, TM), lambda p, i: (i, 0, 0))],
        out_specs=[pl.BlockSpec((1, TOP_K, TM), lambda p, i: (p * i, 0, 0)),
                   _const_spec((1, nbp)), _const_spec(small.shape), _const_spec(small.shape)],
        out_shape=[jax.ShapeDtypeStruct((n_tiles, TOP_K, TM), jnp.int32),
                   jax.ShapeDtypeStruct((1, nbp), jnp.int32), small, small],
        scratch_shapes=[pltpu.VMEM((N_EXPERTS, LANES), F32)] * 3,
        compiler_params=_params(("arbitrary", "arbitrary")),
        name="plan",
    )(route_t)


def _sorted_rows(dest, block_e, cnt, pst, n_tok, n_blocks):
    cnt = cnt[:, 0].astype(jnp.int32)
    pst = pst[:, 0].astype(jnp.int32)
    pad = (cnt + TB - 1) // TB * TB - cnt
    pad_before = jnp.cumsum(pad) - pad
    be = block_e[0, :n_blocks]
    base = TOP_K * n_tok - pst[be] - cnt[be] + pad_before[be]
    init = base[:, None] + jnp.arange(n_blocks * TB, dtype=jnp.int32).reshape(n_blocks, TB)
    tok = jnp.arange(n_tok, dtype=jnp.int32).reshape(-1, 1, TM)
    slot_id = tok + n_tok * jnp.arange(TOP_K, dtype=jnp.int32)[None, :, None]
    slot = init.reshape(-1).at[dest.reshape(-1)].set(slot_id.reshape(-1), unique_indices=True)
    src = jnp.where(slot < TOP_K * n_tok, slot % n_tok, 0)
    return slot.reshape(n_blocks, 1, TB), src.reshape(n_blocks, 1, TB)


def _moe_kernel(be_ref, src_cur, src_nxt, slot_prv, slot_cur, h_hbm, wg_ref, wu_ref, wd_ref, y_hbm,
                x0, x1, y0, y1, gsem, ssem):
    del be_ref
    i = pl.program_id(0)
    last = pl.num_programs(0) - 1
    xs, ys = (x0, x1), (y0, y1)

    def gather_start(idx_ref, dst, sem):
        for r in range(TB):
            pltpu.make_async_copy(h_hbm.at[pl.ds(idx_ref[0, 0, r], 1)], dst.at[pl.ds(r, 1)], sem).start()

    def gather_wait(dst, sem):
        pltpu.make_async_copy(h_hbm.at[pl.ds(0, TB)], dst, sem).wait()

    def scatter_start(idx_ref, src, sem):
        for r in range(TB):
            pltpu.make_async_copy(src.at[pl.ds(r, 1)], y_hbm.at[pl.ds(idx_ref[0, 0, r], 1)], sem).start()

    def scatter_wait(src, sem):
        pltpu.make_async_copy(src, y_hbm.at[pl.ds(0, TB)], sem).wait()

    def experts(x_ref, y_ref):
        x = x_ref[...].astype(BF16)
        g = _dot(x, wg_ref[0])
        u = _dot(x, wu_ref[0])
        a = (g * jax.nn.sigmoid(g) * u).astype(BF16)
        y_ref[...] = _dot(a, wd_ref[0])

    @pl.when(i == 0)
    def _():
        gather_start(src_cur, x0, gsem.at[0])
        gather_wait(x0, gsem.at[0])
        gather_start(src_nxt, x1, gsem.at[1])
        experts(x0, y0)

    for p in range(2):
        @pl.when(jnp.logical_and(i > 0, lax.rem(i, 2) == p))
        def _(p=p):
            q = 1 - p
            gather_wait(xs[p], gsem.at[p])

            @pl.when(i >= 2)
            def _():
                scatter_wait(ys[p], ssem.at[p])

            gather_start(src_nxt, xs[q], gsem.at[q])
            scatter_start(slot_prv, ys[q], ssem.at[q])
            experts(xs[p], ys[p])

            @pl.when(i == last)
            def _():
                scatter_start(slot_cur, ys[p], ssem.at[p])
                scatter_wait(ys[q], ssem.at[q])
                scatter_wait(ys[p], ssem.at[p])
                gather_wait(xs[q], gsem.at[q])


def _moe(block_e, slot3, src3, h2, wg, wu, wd):
    nb = slot3.shape[0]
    d = h2.shape[1]
    de = wg.shape[2]
    idx_spec = lambda f: pl.BlockSpec((1, 1, TB), f, memory_space=pltpu.SMEM)
    cur = lambda i, be: (i, 0, 0)
    nxt = lambda i, be: (jnp.minimum(i + 1, nb - 1), 0, 0)
    prv = lambda i, be: (jnp.maximum(i - 1, 0), 0, 0)
    vm = pltpu.VMEM((TB, d), F32)
    return pl.pallas_call(
        _moe_kernel,
        grid_spec=pltpu.PrefetchScalarGridSpec(
            num_scalar_prefetch=1,
            grid=(nb,),
            in_specs=[idx_spec(cur), idx_spec(nxt), idx_spec(prv), idx_spec(cur),
                      pl.BlockSpec(memory_space=pl.ANY),
                      pl.BlockSpec((1, d, de), lambda i, be: (be[i], 0, 0)),
                      pl.BlockSpec((1, d, de), lambda i, be: (be[i], 0, 0)),
                      pl.BlockSpec((1, de, d), lambda i, be: (be[i], 0, 0))],
            out_specs=pl.BlockSpec(memory_space=pl.ANY),
            scratch_shapes=[vm, vm, vm, vm, pltpu.SemaphoreType.DMA((2,)), pltpu.SemaphoreType.DMA((2,))],
        ),
        out_shape=jax.ShapeDtypeStruct((nb * TB, d), F32),
        compiler_params=_params(("arbitrary",)),
        name="moe",
    )(block_e, src3, src3, slot3, slot3, h2, wg, wu, wd)


def _combine_kernel(y0_ref, y1_ref, x1_ref, route_ref, mod_ref, gf_ref, o_ref, *, final_norm):
    d = x1_ref.shape[1]
    y = route_ref[:, 2:3] * y0_ref[...] + route_ref[:, 3:4] * y1_ref[...]
    x2 = x1_ref[...] + mod_ref[:, 5 * d:6 * d] * y
    if final_norm:
        x2 = x2 * lax.rsqrt(jnp.mean(x2 * x2, axis=-1, keepdims=True) + EPS) * gf_ref[...]
    o_ref[...] = x2


def _combine(y_rows, x1, route, mod3, gf, n_batch, final_norm):
    nt, d = x1.shape
    nj = nt // n_batch // TM
    ctx_row = n_batch

    def modrow(i):
        return (jnp.where(i % nj == 0, ctx_row, i // nj), 0, 0)

    if final_norm:
        out_rows = nt - n_batch * TM
        out_map = lambda i: ((i // nj) * (nj - 1) + jnp.maximum(i % nj - 1, 0), 0)
    else:
        out_rows = nt
        out_map = lambda i: (i, 0)
    return pl.pallas_call(
        functools.partial(_combine_kernel, final_norm=final_norm),
        grid=(nt // TM,),
        in_specs=[pl.BlockSpec((TM, d), lambda i: (i, 0)),
                  pl.BlockSpec((TM, d), lambda i: (nt // TM + i, 0)),
                  pl.BlockSpec((TM, d), lambda i: (i, 0)),
                  pl.BlockSpec((TM, LANES), lambda i: (i, 0)),
                  pl.BlockSpec((None, 1, mod3.shape[2]), modrow),
                  _const_spec(gf.shape)],
        out_specs=pl.BlockSpec((TM, d), out_map),
        out_shape=jax.ShapeDtypeStruct((out_rows, d), F32),
        compiler_params=_params(("arbitrary",)),
        name="combine",
    )(y_rows, y_rows, x1, route, mod3, gf)


def kernel(x, c, ctx, c_ctx, w_mod, b_mod, g_norm1, g_norm2, w_in, w_gate2, b_gate2, gla_norm_g,
           sgu_ln_g, sgu_ln_b, sgu_w, sgu_b, w_branch_a, w_branch_b, b_branch, w_out,
           w_router, b_router, w_exp_gate, w_exp_up, w_exp_down, g_final):
    n_batch, seq, d = x.shape
    ctx_len = ctx.shape[1]
    depth = w_mod.shape[0]
    dk = w_gate2.shape[3]
    rank_lr = w_gate2.shape[2]
    assert ctx_len == TM and seq % TM == 0 and n_batch < MOD_ROWS

    nt = n_batch * (ctx_len + seq)
    streams = (ctx.reshape(-1, d), x.reshape(-1, d), False)
    n_blocks = TOP_K * nt // TB + N_EXPERTS
    cc = jnp.zeros((MOD_ROWS, d), F32).at[:n_batch].set(c).at[n_batch].set(c_ctx)
    mod = _modulation(cc, w_mod, b_mod)

    wr = jnp.zeros((d, LANES), F32).at[:, :N_EXPERTS].set(w_router)
    br = jnp.zeros((1, LANES), F32).at[0, :N_EXPERTS].set(b_router)
    row = lambda a: a.reshape(1, -1)
    n_main = 2 * d + 2 * dk + 2 * d

    out = None
    for l in range(depth):
        last = l == depth - 1
        mod3 = mod[l].reshape(MOD_ROWS, 1, 6 * d)
        wmain = w_in[l][:, :n_main].astype(BF16)
        wlr = w_in[l][:, n_main:n_main + 2 * rank_lr].astype(BF16)
        wgates = w_in[l][:, n_main + 2 * rank_lr:].astype(BF16)
        wg2 = jnp.zeros((2 * rank_lr, 2 * dk), F32)
        wg2 = wg2.at[:rank_lr, :dk].set(w_gate2[l, 0]).at[rank_lr:, dk:].set(w_gate2[l, 1]).astype(BF16)
        bg2 = b_gate2[l].reshape(1, 2 * dk)
        sgub = jnp.repeat(sgu_b[l].T, d // A_GROUPS, axis=1)

        yag, gb, q, k, v, sr, la = _inproj(
            *streams, nt, mod3, row(g_norm1[l]), wmain, wlr, wgates, wg2, bg2, row(sgu_ln_g[l]), row(sgu_ln_b[l]),
            sgu_w[l].astype(BF16), sgub, w_branch_a[l].astype(BF16), row(b_branch[l]), n_batch)
        ob = _gla_bwd(q, k, v, la, n_batch)
        x1, h2, route, route_t = _merge(
            *streams, nt, mod3, q, k, v, la, ob, yag, gb, sr, row(gla_norm_g[l]), w_branch_b[l].astype(BF16),
            w_out[l].astype(BF16), row(g_norm2[l]), wr, br, n_batch)
        dest, block_e, cnt, pst = _plan(route_t, n_blocks)
        slot3, src3 = _sorted_rows(dest, block_e, cnt, pst, nt, n_blocks)
        y_rows = _moe(block_e.reshape(-1), slot3, src3, h2, w_exp_gate[l].astype(BF16),
                      w_exp_up[l].astype(BF16), w_exp_down[l].astype(BF16))
        res = _combine(y_rows, x1, route, mod3, row(g_final), n_batch, final_norm=last)
        if last:
            out = res.reshape(n_batch, seq, d)
        else:
            streams = (res, res, True)
    return out
```

```python
import functools

import jax
import jax.numpy as jnp
from jax import lax
from jax.experimental import pallas as pl
from jax.experimental.pallas import tpu as pltpu

F32 = jnp.float32
BF16 = jnp.bfloat16
HIGHEST = lax.Precision.HIGHEST

A_CHUNK = 128
A_GROUPS = 8
GLA_HEADS = 4
GLA_TAU = 16.0
GLA_CHUNK = 64
N_EXPERTS = 16
N_EXPERT_GROUPS = 4
EXPERTS_PER_GROUP = N_EXPERTS // N_EXPERT_GROUPS
TOP_K = 2
EPS = 1e-6

LANES = 128
SUBLANES = 8
TM = 256
TB = 256
MOD_ROWS = 16
VMEM_LIMIT = 56 * 1024 * 1024


def _dot(a, b):
    return jnp.dot(a, b, preferred_element_type=F32)


def _const_spec(shape):
    nd = len(shape)
    return pl.BlockSpec(shape, lambda *_: (0,) * nd)


def _params(sem):
    return pltpu.CompilerParams(dimension_semantics=sem, vmem_limit_bytes=VMEM_LIMIT)


def _stream_specs(joined, nj, d):
    if joined:
        ctx_map = lambda b, j: (b * nj, 0)
        lat_map = lambda b, j: (b * nj + jnp.maximum(j, 1), 0)
    else:
        ctx_map = lambda b, j: (b, 0)
        lat_map = lambda b, j: (b * (nj - 1) + jnp.maximum(j - 1, 0), 0)
    return pl.BlockSpec((TM, d), ctx_map), pl.BlockSpec((TM, d), lat_map)


def _mod_kernel(cc_ref, w_ref, b_ref, o_ref):
    cc = cc_ref[...]
    s = cc * jax.nn.sigmoid(cc)
    o_ref[0] = jnp.dot(s, w_ref[0], preferred_element_type=F32, precision=HIGHEST) + b_ref[0]


def _modulation(cc, w_mod, b_mod):
    n_layer, d, six_d = w_mod.shape
    return pl.pallas_call(
        _mod_kernel,
        grid=(n_layer, six_d // d),
        in_specs=[
            pl.BlockSpec((MOD_ROWS, d), lambda l, j: (0, 0)),
            pl.BlockSpec((1, d, d), lambda l, j: (l, 0, j)),
            pl.BlockSpec((1, 1, d), lambda l, j: (l, 0, j)),
        ],
        out_specs=pl.BlockSpec((1, MOD_ROWS, d), lambda l, j: (l, 0, j)),
        out_shape=jax.ShapeDtypeStruct((n_layer, MOD_ROWS, six_d), F32),
        compiler_params=_params(("arbitrary", "arbitrary")),
        name="modulation",
    )(cc, w_mod, b_mod.reshape(n_layer, 1, six_d))


def _gelu_tanh(x):
    c = 0.7978845608028654
    return 0.5 * x * (1.0 + jnp.tanh(c * (x + 0.044715 * (x * x * x))))


def _log_sigmoid(z):
    return jnp.minimum(z, 0.0) - jnp.log1p(jnp.exp(-jnp.abs(z)))


def _inproj_kernel(xc_ref, xl_ref, mod_ref, g1_ref, wmain_ref, wlr_ref, wgates_ref, wg2_ref, bg2_ref,
                   lng_ref, lnb_ref, sguw_ref, sgub_ref, wa_ref, bbr_ref,
                   yag_ref, gb_ref, q_ref, k_ref, v_ref, sr_ref, la_ref, sa_ref):
    d = xc_ref.shape[1]
    dk = q_ref.shape[1]
    x = jnp.where(pl.program_id(1) == 0, xc_ref[...], xl_ref[...])
    sh1 = mod_ref[:, 0:d]
    sc1 = mod_ref[:, d:2 * d]
    h = x * lax.rsqrt(jnp.mean(x * x, axis=-1, keepdims=True) + EPS) * g1_ref[...]
    hb = (h * (1.0 + sc1) + sh1).astype(BF16)

    u = _gelu_tanh(_dot(hb, wmain_ref[:, 0:d]))
    vv = _gelu_tanh(_dot(hb, wmain_ref[:, d:2 * d]))
    mu = jnp.mean(vv, axis=-1, keepdims=True)
    vc = vv - mu
    vn = vc * lax.rsqrt(jnp.mean(vc * vc, axis=-1, keepdims=True) + EPS) * lng_ref[...] + lnb_ref[...]
    vnb = vn.astype(BF16)
    gdim = d // A_GROUPS
    for n in range(x.shape[0] // A_CHUNK):
        rs = slice(n * A_CHUNK, (n + 1) * A_CHUNK)
        for g in range(A_GROUPS):
            cs = slice(g * gdim, (g + 1) * gdim)
            mixed = _dot(sguw_ref[g], vnb[rs, cs]) + sgub_ref[:, cs]
            sa_ref[rs, cs] = (u[rs, cs] * mixed).astype(BF16)
    ya = _dot(sa_ref[...], wa_ref[...])
    gates = _dot(hb, wgates_ref[...]) + bbr_ref[...]
    yag_ref[...] = (jax.nn.sigmoid(gates[:, 0:d]) * ya).astype(BF16)
    gb_ref[...] = jax.nn.sigmoid(gates[:, d:2 * d]).astype(BF16)

    o = 2 * d
    head_k = dk // GLA_HEADS
    q_ref[...] = (_dot(hb, wmain_ref[:, o:o + dk]) * (head_k ** -0.5)).astype(BF16)
    k_ref[...] = _dot(hb, wmain_ref[:, o + dk:o + 2 * dk]).astype(BF16)
    o = o + 2 * dk
    v_ref[...] = _dot(hb, wmain_ref[:, o:o + d]).astype(BF16)
    r = _dot(hb, wmain_ref[:, o + d:o + 2 * d])
    sr_ref[...] = (r * jax.nn.sigmoid(r)).astype(BF16)
    lr = _dot(hb, wlr_ref[...]).astype(BF16)
    z = _dot(lr, wg2_ref[...]) + bg2_ref[...]
    la_ref[...] = _log_sigmoid(z) * (1.0 / GLA_TAU)


def _inproj(xc, xl, joined, nt, mod3, g1, wmain, wlr, wgates, wg2, bg2, lng, lnb, sguw, sgub, wa, bbr, n_batch):
    d = xc.shape[1]
    dk = wg2.shape[1] // 2
    nj = nt // n_batch // TM
    ctx_row = n_batch

    def row(b, j):
        return (b * nj + j, 0)

    def modrow(b, j):
        return (jnp.where(j == 0, ctx_row, b), 0, 0)

    tile = lambda w: pl.BlockSpec((TM, w), row)
    outs = [(d, BF16), (d, BF16), (dk, BF16), (dk, BF16), (d, BF16), (d, BF16), (2 * dk, F32)]
    return pl.pallas_call(
        _inproj_kernel,
        grid=(n_batch, nj),
        in_specs=[*_stream_specs(joined, nj, d), pl.BlockSpec((None, 1, mod3.shape[2]), modrow)]
        + [_const_spec(a.shape) for a in (g1, wmain, wlr, wgates, wg2, bg2, lng, lnb, sguw, sgub, wa, bbr)],
        out_specs=[tile(w) for w, _ in outs],
        out_shape=[jax.ShapeDtypeStruct((nt, w), t) for w, t in outs],
        scratch_shapes=[pltpu.VMEM((TM, d), BF16)],
        compiler_params=_params(("arbitrary", "arbitrary")),
        name="inproj",
    )(xc, xl, mod3, g1, wmain, wlr, wgates, wg2, bg2, lng, lnb, sguw, sgub, wa, bbr)


def _gla_tile(q_ref, k_ref, v_ref, la_ref, s_ref, o_ref, reverse):
    rows, dk = q_ref.shape
    dv = v_ref.shape[1]
    hk = dk // GLA_HEADS
    hv = dv // GLA_HEADS
    c = GLA_CHUNK
    ri = lax.broadcasted_iota(jnp.int32, (c, c), 0)
    ci = lax.broadcasted_iota(jnp.int32, (c, c), 1)
    tri = (ci >= ri) if reverse else (ci <= ri)
    rt = lax.broadcasted_iota(jnp.int32, (rows, rows), 0)
    ct = lax.broadcasted_iota(jnp.int32, (rows, rows), 1)
    in_chunk = (rt // c) == (ct // c)
    tri_t = jnp.where(jnp.logical_and(in_chunk, (ct >= rt) if reverse else (ct <= rt)), 1.0, 0.0).astype(BF16)
    la = la_ref[...]
    la_hi = la.astype(BF16)
    la_lo = (la - la_hi.astype(F32)).astype(BF16)
    b_all = _dot(tri_t, la_hi) + _dot(tri_t, la_lo)
    chunks = range(rows // c)
    for n in (reversed(chunks) if reverse else chunks):
        rs = slice(n * c, (n + 1) * c)
        b = b_all[rs, :]
        b_end = b[0:1, :] if reverse else b[c - 1:c, :]
        q = q_ref[rs, :].astype(F32)
        k = k_ref[rs, :].astype(F32)
        q_in = (q * jnp.exp(b)).astype(BF16)
        k_in = (k * jnp.exp(-b)).astype(BF16)
        k_st = (k * jnp.exp(b_end - b)).astype(BF16)
        decay = jnp.exp(b_end)
        for h in range(GLA_HEADS):
            ks = slice(h * hk, (h + 1) * hk)
            vs = slice(h * hv, (h + 1) * hv)
            vh = v_ref[rs, vs]
            att = lax.dot_general(q_in[:, ks], k_in[:, ks], (((1,), (1,)), ((), ())),
                                  preferred_element_type=F32)
            att = jnp.where(tri, att, 0.0).astype(BF16)
            state = s_ref[h]
            o_ref[rs, vs] = _dot(att, vh) + _dot(q_in[:, ks], state.astype(BF16))
            dcol = jnp.transpose(jnp.broadcast_to(decay[:, ks], (hk, hk)))
            dmat = jnp.concatenate([dcol] * (hv // hk), axis=1)
            kv = lax.dot_general(k_st[:, ks], vh, (((0,), (0,)), ((), ())), preferred_element_type=F32)
            s_ref[h] = dmat * state + kv


def _gla_bwd_kernel(q_ref, k_ref, v_ref, la_ref, ob_ref, s_ref, o_sc):
    @pl.when(pl.program_id(1) == 0)
    def _():
        s_ref[...] = jnp.zeros_like(s_ref)

    _gla_tile(q_ref, k_ref, v_ref, la_ref, s_ref, o_sc, reverse=True)
    ob_ref[...] = o_sc[...].astype(BF16)


def _gla_bwd(q, k, v, la, n_batch):
    nt, dk = q.shape
    dv = v.shape[1]
    nj = nt // n_batch // TM

    def row(b, jj):
        return (b * nj + jnp.where(jj == 0, 0, nj - jj), 0)

    def row_la(b, jj):
        return (b * nj + jnp.where(jj == 0, 0, nj - jj), 1)

    return pl.pallas_call(
        _gla_bwd_kernel,
        grid=(n_batch, nj),
        in_specs=[pl.BlockSpec((TM, dk), row), pl.BlockSpec((TM, dk), row), pl.BlockSpec((TM, dv), row),
                  pl.BlockSpec((TM, dk), row_la)],
        out_specs=pl.BlockSpec((TM, dv), row),
        out_shape=jax.ShapeDtypeStruct((nt, dv), BF16),
        scratch_shapes=[pltpu.VMEM((GLA_HEADS, dk // GLA_HEADS, dv // GLA_HEADS), F32),
                        pltpu.VMEM((TM, dv), F32)],
        compiler_params=_params(("arbitrary", "arbitrary")),
        name="gla_bwd",
    )(q, k, v, la)


def _route(logits_t):
    n_tok = logits_t.shape[1]
    eid = lax.broadcasted_iota(jnp.int32, logits_t.shape, 0)
    ex = jnp.exp(logits_t - jnp.max(logits_t, axis=0, keepdims=True))
    p = ex / jnp.sum(ex, axis=0, keepdims=True)
    grp = eid // EXPERTS_PER_GROUP
    none = -1.0
    far = 2 * N_EXPERTS
    best = None
    for g in range(N_EXPERT_GROUPS):
        pg = jnp.where(grp == g, p, none)
        m1 = jnp.max(pg, axis=0, keepdims=True)
        i1 = jnp.min(jnp.where(pg == m1, eid, far), axis=0, keepdims=True)
        pg2 = jnp.where(eid == i1, none, pg)
        m2 = jnp.max(pg2, axis=0, keepdims=True)
        i2 = jnp.min(jnp.where(pg2 == m2, eid, far), axis=0, keepdims=True)
        cand = (m1 + m2, m1, i1, m2, i2)
        if best is None:
            best = cand
        else:
            better = cand[0] > best[0]
            best = tuple(jnp.where(better, c, o) for c, o in zip(cand, best))
    _, m1, i1, m2, i2 = best
    tot = m1 + m2
    sub = lax.broadcasted_iota(jnp.int32, (SUBLANES, n_tok), 0)
    out = jnp.where(sub == 0, i1.astype(F32), 0.0)
    out = jnp.where(sub == 1, i2.astype(F32), out)
    out = jnp.where(sub == 2, m1 / tot, out)
    out = jnp.where(sub == 3, m2 / tot, out)
    return out


def _merge_kernel(xc_ref, xl_ref, mod_ref, q_ref, k_ref, v_ref, la_ref, ob_ref, yag_ref, gb_ref, sr_ref,
                  glag_ref, wb_ref, wout_ref, g2_ref, wr_ref, br_ref,
                  x1_ref, h2_ref, route_ref, routet_ref, s_ref, o_sc):
    @pl.when(pl.program_id(1) == 0)
    def _():
        s_ref[...] = jnp.zeros_like(s_ref)

    _gla_tile(q_ref, k_ref, v_ref, la_ref, s_ref, o_sc, reverse=False)

    d = xc_ref.shape[1]
    hv = d // GLA_HEADS
    o = o_sc[...] + ob_ref[...].astype(F32)
    parts = []
    for h in range(GLA_HEADS):
        oh = o[:, h * hv:(h + 1) * hv]
        parts.append(oh * lax.rsqrt(jnp.mean(oh * oh, axis=-1, keepdims=True) + EPS))
    on = jnp.concatenate(parts, axis=1) * glag_ref[...]
    yb = _dot((on * sr_ref[...].astype(F32)).astype(BF16), wb_ref[...])
    m = yag_ref[...].astype(F32) + gb_ref[...].astype(F32) * yb
    y = _dot(m.astype(BF16), wout_ref[...])
    gt1 = mod_ref[:, 2 * d:3 * d]
    sh2 = mod_ref[:, 3 * d:4 * d]
    sc2 = mod_ref[:, 4 * d:5 * d]
    x1 = jnp.where(pl.program_id(1) == 0, xc_ref[...], xl_ref[...]) + gt1 * y
    x1_ref[...] = x1
    h2 = x1 * lax.rsqrt(jnp.mean(x1 * x1, axis=-1, keepdims=True) + EPS) * g2_ref[...]
    h2 = h2 * (1.0 + sc2) + sh2
    h2_ref[...] = h2
    h_hi = h2.astype(BF16)
    h_lo = (h2 - h_hi.astype(F32)).astype(BF16)
    logits = _dot(h_hi, wr_ref[0]) + _dot(h_lo, wr_ref[0]) + _dot(h_hi, wr_ref[1]) + br_ref[...]
    rt = _route(jnp.transpose(logits)[0:N_EXPERTS, :])
    routet_ref[0] = rt
    sub = lax.broadcasted_iota(jnp.int32, (LANES, rt.shape[1]), 0)
    padded = jnp.zeros((LANES, rt.shape[1]), F32)
    for r in range(4):
        padded = jnp.where(sub == r, rt[r:r + 1, :], padded)
    route_ref[...] = jnp.transpose(padded)


def _merge(xc, xl, joined, nt, mod3, q, k, v, la, ob, yag, gb, sr, glag, wb, wout, g2, wr, br, n_batch):
    d = xc.shape[1]
    dk = q.shape[1]
    nj = nt // n_batch // TM
    ctx_row = n_batch

    def row(b, j):
        return (b * nj + j, 0)

    def modrow(b, j):
        return (jnp.where(j == 0, ctx_row, b), 0, 0)

    tile = lambda w: pl.BlockSpec((TM, w), row)
    return pl.pallas_call(
        _merge_kernel,
        grid=(n_batch, nj),
        in_specs=[*_stream_specs(joined, nj, d), pl.BlockSpec((None, 1, mod3.shape[2]), modrow),
                  tile(dk), tile(dk), tile(d), tile(dk), tile(d), tile(d), tile(d), tile(d)]
        + [_const_spec(a.shape) for a in (glag, wb, wout, g2, wr, br)],
        out_specs=[tile(d), tile(d), tile(LANES),
                   pl.BlockSpec((1, SUBLANES, TM), lambda b, j: (b * nj + j, 0, 0))],
        out_shape=[jax.ShapeDtypeStruct((nt, d), F32), jax.ShapeDtypeStruct((nt, d), F32),
                   jax.ShapeDtypeStruct((nt, LANES), F32),
                   jax.ShapeDtypeStruct((nt // TM, SUBLANES, TM), F32)],
        scratch_shapes=[pltpu.VMEM((GLA_HEADS, dk // GLA_HEADS, d // GLA_HEADS), F32),
                        pltpu.VMEM((TM, d), F32)],
        compiler_params=_params(("arbitrary", "arbitrary")),
        name="merge",
    )(xc, xl, mod3, q, k, v, la, ob, yag, gb, sr, glag, wb, wout, g2, wr, br)


def _plan_kernel(rt_ref, dest_ref, be_ref, cnt_ref, pst_ref, cnt_sc, run_sc, pst_sc):
    ph = pl.program_id(0)
    i = pl.program_id(1)
    rows = rt_ref.shape[2]
    sub = lax.broadcasted_iota(jnp.int32, (N_EXPERTS, rows), 0).astype(F32)
    oh0 = jnp.where(sub == rt_ref[0, 0:1, :], 1.0, 0.0)
    oh1 = jnp.where(sub == rt_ref[0, 1:2, :], 1.0, 0.0)
    tot0 = jnp.sum(oh0, axis=1, keepdims=True)
    tot1 = jnp.sum(oh1, axis=1, keepdims=True)
    wide = lambda col: jnp.broadcast_to(col, (N_EXPERTS, LANES))

    @pl.when(jnp.logical_and(ph == 0, i == 0))
    def _():
        cnt_sc[...] = jnp.zeros_like(cnt_sc)

    @pl.when(ph == 0)
    def _():
        cnt_sc[...] += wide(tot0 + tot1)

    @pl.when(jnp.logical_and(ph == 1, i == 0))
    def _():
        cnt = cnt_sc[...]
        padded = jnp.floor((cnt + (TB - 1)) * (1.0 / TB)) * TB
        ri = lax.broadcasted_iota(jnp.int32, (N_EXPERTS, N_EXPERTS), 0)
        ci = lax.broadcasted_iota(jnp.int32, (N_EXPERTS, N_EXPERTS), 1)
        p_end = jnp.dot(jnp.where(ci <= ri, 1.0, 0.0), padded, preferred_element_type=F32, precision=HIGHEST)
        pst_sc[...] = p_end - padded
        run_sc[...] = jnp.zeros_like(run_sc)
        cnt_ref[...] = cnt
        pst_ref[...] = p_end - padded
        starts = lax.broadcasted_iota(jnp.int32, (N_EXPERTS, be_ref.shape[1]), 1).astype(F32) * TB
        done = jnp.sum(jnp.where(p_end[:, 0:1] <= starts, 1.0, 0.0), axis=0, keepdims=True)
        be_ref[...] = jnp.minimum(done, N_EXPERTS - 1.0).astype(jnp.int32)

    @pl.when(ph == 1)
    def _():
        ri = lax.broadcasted_iota(jnp.int32, (rows, rows), 0)
        ci = lax.broadcasted_iota(jnp.int32, (rows, rows), 1)
        earlier = jnp.where(ri < ci, 1.0, 0.0).astype(BF16)
        c0 = _dot(oh0.astype(BF16), earlier)
        c1 = _dot(oh1.astype(BF16), earlier)
        base = pst_sc[:, 0:1] + run_sc[:, 0:1]
        d0 = jnp.sum(oh0 * (c0 + base), axis=0, keepdims=True)
        d1 = jnp.sum(oh1 * (c1 + base + tot0), axis=0, keepdims=True)
        run_sc[...] += wide(tot0 + tot1)
        dest_ref[0] = jnp.concatenate([d0, d1], axis=0).astype(jnp.int32)


def _plan(route_t, n_blocks):
    n_tiles = route_t.shape[0]
    nbp = -(-n_blocks // LANES) * LANES
    small = jax.ShapeDtypeStruct((N_EXPERTS, LANES), F32)
    return pl.pallas_call(
        _plan_kernel,
        grid=(2, n_tiles),
        in_specs=[pl.BlockSpec((1, SUBLANES, TM), lambda p, i: (i, 0, 0))],
        out_specs=[pl.BlockSpec((1, TOP_K, TM), lambda p, i: (p * i, 0, 0)),
                   _const_spec((1, nbp)), _const_spec(small.shape), _const_spec(small.shape)],
        out_shape=[jax.ShapeDtypeStruct((n_tiles, TOP_K, TM), jnp.int32),
                   jax.ShapeDtypeStruct((1, nbp), jnp.int32), small, small],
        scratch_shapes=[pltpu.VMEM((N_EXPERTS, LANES), F32)] * 3,
        compiler_params=_params(("arbitrary", "arbitrary")),
        name="plan",
    )(route_t)


def _sorted_rows(dest, block_e, cnt, pst, n_tok, n_blocks):
    cnt = cnt[:, 0].astype(jnp.int32)
    pst = pst[:, 0].astype(jnp.int32)
    pad = (cnt + TB - 1) // TB * TB - cnt
    pad_before = jnp.cumsum(pad) - pad
    be = block_e[0, :n_blocks]
    base = TOP_K * n_tok - pst[be] - cnt[be] + pad_before[be]
    init = base[:, None] + jnp.arange(n_blocks * TB, dtype=jnp.int32).reshape(n_blocks, TB)
    tok = jnp.arange(n_tok, dtype=jnp.int32).reshape(-1, 1, TM)
    slot_id = tok + n_tok * jnp.arange(TOP_K, dtype=jnp.int32)[None, :, None]
    slot = init.reshape(-1).at[dest.reshape(-1)].set(slot_id.reshape(-1), unique_indices=True)
    src = jnp.where(slot < TOP_K * n_tok, slot % n_tok, 0)
    return slot.reshape(n_blocks, 1, TB), src.reshape(n_blocks, 1, TB)


def _moe_kernel(be_ref, src_cur, src_nxt, slot_prv, slot_cur, h_hbm, wg_ref, wu_ref, wd_ref, y_hbm,
                x0, x1, y0, y1, gsem, ssem):
    del be_ref
    i = pl.program_id(0)
    last = pl.num_programs(0) - 1
    xs, ys = (x0, x1), (y0, y1)

    def gather_start(idx_ref, dst, sem):
        for r in range(TB):
            pltpu.make_async_copy(h_hbm.at[pl.ds(idx_ref[0, 0, r], 1)], dst.at[pl.ds(r, 1)], sem).start()

    def gather_wait(dst, sem):
        pltpu.make_async_copy(h_hbm.at[pl.ds(0, TB)], dst, sem).wait()

    def scatter_start(idx_ref, src, sem):
        for r in range(TB):
            pltpu.make_async_copy(src.at[pl.ds(r, 1)], y_hbm.at[pl.ds(idx_ref[0, 0, r], 1)], sem).start()

    def scatter_wait(src, sem):
        pltpu.make_async_copy(src, y_hbm.at[pl.ds(0, TB)], sem).wait()

    def experts(x_ref, y_ref):
        x = x_ref[...].astype(BF16)
        g = _dot(x, wg_ref[0])
        u = _dot(x, wu_ref[0])
        a = (g * jax.nn.sigmoid(g) * u).astype(BF16)
        y_ref[...] = _dot(a, wd_ref[0])

    @pl.when(i == 0)
    def _():
        gather_start(src_cur, x0, gsem.at[0])
        gather_wait(x0, gsem.at[0])
        gather_start(src_nxt, x1, gsem.at[1])
        experts(x0, y0)

    for p in range(2):
        @pl.when(jnp.logical_and(i > 0, lax.rem(i, 2) == p))
        def _(p=p):
            q = 1 - p
            gather_wait(xs[p], gsem.at[p])

            @pl.when(i >= 2)
            def _():
                scatter_wait(ys[p], ssem.at[p])

            gather_start(src_nxt, xs[q], gsem.at[q])
            scatter_start(slot_prv, ys[q], ssem.at[q])
            experts(xs[p], ys[p])

            @pl.when(i == last)
            def _():
                scatter_start(slot_cur, ys[p], ssem.at[p])
                scatter_wait(ys[q], ssem.at[q])
                scatter_wait(ys[p], ssem.at[p])
                gather_wait(xs[q], gsem.at[q])


def _moe(block_e, slot3, src3, h2, wg, wu, wd):
    nb = slot3.shape[0]
    d = h2.shape[1]
    de = wg.shape[2]
    idx_spec = lambda f: pl.BlockSpec((1, 1, TB), f, memory_space=pltpu.SMEM)
    cur = lambda i, be: (i, 0, 0)
    nxt = lambda i, be: (jnp.minimum(i + 1, nb - 1), 0, 0)
    prv = lambda i, be: (jnp.maximum(i - 1, 0), 0, 0)
    vm = pltpu.VMEM((TB, d), F32)
    return pl.pallas_call(
        _moe_kernel,
        grid_spec=pltpu.PrefetchScalarGridSpec(
            num_scalar_prefetch=1,
            grid=(nb,),
            in_specs=[idx_spec(cur), idx_spec(nxt), idx_spec(prv), idx_spec(cur),
                      pl.BlockSpec(memory_space=pl.ANY),
                      pl.BlockSpec((1, d, de), lambda i, be: (be[i], 0, 0)),
                      pl.BlockSpec((1, d, de), lambda i, be: (be[i], 0, 0)),
                      pl.BlockSpec((1, de, d), lambda i, be: (be[i], 0, 0))],
            out_specs=pl.BlockSpec(memory_space=pl.ANY),
            scratch_shapes=[vm, vm, vm, vm, pltpu.SemaphoreType.DMA((2,)), pltpu.SemaphoreType.DMA((2,))],
        ),
        out_shape=jax.ShapeDtypeStruct((nb * TB, d), F32),
        compiler_params=_params(("arbitrary",)),
        name="moe",
    )(block_e, src3, src3, slot3, slot3, h2, wg, wu, wd)


def _combine_kernel(y0_ref, y1_ref, x1_ref, route_ref, mod_ref, gf_ref, o_ref, *, final_norm):
    d = x1_ref.shape[1]
    y = route_ref[:, 2:3] * y0_ref[...] + route_ref[:, 3:4] * y1_ref[...]
    x2 = x1_ref[...] + mod_ref[:, 5 * d:6 * d] * y
    if final_norm:
        x2 = x2 * lax.rsqrt(jnp.mean(x2 * x2, axis=-1, keepdims=True) + EPS) * gf_ref[...]
    o_ref[...] = x2


def _combine(y_rows, x1, route, mod3, gf, n_batch, final_norm):
    nt, d = x1.shape
    nj = nt // n_batch // TM
    ctx_row = n_batch

    def modrow(i):
        return (jnp.where(i % nj == 0, ctx_row, i // nj), 0, 0)

    if final_norm:
        out_rows = nt - n_batch * TM
        out_map = lambda i: ((i // nj) * (nj - 1) + jnp.maximum(i % nj - 1, 0), 0)
    else:
        out_rows = nt
        out_map = lambda i: (i, 0)
    return pl.pallas_call(
        functools.partial(_combine_kernel, final_norm=final_norm),
        grid=(nt // TM,),
        in_specs=[pl.BlockSpec((TM, d), lambda i: (i, 0)),
                  pl.BlockSpec((TM, d), lambda i: (nt // TM + i, 0)),
                  pl.BlockSpec((TM, d), lambda i: (i, 0)),
                  pl.BlockSpec((TM, LANES), lambda i: (i, 0)),
                  pl.BlockSpec((None, 1, mod3.shape[2]), modrow),
                  _const_spec(gf.shape)],
        out_specs=pl.BlockSpec((TM, d), out_map),
        out_shape=jax.ShapeDtypeStruct((out_rows, d), F32),
        compiler_params=_params(("arbitrary",)),
        name="combine",
    )(y_rows, y_rows, x1, route, mod3, gf)


def kernel(x, c, ctx, c_ctx, w_mod, b_mod, g_norm1, g_norm2, w_in, w_gate2, b_gate2, gla_norm_g,
           sgu_ln_g, sgu_ln_b, sgu_w, sgu_b, w_branch_a, w_branch_b, b_branch, w_out,
           w_router, b_router, w_exp_gate, w_exp_up, w_exp_down, g_final):
    n_batch, seq, d = x.shape
    ctx_len = ctx.shape[1]
    depth = w_mod.shape[0]
    dk = w_gate2.shape[3]
    rank_lr = w_gate2.shape[2]
    assert ctx_len == TM and seq % TM == 0 and n_batch < MOD_ROWS

    nt = n_batch * (ctx_len + seq)
    streams = (ctx.reshape(-1, d), x.reshape(-1, d), False)
    n_blocks = TOP_K * nt // TB + N_EXPERTS
    cc = jnp.zeros((MOD_ROWS, d), F32).at[:n_batch].set(c).at[n_batch].set(c_ctx)
    mod = _modulation(cc, w_mod, b_mod)

    wr = jnp.zeros((d, LANES), F32).at[:, :N_EXPERTS].set(w_router)
    wr_hi = wr.astype(BF16)
    wr = jnp.stack([wr_hi, (wr - wr_hi.astype(F32)).astype(BF16)])
    br = jnp.zeros((1, LANES), F32).at[0, :N_EXPERTS].set(b_router)
    row = lambda a: a.reshape(1, -1)
    n_main = 2 * d + 2 * dk + 2 * d

    out = None
    for l in range(depth):
        last = l == depth - 1
        mod3 = mod[l].reshape(MOD_ROWS, 1, 6 * d)
        wmain = w_in[l][:, :n_main].astype(BF16)
        wlr = w_in[l][:, n_main:n_main + 2 * rank_lr].astype(BF16)
        wgates = w_in[l][:, n_main + 2 * rank_lr:].astype(BF16)
        wg2 = jnp.zeros((2 * rank_lr, 2 * dk), F32)
        wg2 = wg2.at[:rank_lr, :dk].set(w_gate2[l, 0]).at[rank_lr:, dk:].set(w_gate2[l, 1]).astype(BF16)
        bg2 = b_gate2[l].reshape(1, 2 * dk)
        sgub = jnp.repeat(sgu_b[l].T, d // A_GROUPS, axis=1)

        yag, gb, q, k, v, sr, la = _inproj(
            *streams, nt, mod3, row(g_norm1[l]), wmain, wlr, wgates, wg2, bg2, row(sgu_ln_g[l]), row(sgu_ln_b[l]),
            sgu_w[l].astype(BF16), sgub, w_branch_a[l].astype(BF16), row(b_branch[l]), n_batch)
        ob = _gla_bwd(q, k, v, la, n_batch)
        x1, h2, route, route_t = _merge(
            *streams, nt, mod3, q, k, v, la, ob, yag, gb, sr, row(gla_norm_g[l]), w_branch_b[l].astype(BF16),
            w_out[l].astype(BF16), row(g_norm2[l]), wr, br, n_batch)
        dest, block_e, cnt, pst = _plan(route_t, n_blocks)
        slot3, src3 = _sorted_rows(dest, block_e, cnt, pst, nt, n_blocks)
        y_rows = _moe(block_e.reshape(-1), slot3, src3, h2, w_exp_gate[l].astype(BF16),
                      w_exp_up[l].astype(BF16), w_exp_down[l].astype(BF16))
        res = _combine(y_rows, x1, route, mod3, row(g_final), n_batch, final_norm=last)
        if last:
            out = res.reshape(n_batch, seq, d)
        else:
            streams = (res, res, True)
    return out
```

```python
import functools

import jax
import jax.numpy as jnp
from jax import lax
from jax.experimental import pallas as pl
from jax.experimental.pallas import tpu as pltpu

F32 = jnp.float32
BF16 = jnp.bfloat16
HIGHEST = lax.Precision.HIGHEST

A_CHUNK = 128
A_GROUPS = 8
GLA_HEADS = 4
GLA_TAU = 16.0
GLA_CHUNK = 64
N_EXPERTS = 16
N_EXPERT_GROUPS = 4
EXPERTS_PER_GROUP = N_EXPERTS // N_EXPERT_GROUPS
TOP_K = 2
EPS = 1e-6

LANES = 128
SUBLANES = 8
TM = 256
TB = 256
MOD_ROWS = 16
VMEM_LIMIT = 56 * 1024 * 1024


def _dot(a, b):
    return jnp.dot(a, b, preferred_element_type=F32)


def _const_spec(shape):
    nd = len(shape)
    return pl.BlockSpec(shape, lambda *_: (0,) * nd)


def _params(sem):
    return pltpu.CompilerParams(dimension_semantics=sem, vmem_limit_bytes=VMEM_LIMIT)


def _stream_specs(joined, nj, d):
    if joined:
        ctx_map = lambda b, j: (b * nj, 0)
        lat_map = lambda b, j: (b * nj + jnp.maximum(j, 1), 0)
    else:
        ctx_map = lambda b, j: (b, 0)
        lat_map = lambda b, j: (b * (nj - 1) + jnp.maximum(j - 1, 0), 0)
    return pl.BlockSpec((TM, d), ctx_map), pl.BlockSpec((TM, d), lat_map)


def _mod_kernel(cc_ref, w_ref, b_ref, o_ref):
    cc = cc_ref[...]
    s = cc * jax.nn.sigmoid(cc)
    o_ref[0] = jnp.dot(s, w_ref[0], preferred_element_type=F32, precision=HIGHEST) + b_ref[0]


def _modulation(cc, w_mod, b_mod):
    n_layer, d, six_d = w_mod.shape
    return pl.pallas_call(
        _mod_kernel,
        grid=(n_layer, six_d // d),
        in_specs=[
            pl.BlockSpec((MOD_ROWS, d), lambda l, j: (0, 0)),
            pl.BlockSpec((1, d, d), lambda l, j: (l, 0, j)),
            pl.BlockSpec((1, 1, d), lambda l, j: (l, 0, j)),
        ],
        out_specs=pl.BlockSpec((1, MOD_ROWS, d), lambda l, j: (l, 0, j)),
        out_shape=jax.ShapeDtypeStruct((n_layer, MOD_ROWS, six_d), F32),
        compiler_params=_params(("arbitrary", "arbitrary")),
        name="modulation",
    )(cc, w_mod, b_mod.reshape(n_layer, 1, six_d))


def _gelu_tanh(x):
    c = 0.7978845608028654
    return 0.5 * x * (1.0 + jnp.tanh(c * (x + 0.044715 * (x * x * x))))


def _log_sigmoid(z):
    return jnp.minimum(z, 0.0) - jnp.log1p(jnp.exp(-jnp.abs(z)))


def _inproj_kernel(xc_ref, xl_ref, mod_ref, g1_ref, wmain_ref, wlr_ref, wgates_ref, wg2_ref, bg2_ref,
                   lng_ref, lnb_ref, sguw_ref, sgub_ref, wa_ref, bbr_ref,
                   yag_ref, gb_ref, q_ref, k_ref, v_ref, sr_ref, la_ref, sa_ref):
    d = xc_ref.shape[1]
    dk = q_ref.shape[1]
    x = jnp.where(pl.program_id(1) == 0, xc_ref[...], xl_ref[...])
    sh1 = mod_ref[:, 0:d]
    sc1 = mod_ref[:, d:2 * d]
    h = x * lax.rsqrt(jnp.mean(x * x, axis=-1, keepdims=True) + EPS) * g1_ref[...]
    hb = (h * (1.0 + sc1) + sh1).astype(BF16)

    u = _gelu_tanh(_dot(hb, wmain_ref[:, 0:d]))
    vv = _gelu_tanh(_dot(hb, wmain_ref[:, d:2 * d]))
    mu = jnp.mean(vv, axis=-1, keepdims=True)
    vc = vv - mu
    vn = vc * lax.rsqrt(jnp.mean(vc * vc, axis=-1, keepdims=True) + EPS) * lng_ref[...] + lnb_ref[...]
    vnb = vn.astype(BF16)
    gdim = d // A_GROUPS
    for n in range(x.shape[0] // A_CHUNK):
        rs = slice(n * A_CHUNK, (n + 1) * A_CHUNK)
        for g in range(A_GROUPS):
            cs = slice(g * gdim, (g + 1) * gdim)
            mixed = _dot(sguw_ref[g], vnb[rs, cs]) + sgub_ref[:, cs]
            sa_ref[rs, cs] = (u[rs, cs] * mixed).astype(BF16)
    ya = _dot(sa_ref[...], wa_ref[...])
    gates = _dot(hb, wgates_ref[...]) + bbr_ref[...]
    yag_ref[...] = (jax.nn.sigmoid(gates[:, 0:d]) * ya).astype(BF16)
    gb_ref[...] = jax.nn.sigmoid(gates[:, d:2 * d]).astype(BF16)

    o = 2 * d
    head_k = dk // GLA_HEADS
    q_ref[...] = (_dot(hb, wmain_ref[:, o:o + dk]) * (head_k ** -0.5)).astype(BF16)
    k_ref[...] = _dot(hb, wmain_ref[:, o + dk:o + 2 * dk]).astype(BF16)
    o = o + 2 * dk
    v_ref[...] = _dot(hb, wmain_ref[:, o:o + d]).astype(BF16)
    r = _dot(hb, wmain_ref[:, o + d:o + 2 * d])
    sr_ref[...] = (r * jax.nn.sigmoid(r)).astype(BF16)
    lr = _dot(hb, wlr_ref[...]).astype(BF16)
    z = _dot(lr, wg2_ref[...]) + bg2_ref[...]
    la_ref[...] = _log_sigmoid(z) * (1.0 / GLA_TAU)


def _inproj(xc, xl, joined, nt, mod3, g1, wmain, wlr, wgates, wg2, bg2, lng, lnb, sguw, sgub, wa, bbr, n_batch):
    d = xc.shape[1]
    dk = wg2.shape[1] // 2
    nj = nt // n_batch // TM
    ctx_row = n_batch

    def row(b, j):
        return (b * nj + j, 0)

    def modrow(b, j):
        return (jnp.where(j == 0, ctx_row, b), 0, 0)

    tile = lambda w: pl.BlockSpec((TM, w), row)
    outs = [(d, BF16), (d, BF16), (dk, BF16), (dk, BF16), (d, BF16), (d, BF16), (2 * dk, F32)]
    return pl.pallas_call(
        _inproj_kernel,
        grid=(n_batch, nj),
        in_specs=[*_stream_specs(joined, nj, d), pl.BlockSpec((None, 1, mod3.shape[2]), modrow)]
        + [_const_spec(a.shape) for a in (g1, wmain, wlr, wgates, wg2, bg2, lng, lnb, sguw, sgub, wa, bbr)],
        out_specs=[tile(w) for w, _ in outs],
        out_shape=[jax.ShapeDtypeStruct((nt, w), t) for w, t in outs],
        scratch_shapes=[pltpu.VMEM((TM, d), BF16)],
        compiler_params=_params(("arbitrary", "arbitrary")),
        name="inproj",
    )(xc, xl, mod3, g1, wmain, wlr, wgates, wg2, bg2, lng, lnb, sguw, sgub, wa, bbr)


def _gla_tile(q_ref, k_ref, v_ref, la_ref, s_ref, o_ref, reverse):
    rows, dk = q_ref.shape
    dv = v_ref.shape[1]
    hk = dk // GLA_HEADS
    hv = dv // GLA_HEADS
    c = GLA_CHUNK
    ri = lax.broadcasted_iota(jnp.int32, (c, c), 0)
    ci = lax.broadcasted_iota(jnp.int32, (c, c), 1)
    tri = (ci >= ri) if reverse else (ci <= ri)
    rt = lax.broadcasted_iota(jnp.int32, (rows, rows), 0)
    ct = lax.broadcasted_iota(jnp.int32, (rows, rows), 1)
    in_chunk = (rt // c) == (ct // c)
    tri_t = jnp.where(jnp.logical_and(in_chunk, (ct >= rt) if reverse else (ct <= rt)), 1.0, 0.0).astype(BF16)
    la = la_ref[...]
    la_hi = la.astype(BF16)
    la_lo = (la - la_hi.astype(F32)).astype(BF16)
    b_all = _dot(tri_t, la_hi) + _dot(tri_t, la_lo)
    chunks = range(rows // c)
    for n in (reversed(chunks) if reverse else chunks):
        rs = slice(n * c, (n + 1) * c)
        b = b_all[rs, :]
        b_end = b[0:1, :] if reverse else b[c - 1:c, :]
        q = q_ref[rs, :].astype(F32)
        k = k_ref[rs, :].astype(F32)
        q_in = (q * jnp.exp(b)).astype(BF16)
        k_in = (k * jnp.exp(-b)).astype(BF16)
        k_st = (k * jnp.exp(b_end - b)).astype(BF16)
        decay = jnp.exp(b_end)
        for h in range(GLA_HEADS):
            ks = slice(h * hk, (h + 1) * hk)
            vs = slice(h * hv, (h + 1) * hv)
            vh = v_ref[rs, vs]
            att = lax.dot_general(q_in[:, ks], k_in[:, ks], (((1,), (1,)), ((), ())),
                                  preferred_element_type=F32)
            att = jnp.where(tri, att, 0.0).astype(BF16)
            state = s_ref[h]
            o_ref[rs, vs] = _dot(att, vh) + _dot(q_in[:, ks], state.astype(BF16))
            dcol = jnp.transpose(jnp.broadcast_to(decay[:, ks], (hk, hk)))
            dmat = jnp.concatenate([dcol] * (hv // hk), axis=1)
            kv = lax.dot_general(k_st[:, ks], vh, (((0,), (0,)), ((), ())), preferred_element_type=F32)
            s_ref[h] = dmat * state + kv


def _gla_bwd_kernel(q_ref, k_ref, v_ref, la_ref, ob_ref, s_ref, o_sc):
    @pl.when(pl.program_id(1) == 0)
    def _():
        s_ref[...] = jnp.zeros_like(s_ref)

    _gla_tile(q_ref, k_ref, v_ref, la_ref, s_ref, o_sc, reverse=True)
    ob_ref[...] = o_sc[...].astype(BF16)


def _gla_bwd(q, k, v, la, n_batch):
    nt, dk = q.shape
    dv = v.shape[1]
    nj = nt // n_batch // TM

    def row(b, jj):
        return (b * nj + jnp.where(jj == 0, 0, nj - jj), 0)

    def row_la(b, jj):
        return (b * nj + jnp.where(jj == 0, 0, nj - jj), 1)

    return pl.pallas_call(
        _gla_bwd_kernel,
        grid=(n_batch, nj),
        in_specs=[pl.BlockSpec((TM, dk), row), pl.BlockSpec((TM, dk), row), pl.BlockSpec((TM, dv), row),
                  pl.BlockSpec((TM, dk), row_la)],
        out_specs=pl.BlockSpec((TM, dv), row),
        out_shape=jax.ShapeDtypeStruct((nt, dv), BF16),
        scratch_shapes=[pltpu.VMEM((GLA_HEADS, dk // GLA_HEADS, dv // GLA_HEADS), F32),
                        pltpu.VMEM((TM, dv), F32)],
        compiler_params=_params(("arbitrary", "arbitrary")),
        name="gla_bwd",
    )(q, k, v, la)


def _route(logits_t):
    n_tok = logits_t.shape[1]
    eid = lax.broadcasted_iota(jnp.int32, logits_t.shape, 0)
    ex = jnp.exp(logits_t - jnp.max(logits_t, axis=0, keepdims=True))
    p = ex / jnp.sum(ex, axis=0, keepdims=True)
    grp = eid // EXPERTS_PER_GROUP
    none = -1.0
    far = 2 * N_EXPERTS
    best = None
    for g in range(N_EXPERT_GROUPS):
        pg = jnp.where(grp == g, p, none)
        m1 = jnp.max(pg, axis=0, keepdims=True)
        i1 = jnp.min(jnp.where(pg == m1, eid, far), axis=0, keepdims=True)
        pg2 = jnp.where(eid == i1, none, pg)
        m2 = jnp.max(pg2, axis=0, keepdims=True)
        i2 = jnp.min(jnp.where(pg2 == m2, eid, far), axis=0, keepdims=True)
        cand = (m1 + m2, m1, i1, m2, i2)
        if best is None:
            best = cand
        else:
            better = cand[0] > best[0]
            best = tuple(jnp.where(better, c, o) for c, o in zip(cand, best))
    _, m1, i1, m2, i2 = best
    tot = m1 + m2
    sub = lax.broadcasted_iota(jnp.int32, (SUBLANES, n_tok), 0)
    out = jnp.where(sub == 0, i1.astype(F32), 0.0)
    out = jnp.where(sub == 1, i2.astype(F32), out)
    out = jnp.where(sub == 2, m1 / tot, out)
    out = jnp.where(sub == 3, m2 / tot, out)
    return out


def _merge_kernel(xc_ref, xl_ref, mod_ref, q_ref, k_ref, v_ref, la_ref, ob_ref, yag_ref, gb_ref, sr_ref,
                  glag_ref, wb_ref, wout_ref, g2_ref, wr_ref, br_ref,
                  x1_ref, h2_ref, route_ref, routet_ref, s_ref, o_sc):
    @pl.when(pl.program_id(1) == 0)
    def _():
        s_ref[...] = jnp.zeros_like(s_ref)

    _gla_tile(q_ref, k_ref, v_ref, la_ref, s_ref, o_sc, reverse=False)

    d = xc_ref.shape[1]
    hv = d // GLA_HEADS
    o = o_sc[...] + ob_ref[...].astype(F32)
    parts = []
    for h in range(GLA_HEADS):
        oh = o[:, h * hv:(h + 1) * hv]
        parts.append(oh * lax.rsqrt(jnp.mean(oh * oh, axis=-1, keepdims=True) + EPS))
    on = jnp.concatenate(parts, axis=1) * glag_ref[...]
    yb = _dot((on * sr_ref[...].astype(F32)).astype(BF16), wb_ref[...])
    m = yag_ref[...].astype(F32) + gb_ref[...].astype(F32) * yb
    y = _dot(m.astype(BF16), wout_ref[...])
    gt1 = mod_ref[:, 2 * d:3 * d]
    sh2 = mod_ref[:, 3 * d:4 * d]
    sc2 = mod_ref[:, 4 * d:5 * d]
    x1 = jnp.where(pl.program_id(1) == 0, xc_ref[...], xl_ref[...]) + gt1 * y
    x1_ref[...] = x1
    h2 = x1 * lax.rsqrt(jnp.mean(x1 * x1, axis=-1, keepdims=True) + EPS) * g2_ref[...]
    h2 = h2 * (1.0 + sc2) + sh2
    h2_ref[...] = h2
    h_hi = h2.astype(BF16)
    h_lo = (h2 - h_hi.astype(F32)).astype(BF16)
    logits = _dot(h_hi, wr_ref[0]) + _dot(h_lo, wr_ref[0]) + _dot(h_hi, wr_ref[1]) + br_ref[...]
    rt = _route(jnp.transpose(logits)[0:N_EXPERTS, :])
    routet_ref[0] = rt
    sub = lax.broadcasted_iota(jnp.int32, (LANES, rt.shape[1]), 0)
    padded = jnp.zeros((LANES, rt.shape[1]), F32)
    for r in range(4):
        padded = jnp.where(sub == r, rt[r:r + 1, :], padded)
    route_ref[...] = jnp.transpose(padded)


def _merge(xc, xl, joined, nt, mod3, q, k, v, la, ob, yag, gb, sr, glag, wb, wout, g2, wr, br, n_batch):
    d = xc.shape[1]
    dk = q.shape[1]
    nj = nt // n_batch // TM
    ctx_row = n_batch

    def row(b, j):
        return (b * nj + j, 0)

    def modrow(b, j):
        return (jnp.where(j == 0, ctx_row, b), 0, 0)

    tile = lambda w: pl.BlockSpec((TM, w), row)
    return pl.pallas_call(
        _merge_kernel,
        grid=(n_batch, nj),
        in_specs=[*_stream_specs(joined, nj, d), pl.BlockSpec((None, 1, mod3.shape[2]), modrow),
                  tile(dk), tile(dk), tile(d), tile(dk), tile(d), tile(d), tile(d), tile(d)]
        + [_const_spec(a.shape) for a in (glag, wb, wout, g2, wr, br)],
        out_specs=[tile(d), tile(d), tile(LANES),
                   pl.BlockSpec((1, SUBLANES, TM), lambda b, j: (b * nj + j, 0, 0))],
        out_shape=[jax.ShapeDtypeStruct((nt, d), F32), jax.ShapeDtypeStruct((nt, d), F32),
                   jax.ShapeDtypeStruct((nt, LANES), F32),
                   jax.ShapeDtypeStruct((nt // TM, SUBLANES, TM), F32)],
        scratch_shapes=[pltpu.VMEM((GLA_HEADS, dk // GLA_HEADS, d // GLA_HEADS), F32),
                        pltpu.VMEM((TM, d), F32)],
        compiler_params=_params(("arbitrary", "arbitrary")),
        name="merge",
    )(xc, xl, mod3, q, k, v, la, ob, yag, gb, sr, glag, wb, wout, g2, wr, br)


def _plan_kernel(rt_ref, dest_ref, be_ref, cnt_ref, pst_ref):
    n_tiles, _, rows = rt_ref.shape
    sub = lax.broadcasted_iota(jnp.int32, (N_EXPERTS, rows), 0).astype(F32)
    wide = lambda col: jnp.broadcast_to(col, (N_EXPERTS, LANES))

    def one_hots(i):
        rt = rt_ref[i]
        return jnp.where(sub == rt[0:1, :], 1.0, 0.0), jnp.where(sub == rt[1:2, :], 1.0, 0.0)

    def count(i, acc):
        oh0, oh1 = one_hots(i)
        return acc + jnp.sum(oh0 + oh1, axis=1, keepdims=True)

    cnt = wide(lax.fori_loop(0, n_tiles, count, jnp.zeros((N_EXPERTS, 1), F32)))
    padded = jnp.floor((cnt + (TB - 1)) * (1.0 / TB)) * TB
    ri = lax.broadcasted_iota(jnp.int32, (N_EXPERTS, N_EXPERTS), 0)
    ci = lax.broadcasted_iota(jnp.int32, (N_EXPERTS, N_EXPERTS), 1)
    p_end = jnp.dot(jnp.where(ci <= ri, 1.0, 0.0), padded, preferred_element_type=F32, precision=HIGHEST)
    p_start = p_end - padded
    cnt_ref[...] = cnt
    pst_ref[...] = p_start
    starts = lax.broadcasted_iota(jnp.int32, (N_EXPERTS, be_ref.shape[1]), 1).astype(F32) * TB
    done = jnp.sum(jnp.where(p_end[:, 0:1] <= starts, 1.0, 0.0), axis=0, keepdims=True)
    be_ref[...] = jnp.minimum(done, N_EXPERTS - 1.0).astype(jnp.int32)

    rr = lax.broadcasted_iota(jnp.int32, (rows, rows), 0)
    cc = lax.broadcasted_iota(jnp.int32, (rows, rows), 1)
    earlier = jnp.where(rr < cc, 1.0, 0.0).astype(BF16)

    def place(i, run):
        oh0, oh1 = one_hots(i)
        c0 = _dot(oh0.astype(BF16), earlier)
        c1 = _dot(oh1.astype(BF16), earlier)
        tot0 = jnp.sum(oh0, axis=1, keepdims=True)
        base = p_start[:, 0:1] + run
        d0 = jnp.sum(oh0 * (c0 + base), axis=0, keepdims=True)
        d1 = jnp.sum(oh1 * (c1 + base + tot0), axis=0, keepdims=True)
        dest_ref[i] = jnp.concatenate([d0, d1], axis=0).astype(jnp.int32)
        return run + tot0 + jnp.sum(oh1, axis=1, keepdims=True)

    lax.fori_loop(0, n_tiles, place, jnp.zeros((N_EXPERTS, 1), F32))


def _plan(route_t, n_blocks):
    n_tiles = route_t.shape[0]
    nbp = -(-n_blocks // LANES) * LANES
    small = jax.ShapeDtypeStruct((N_EXPERTS, LANES), F32)
    return pl.pallas_call(
        _plan_kernel,
        out_shape=[jax.ShapeDtypeStruct((n_tiles, TOP_K, TM), jnp.int32),
                   jax.ShapeDtypeStruct((1, nbp), jnp.int32), small, small],
        compiler_params=pltpu.CompilerParams(vmem_limit_bytes=VMEM_LIMIT),
        name="plan",
    )(route_t)


def _sorted_rows(dest, block_e, cnt, pst, n_tok, n_blocks):
    cnt = cnt[:, 0].astype(jnp.int32)
    pst = pst[:, 0].astype(jnp.int32)
    pad = (cnt + TB - 1) // TB * TB - cnt
    pad_before = jnp.cumsum(pad) - pad
    be = block_e[0, :n_blocks]
    base = TOP_K * n_tok - pst[be] - cnt[be] + pad_before[be]
    init = base[:, None] + jnp.arange(n_blocks * TB, dtype=jnp.int32).reshape(n_blocks, TB)
    tok = jnp.arange(n_tok, dtype=jnp.int32).reshape(-1, 1, TM)
    slot_id = tok + n_tok * jnp.arange(TOP_K, dtype=jnp.int32)[None, :, None]
    slot = init.reshape(-1).at[dest.reshape(-1)].set(slot_id.reshape(-1), unique_indices=True)
    src = jnp.where(slot < TOP_K * n_tok, slot % n_tok, 0)
    return slot.reshape(n_blocks, 1, TB), src.reshape(n_blocks, 1, TB)


def _moe_kernel(be_ref, src_cur, src_nxt, slot_prv, slot_cur, h_hbm, wg_ref, wu_ref, wd_ref, y_hbm,
                x0, x1, y0, y1, gsem, ssem):
    del be_ref
    i = pl.program_id(0)
    last = pl.num_programs(0) - 1
    xs, ys = (x0, x1), (y0, y1)

    def gather_start(idx_ref, dst, sem):
        for r in range(TB):
            pltpu.make_async_copy(h_hbm.at[pl.ds(idx_ref[0, 0, r], 1)], dst.at[pl.ds(r, 1)], sem).start(
                priority=r % 2)

    def gather_wait(dst, sem):
        pltpu.make_async_copy(h_hbm.at[pl.ds(0, TB)], dst, sem).wait()

    def scatter_start(idx_ref, src, sem):
        for r in range(TB):
            pltpu.make_async_copy(src.at[pl.ds(r, 1)], y_hbm.at[pl.ds(idx_ref[0, 0, r], 1)], sem).start(
                priority=r % 2)

    def scatter_wait(src, sem):
        pltpu.make_async_copy(src, y_hbm.at[pl.ds(0, TB)], sem).wait()

    def experts(x_ref, y_ref):
        x = x_ref[...].astype(BF16)
        g = _dot(x, wg_ref[0])
        u = _dot(x, wu_ref[0])
        a = (g * jax.nn.sigmoid(g) * u).astype(BF16)
        y_ref[...] = _dot(a, wd_ref[0])

    @pl.when(i == 0)
    def _():
        gather_start(src_cur, x0, gsem.at[0])
        gather_wait(x0, gsem.at[0])
        gather_start(src_nxt, x1, gsem.at[1])
        experts(x0, y0)

    for p in range(2):
        @pl.when(jnp.logical_and(i > 0, lax.rem(i, 2) == p))
        def _(p=p):
            q = 1 - p
            gather_wait(xs[p], gsem.at[p])

            @pl.when(i >= 2)
            def _():
                scatter_wait(ys[p], ssem.at[p])

            gather_start(src_nxt, xs[q], gsem.at[q])
            scatter_start(slot_prv, ys[q], ssem.at[q])
            experts(xs[p], ys[p])

            @pl.when(i == last)
            def _():
                scatter_start(slot_cur, ys[p], ssem.at[p])
                scatter_wait(ys[q], ssem.at[q])
                scatter_wait(ys[p], ssem.at[p])
                gather_wait(xs[q], gsem.at[q])


def _moe(block_e, slot3, src3, h2, wg, wu, wd):
    nb = slot3.shape[0]
    d = h2.shape[1]
    de = wg.shape[2]
    idx_spec = lambda f: pl.BlockSpec((1, 1, TB), f, memory_space=pltpu.SMEM)
    cur = lambda i, be: (i, 0, 0)
    nxt = lambda i, be: (jnp.minimum(i + 1, nb - 1), 0, 0)
    prv = lambda i, be: (jnp.maximum(i - 1, 0), 0, 0)
    vm = pltpu.VMEM((TB, d), F32)
    return pl.pallas_call(
        _moe_kernel,
        grid_spec=pltpu.PrefetchScalarGridSpec(
            num_scalar_prefetch=1,
            grid=(nb,),
            in_specs=[idx_spec(cur), idx_spec(nxt), idx_spec(prv), idx_spec(cur),
                      pl.BlockSpec(memory_space=pl.ANY),
                      pl.BlockSpec((1, d, de), lambda i, be: (be[i], 0, 0)),
                      pl.BlockSpec((1, d, de), lambda i, be: (be[i], 0, 0)),
                      pl.BlockSpec((1, de, d), lambda i, be: (be[i], 0, 0))],
            out_specs=pl.BlockSpec(memory_space=pl.ANY),
            scratch_shapes=[vm, vm, vm, vm, pltpu.SemaphoreType.DMA((2,)), pltpu.SemaphoreType.DMA((2,))],
        ),
        out_shape=jax.ShapeDtypeStruct((nb * TB, d), F32),
        compiler_params=_params(("arbitrary",)),
        name="moe",
    )(block_e, src3, src3, slot3, slot3, h2, wg, wu, wd)


def _combine_kernel(y0_ref, y1_ref, x1_ref, route_ref, mod_ref, gf_ref, o_ref, *, final_norm):
    d = x1_ref.shape[1]
    y = route_ref[:, 2:3] * y0_ref[...] + route_ref[:, 3:4] * y1_ref[...]
    x2 = x1_ref[...] + mod_ref[:, 5 * d:6 * d] * y
    if final_norm:
        x2 = x2 * lax.rsqrt(jnp.mean(x2 * x2, axis=-1, keepdims=True) + EPS) * gf_ref[...]
    o_ref[...] = x2


def _combine(y_rows, x1, route, mod3, gf, n_batch, final_norm):
    nt, d = x1.shape
    nj = nt // n_batch // TM
    ctx_row = n_batch

    def modrow(i):
        return (jnp.where(i % nj == 0, ctx_row, i // nj), 0, 0)

    if final_norm:
        out_rows = nt - n_batch * TM
        out_map = lambda i: ((i // nj) * (nj - 1) + jnp.maximum(i % nj - 1, 0), 0)
    else:
        out_rows = nt
        out_map = lambda i: (i, 0)
    return pl.pallas_call(
        functools.partial(_combine_kernel, final_norm=final_norm),
        grid=(nt // TM,),
        in_specs=[pl.BlockSpec((TM, d), lambda i: (i, 0)),
                  pl.BlockSpec((TM, d), lambda i: (nt // TM + i, 0)),
                  pl.BlockSpec((TM, d), lambda i: (i, 0)),
                  pl.BlockSpec((TM, LANES), lambda i: (i, 0)),
                  pl.BlockSpec((None, 1, mod3.shape[2]), modrow),
                  _const_spec(gf.shape)],
        out_specs=pl.BlockSpec((TM, d), out_map),
        out_shape=jax.ShapeDtypeStruct((out_rows, d), F32),
        compiler_params=_params(("arbitrary",)),
        name="combine",
    )(y_rows, y_rows, x1, route, mod3, gf)


def kernel(x, c, ctx, c_ctx, w_mod, b_mod, g_norm1, g_norm2, w_in, w_gate2, b_gate2, gla_norm_g,
           sgu_ln_g, sgu_ln_b, sgu_w, sgu_b, w_branch_a, w_branch_b, b_branch, w_out,
           w_router, b_router, w_exp_gate, w_exp_up, w_exp_down, g_final):
    n_batch, seq, d = x.shape
    ctx_len = ctx.shape[1]
    depth = w_mod.shape[0]
    dk = w_gate2.shape[3]
    rank_lr = w_gate2.shape[2]
    assert ctx_len == TM and seq % TM == 0 and n_batch < MOD_ROWS

    nt = n_batch * (ctx_len + seq)
    streams = (ctx.reshape(-1, d), x.reshape(-1, d), False)
    n_blocks = TOP_K * nt // TB + N_EXPERTS
    cc = jnp.zeros((MOD_ROWS, d), F32).at[:n_batch].set(c).at[n_batch].set(c_ctx)
    mod = _modulation(cc, w_mod, b_mod)

    wr = jnp.zeros((d, LANES), F32).at[:, :N_EXPERTS].set(w_router)
    wr_hi = wr.astype(BF16)
    wr = jnp.stack([wr_hi, (wr - wr_hi.astype(F32)).astype(BF16)])
    br = jnp.zeros((1, LANES), F32).at[0, :N_EXPERTS].set(b_router)
    row = lambda a: a.reshape(1, -1)
    n_main = 2 * d + 2 * dk + 2 * d

    out = None
    for l in range(depth):
        last = l == depth - 1
        mod3 = mod[l].reshape(MOD_ROWS, 1, 6 * d)
        wmain = w_in[l][:, :n_main].astype(BF16)
        wlr = w_in[l][:, n_main:n_main + 2 * rank_lr].astype(BF16)
        wgates = w_in[l][:, n_main + 2 * rank_lr:].astype(BF16)
        wg2 = jnp.zeros((2 * rank_lr, 2 * dk), F32)
        wg2 = wg2.at[:rank_lr, :dk].set(w_gate2[l, 0]).at[rank_lr:, dk:].set(w_gate2[l, 1]).astype(BF16)
        bg2 = b_gate2[l].reshape(1, 2 * dk)
        sgub = jnp.repeat(sgu_b[l].T, d // A_GROUPS, axis=1)

        yag, gb, q, k, v, sr, la = _inproj(
            *streams, nt, mod3, row(g_norm1[l]), wmain, wlr, wgates, wg2, bg2, row(sgu_ln_g[l]), row(sgu_ln_b[l]),
            sgu_w[l].astype(BF16), sgub, w_branch_a[l].astype(BF16), row(b_branch[l]), n_batch)
        ob = _gla_bwd(q, k, v, la, n_batch)
        x1, h2, route, route_t = _merge(
            *streams, nt, mod3, q, k, v, la, ob, yag, gb, sr, row(gla_norm_g[l]), w_branch_b[l].astype(BF16),
            w_out[l].astype(BF16), row(g_norm2[l]), wr, br, n_batch)
        dest, block_e, cnt, pst = _plan(route_t, n_blocks)
        slot3, src3 = _sorted_rows(dest, block_e, cnt, pst, nt, n_blocks)
        y_rows = _moe(block_e.reshape(-1), slot3, src3, h2, w_exp_gate[l].astype(BF16),
                      w_exp_up[l].astype(BF16), w_exp_down[l].astype(BF16))
        res = _combine(y_rows, x1, route, mod3, row(g_final), n_batch, final_norm=last)
        if last:
            out = res.reshape(n_batch, seq, d)
        else:
            streams = (res, res, True)
    return out
```

```python
import functools

import jax
import jax.numpy as jnp
from jax import lax
from jax.experimental import pallas as pl
from jax.experimental.pallas import tpu as pltpu

F32 = jnp.float32
BF16 = jnp.bfloat16
HIGHEST = lax.Precision.HIGHEST

A_CHUNK = 128
A_GROUPS = 8
GLA_HEADS = 4
GLA_TAU = 16.0
GLA_CHUNK = 64
N_EXPERTS = 16
N_EXPERT_GROUPS = 4
EXPERTS_PER_GROUP = N_EXPERTS // N_EXPERT_GROUPS
TOP_K = 2
EPS = 1e-6

LANES = 128
SUBLANES = 8
TM = 256
TB = 256
MOD_ROWS = 16
VMEM_LIMIT = 56 * 1024 * 1024


def _dot(a, b):
    return jnp.dot(a, b, preferred_element_type=F32)


def _const_spec(shape):
    nd = len(shape)
    return pl.BlockSpec(shape, lambda *_: (0,) * nd)


def _params(sem):
    return pltpu.CompilerParams(dimension_semantics=sem, vmem_limit_bytes=VMEM_LIMIT)


def _stream_specs(joined, nj, d):
    if joined:
        ctx_map = lambda b, j: (b * nj, 0)
        lat_map = lambda b, j: (b * nj + jnp.maximum(j, 1), 0)
    else:
        ctx_map = lambda b, j: (b, 0)
        lat_map = lambda b, j: (b * (nj - 1) + jnp.maximum(j - 1, 0), 0)
    return pl.BlockSpec((TM, d), ctx_map), pl.BlockSpec((TM, d), lat_map)


def _mod_kernel(cc_ref, w_ref, b_ref, o_ref):
    cc = cc_ref[...]
    s = cc * jax.nn.sigmoid(cc)
    o_ref[0] = jnp.dot(s, w_ref[0], preferred_element_type=F32, precision=HIGHEST) + b_ref[0]


def _modulation(cc, w_mod, b_mod):
    n_layer, d, six_d = w_mod.shape
    return pl.pallas_call(
        _mod_kernel,
        grid=(n_layer, six_d // d),
        in_specs=[
            pl.BlockSpec((MOD_ROWS, d), lambda l, j: (0, 0)),
            pl.BlockSpec((1, d, d), lambda l, j: (l, 0, j)),
            pl.BlockSpec((1, 1, d), lambda l, j: (l, 0, j)),
        ],
        out_specs=pl.BlockSpec((1, MOD_ROWS, d), lambda l, j: (l, 0, j)),
        out_shape=jax.ShapeDtypeStruct((n_layer, MOD_ROWS, six_d), F32),
        compiler_params=_params(("arbitrary", "arbitrary")),
        name="modulation",
    )(cc, w_mod, b_mod.reshape(n_layer, 1, six_d))


def _gelu_tanh(x):
    c = 0.7978845608028654
    return 0.5 * x * (1.0 + jnp.tanh(c * (x + 0.044715 * (x * x * x))))


def _log_sigmoid(z):
    return jnp.minimum(z, 0.0) - jnp.log1p(jnp.exp(-jnp.abs(z)))


def _inproj_kernel(xc_ref, xl_ref, mod_ref, g1_ref, wmain_ref, wlr_ref, wgates_ref, wg2_ref, bg2_ref,
                   lng_ref, lnb_ref, sguw_ref, sgub_ref, wa_ref, bbr_ref,
                   yag_ref, gb_ref, q_ref, k_ref, v_ref, sr_ref, la_ref, sa_ref):
    d = xc_ref.shape[1]
    dk = q_ref.shape[1]
    x = jnp.where(pl.program_id(1) == 0, xc_ref[...], xl_ref[...])
    sh1 = mod_ref[:, 0:d]
    sc1 = mod_ref[:, d:2 * d]
    h = x * lax.rsqrt(jnp.mean(x * x, axis=-1, keepdims=True) + EPS) * g1_ref[...]
    hb = (h * (1.0 + sc1) + sh1).astype(BF16)

    u = _gelu_tanh(_dot(hb, wmain_ref[:, 0:d]))
    vv = _gelu_tanh(_dot(hb, wmain_ref[:, d:2 * d]))
    mu = jnp.mean(vv, axis=-1, keepdims=True)
    vc = vv - mu
    vn = vc * lax.rsqrt(jnp.mean(vc * vc, axis=-1, keepdims=True) + EPS) * lng_ref[...] + lnb_ref[...]
    vnb = vn.astype(BF16)
    gdim = d // A_GROUPS
    for n in range(x.shape[0] // A_CHUNK):
        rs = slice(n * A_CHUNK, (n + 1) * A_CHUNK)
        for g in range(A_GROUPS):
            cs = slice(g * gdim, (g + 1) * gdim)
            mixed = _dot(sguw_ref[g], vnb[rs, cs]) + sgub_ref[:, cs]
            sa_ref[rs, cs] = (u[rs, cs] * mixed).astype(BF16)
    ya = _dot(sa_ref[...], wa_ref[...])
    gates = _dot(hb, wgates_ref[...]) + bbr_ref[...]
    yag_ref[...] = (jax.nn.sigmoid(gates[:, 0:d]) * ya).astype(BF16)
    gb_ref[...] = jax.nn.sigmoid(gates[:, d:2 * d]).astype(BF16)

    o = 2 * d
    head_k = dk // GLA_HEADS
    q_ref[...] = (_dot(hb, wmain_ref[:, o:o + dk]) * (head_k ** -0.5)).astype(BF16)
    k_ref[...] = _dot(hb, wmain_ref[:, o + dk:o + 2 * dk]).astype(BF16)
    o = o + 2 * dk
    v_ref[...] = _dot(hb, wmain_ref[:, o:o + d]).astype(BF16)
    r = _dot(hb, wmain_ref[:, o + d:o + 2 * d])
    sr_ref[...] = (r * jax.nn.sigmoid(r)).astype(BF16)
    lr = _dot(hb, wlr_ref[...]).astype(BF16)
    z = _dot(lr, wg2_ref[...]) + bg2_ref[...]
    la_ref[...] = _log_sigmoid(z) * (1.0 / GLA_TAU)


def _inproj(xc, xl, joined, nt, mod3, g1, wmain, wlr, wgates, wg2, bg2, lng, lnb, sguw, sgub, wa, bbr, n_batch):
    d = xc.shape[1]
    dk = wg2.shape[1] // 2
    nj = nt // n_batch // TM
    ctx_row = n_batch

    def row(b, j):
        return (b * nj + j, 0)

    def modrow(b, j):
        return (jnp.where(j == 0, ctx_row, b), 0, 0)

    tile = lambda w: pl.BlockSpec((TM, w), row)
    outs = [(d, BF16), (d, BF16), (dk, BF16), (dk, BF16), (d, BF16), (d, BF16), (2 * dk, F32)]
    return pl.pallas_call(
        _inproj_kernel,
        grid=(n_batch, nj),
        in_specs=[*_stream_specs(joined, nj, d), pl.BlockSpec((None, 1, mod3.shape[2]), modrow)]
        + [_const_spec(a.shape) for a in (g1, wmain, wlr, wgates, wg2, bg2, lng, lnb, sguw, sgub, wa, bbr)],
        out_specs=[tile(w) for w, _ in outs],
        out_shape=[jax.ShapeDtypeStruct((nt, w), t) for w, t in outs],
        scratch_shapes=[pltpu.VMEM((TM, d), BF16)],
        compiler_params=_params(("arbitrary", "arbitrary")),
        name="inproj",
    )(xc, xl, mod3, g1, wmain, wlr, wgates, wg2, bg2, lng, lnb, sguw, sgub, wa, bbr)


def _gla_tile(q_ref, k_ref, v_ref, la_ref, s_ref, o_ref, reverse):
    rows, dk = q_ref.shape
    dv = v_ref.shape[1]
    hk = dk // GLA_HEADS
    hv = dv // GLA_HEADS
    c = GLA_CHUNK
    ri = lax.broadcasted_iota(jnp.int32, (c, c), 0)
    ci = lax.broadcasted_iota(jnp.int32, (c, c), 1)
    tri = (ci >= ri) if reverse else (ci <= ri)
    rt = lax.broadcasted_iota(jnp.int32, (rows, rows), 0)
    ct = lax.broadcasted_iota(jnp.int32, (rows, rows), 1)
    in_chunk = (rt // c) == (ct // c)
    tri_t = jnp.where(jnp.logical_and(in_chunk, (ct >= rt) if reverse else (ct <= rt)), 1.0, 0.0).astype(BF16)
    la = la_ref[...]
    la_hi = la.astype(BF16)
    la_lo = (la - la_hi.astype(F32)).astype(BF16)
    b_all = _dot(tri_t, la_hi) + _dot(tri_t, la_lo)
    chunks = range(rows // c)
    for n in (reversed(chunks) if reverse else chunks):
        rs = slice(n * c, (n + 1) * c)
        b = b_all[rs, :]
        b_end = b[0:1, :] if reverse else b[c - 1:c, :]
        q = q_ref[rs, :].astype(F32)
        k = k_ref[rs, :].astype(F32)
        q_in = (q * jnp.exp(b)).astype(BF16)
        k_in = (k * jnp.exp(-b)).astype(BF16)
        k_st = (k * jnp.exp(b_end - b)).astype(BF16)
        decay = jnp.exp(b_end)
        for h in range(GLA_HEADS):
            ks = slice(h * hk, (h + 1) * hk)
            vs = slice(h * hv, (h + 1) * hv)
            vh = v_ref[rs, vs]
            att = lax.dot_general(q_in[:, ks], k_in[:, ks], (((1,), (1,)), ((), ())),
                                  preferred_element_type=F32)
            att = jnp.where(tri, att, 0.0).astype(BF16)
            state = s_ref[h]
            o_ref[rs, vs] = _dot(att, vh) + _dot(q_in[:, ks], state.astype(BF16))
            dcol = jnp.transpose(jnp.broadcast_to(decay[:, ks], (hk, hk)))
            dmat = jnp.concatenate([dcol] * (hv // hk), axis=1)
            kv = lax.dot_general(k_st[:, ks], vh, (((0,), (0,)), ((), ())), preferred_element_type=F32)
            s_ref[h] = dmat * state + kv


def _gla_bwd_kernel(q_ref, k_ref, v_ref, la_ref, ob_ref, s_ref, o_sc):
    @pl.when(pl.program_id(1) == 0)
    def _():
        s_ref[...] = jnp.zeros_like(s_ref)

    _gla_tile(q_ref, k_ref, v_ref, la_ref, s_ref, o_sc, reverse=True)
    ob_ref[...] = o_sc[...].astype(BF16)


def _gla_bwd(q, k, v, la, n_batch):
    nt, dk = q.shape
    dv = v.shape[1]
    nj = nt // n_batch // TM

    def row(b, jj):
        return (b * nj + jnp.where(jj == 0, 0, nj - jj), 0)

    def row_la(b, jj):
        return (b * nj + jnp.where(jj == 0, 0, nj - jj), 1)

    return pl.pallas_call(
        _gla_bwd_kernel,
        grid=(n_batch, nj),
        in_specs=[pl.BlockSpec((TM, dk), row), pl.BlockSpec((TM, dk), row), pl.BlockSpec((TM, dv), row),
                  pl.BlockSpec((TM, dk), row_la)],
        out_specs=pl.BlockSpec((TM, dv), row),
        out_shape=jax.ShapeDtypeStruct((nt, dv), BF16),
        scratch_shapes=[pltpu.VMEM((GLA_HEADS, dk // GLA_HEADS, dv // GLA_HEADS), F32),
                        pltpu.VMEM((TM, dv), F32)],
        compiler_params=_params(("arbitrary", "arbitrary")),
        name="gla_bwd",
    )(q, k, v, la)


def _route(logits_t):
    n_tok = logits_t.shape[1]
    eid = lax.broadcasted_iota(jnp.int32, logits_t.shape, 0)
    ex = jnp.exp(logits_t - jnp.max(logits_t, axis=0, keepdims=True))
    p = ex / jnp.sum(ex, axis=0, keepdims=True)
    grp = eid // EXPERTS_PER_GROUP
    none = -1.0
    far = 2 * N_EXPERTS
    best = None
    for g in range(N_EXPERT_GROUPS):
        pg = jnp.where(grp == g, p, none)
        m1 = jnp.max(pg, axis=0, keepdims=True)
        i1 = jnp.min(jnp.where(pg == m1, eid, far), axis=0, keepdims=True)
        pg2 = jnp.where(eid == i1, none, pg)
        m2 = jnp.max(pg2, axis=0, keepdims=True)
        i2 = jnp.min(jnp.where(pg2 == m2, eid, far), axis=0, keepdims=True)
        cand = (m1 + m2, m1, i1, m2, i2)
        if best is None:
            best = cand
        else:
            better = cand[0] > best[0]
            best = tuple(jnp.where(better, c, o) for c, o in zip(cand, best))
    _, m1, i1, m2, i2 = best
    tot = m1 + m2
    sub = lax.broadcasted_iota(jnp.int32, (SUBLANES, n_tok), 0)
    out = jnp.where(sub == 0, i1.astype(F32), 0.0)
    out = jnp.where(sub == 1, i2.astype(F32), out)
    out = jnp.where(sub == 2, m1 / tot, out)
    out = jnp.where(sub == 3, m2 / tot, out)
    return out


def _merge_kernel(xc_ref, xl_ref, mod_ref, q_ref, k_ref, v_ref, la_ref, ob_ref, yag_ref, gb_ref, sr_ref,
                  glag_ref, wb_ref, wout_ref, g2_ref, wr_ref, br_ref,
                  x1_ref, h2_ref, route_ref, routet_ref, s_ref, o_sc):
    @pl.when(pl.program_id(1) == 0)
    def _():
        s_ref[...] = jnp.zeros_like(s_ref)

    _gla_tile(q_ref, k_ref, v_ref, la_ref, s_ref, o_sc, reverse=False)

    d = xc_ref.shape[1]
    hv = d // GLA_HEADS
    o = o_sc[...] + ob_ref[...].astype(F32)
    parts = []
    for h in range(GLA_HEADS):
        oh = o[:, h * hv:(h + 1) * hv]
        parts.append(oh * lax.rsqrt(jnp.mean(oh * oh, axis=-1, keepdims=True) + EPS))
    on = jnp.concatenate(parts, axis=1) * glag_ref[...]
    yb = _dot((on * sr_ref[...].astype(F32)).astype(BF16), wb_ref[...])
    m = yag_ref[...].astype(F32) + gb_ref[...].astype(F32) * yb
    y = _dot(m.astype(BF16), wout_ref[...])
    gt1 = mod_ref[:, 2 * d:3 * d]
    sh2 = mod_ref[:, 3 * d:4 * d]
    sc2 = mod_ref[:, 4 * d:5 * d]
    x1 = jnp.where(pl.program_id(1) == 0, xc_ref[...], xl_ref[...]) + gt1 * y
    x1_ref[...] = x1
    h2 = x1 * lax.rsqrt(jnp.mean(x1 * x1, axis=-1, keepdims=True) + EPS) * g2_ref[...]
    h2 = h2 * (1.0 + sc2) + sh2
    h2_ref[...] = h2
    h_hi = h2.astype(BF16)
    h_lo = (h2 - h_hi.astype(F32)).astype(BF16)
    logits = _dot(h_hi, wr_ref[0]) + _dot(h_lo, wr_ref[0]) + _dot(h_hi, wr_ref[1]) + br_ref[...]
    rt = _route(jnp.transpose(logits)[0:N_EXPERTS, :])
    routet_ref[0] = rt
    sub = lax.broadcasted_iota(jnp.int32, (LANES, rt.shape[1]), 0)
    padded = jnp.zeros((LANES, rt.shape[1]), F32)
    for r in range(4):
        padded = jnp.where(sub == r, rt[r:r + 1, :], padded)
    route_ref[...] = jnp.transpose(padded)


def _merge(xc, xl, joined, nt, mod3, q, k, v, la, ob, yag, gb, sr, glag, wb, wout, g2, wr, br, n_batch):
    d = xc.shape[1]
    dk = q.shape[1]
    nj = nt // n_batch // TM
    ctx_row = n_batch

    def row(b, j):
        return (b * nj + j, 0)

    def modrow(b, j):
        return (jnp.where(j == 0, ctx_row, b), 0, 0)

    tile = lambda w: pl.BlockSpec((TM, w), row)
    return pl.pallas_call(
        _merge_kernel,
        grid=(n_batch, nj),
        in_specs=[*_stream_specs(joined, nj, d), pl.BlockSpec((None, 1, mod3.shape[2]), modrow),
                  tile(dk), tile(dk), tile(d), tile(dk), tile(d), tile(d), tile(d), tile(d)]
        + [_const_spec(a.shape) for a in (glag, wb, wout, g2, wr, br)],
        out_specs=[tile(d), tile(d), tile(LANES),
                   pl.BlockSpec((1, SUBLANES, TM), lambda b, j: (b * nj + j, 0, 0))],
        out_shape=[jax.ShapeDtypeStruct((nt, d), F32), jax.ShapeDtypeStruct((nt, d), F32),
                   jax.ShapeDtypeStruct((nt, LANES), F32),
                   jax.ShapeDtypeStruct((nt // TM, SUBLANES, TM), F32)],
        scratch_shapes=[pltpu.VMEM((GLA_HEADS, dk // GLA_HEADS, d // GLA_HEADS), F32),
                        pltpu.VMEM((TM, d), F32)],
        compiler_params=_params(("arbitrary", "arbitrary")),
        name="merge",
    )(xc, xl, mod3, q, k, v, la, ob, yag, gb, sr, glag, wb, wout, g2, wr, br)


def _plan_kernel(rt_ref, dest_ref, be_ref, cnt_ref, pst_ref):
    n_tiles, _, rows = rt_ref.shape
    sub = lax.broadcasted_iota(jnp.int32, (N_EXPERTS, rows), 0).astype(F32)
    wide = lambda col: jnp.broadcast_to(col, (N_EXPERTS, LANES))

    def one_hots(i):
        rt = rt_ref[i]
        return jnp.where(sub == rt[0:1, :], 1.0, 0.0), jnp.where(sub == rt[1:2, :], 1.0, 0.0)

    def count(i, acc):
        oh0, oh1 = one_hots(i)
        return acc + jnp.sum(oh0 + oh1, axis=1, keepdims=True)

    cnt = wide(lax.fori_loop(0, n_tiles, count, jnp.zeros((N_EXPERTS, 1), F32)))
    padded = jnp.floor((cnt + (TB - 1)) * (1.0 / TB)) * TB
    ri = lax.broadcasted_iota(jnp.int32, (N_EXPERTS, N_EXPERTS), 0)
    ci = lax.broadcasted_iota(jnp.int32, (N_EXPERTS, N_EXPERTS), 1)
    p_end = jnp.dot(jnp.where(ci <= ri, 1.0, 0.0), padded, preferred_element_type=F32, precision=HIGHEST)
    p_start = p_end - padded
    cnt_ref[...] = cnt
    pst_ref[...] = p_start
    starts = lax.broadcasted_iota(jnp.int32, (N_EXPERTS, be_ref.shape[1]), 1).astype(F32) * TB
    done = jnp.sum(jnp.where(p_end[:, 0:1] <= starts, 1.0, 0.0), axis=0, keepdims=True)
    be_ref[...] = jnp.minimum(done, N_EXPERTS - 1.0).astype(jnp.int32)

    rr = lax.broadcasted_iota(jnp.int32, (rows, rows), 0)
    cc = lax.broadcasted_iota(jnp.int32, (rows, rows), 1)
    earlier = jnp.where(rr < cc, 1.0, 0.0).astype(BF16)

    def place(i, run):
        oh0, oh1 = one_hots(i)
        c0 = _dot(oh0.astype(BF16), earlier)
        c1 = _dot(oh1.astype(BF16), earlier)
        tot0 = jnp.sum(oh0, axis=1, keepdims=True)
        base = p_start[:, 0:1] + run
        d0 = jnp.sum(oh0 * (c0 + base), axis=0, keepdims=True)
        d1 = jnp.sum(oh1 * (c1 + base + tot0), axis=0, keepdims=True)
        dest_ref[i] = jnp.concatenate([d0, d1], axis=0).astype(jnp.int32)
        return run + tot0 + jnp.sum(oh1, axis=1, keepdims=True)

    lax.fori_loop(0, n_tiles, place, jnp.zeros((N_EXPERTS, 1), F32))


def _plan(route_t, n_blocks):
    n_tiles = route_t.shape[0]
    nbp = -(-n_blocks // LANES) * LANES
    small = jax.ShapeDtypeStruct((N_EXPERTS, LANES), F32)
    return pl.pallas_call(
        _plan_kernel,
        out_shape=[jax.ShapeDtypeStruct((n_tiles, TOP_K, TM), jnp.int32),
                   jax.ShapeDtypeStruct((1, nbp), jnp.int32), small, small],
        compiler_params=pltpu.CompilerParams(vmem_limit_bytes=VMEM_LIMIT),
        name="plan",
    )(route_t)


def _sorted_rows(dest, block_e, cnt, pst, n_tok, n_blocks):
    cnt = cnt[:, 0].astype(jnp.int32)
    pst = pst[:, 0].astype(jnp.int32)
    pad = (cnt + TB - 1) // TB * TB - cnt
    pad_before = jnp.cumsum(pad) - pad
    be = block_e[0, :n_blocks]
    base = TOP_K * n_tok - pst[be] - cnt[be] + pad_before[be]
    init = base[:, None] + jnp.arange(n_blocks * TB, dtype=jnp.int32).reshape(n_blocks, TB)
    tok = jnp.arange(n_tok, dtype=jnp.int32).reshape(-1, 1, TM)
    slot_id = tok + n_tok * jnp.arange(TOP_K, dtype=jnp.int32)[None, :, None]
    slot = init.reshape(-1).at[dest.reshape(-1)].set(slot_id.reshape(-1), unique_indices=True)
    src = jnp.where(slot < TOP_K * n_tok, slot % n_tok, 0)
    return slot.reshape(n_blocks, 1, TB), src.reshape(n_blocks, 1, TB)


def _moe_kernel(be_ref, src_cur, src_nxt, src_nx2, slot_prv, slot_cur, h_hbm, wg_ref, wu_ref, wd_ref, y_hbm,
                x0, x1, x2, y0, y1, y2, gsem, ssem):
    del be_ref
    i = pl.program_id(0)
    last = pl.num_programs(0) - 1
    xs, ys = (x0, x1, x2), (y0, y1, y2)

    def gather_start(idx_ref, dst, sem):
        for r in range(TB):
            pltpu.make_async_copy(h_hbm.at[pl.ds(idx_ref[0, 0, r], 1)], dst.at[pl.ds(r, 1)], sem).start(
                priority=r % 2)

    def gather_wait(dst, sem):
        pltpu.make_async_copy(h_hbm.at[pl.ds(0, TB)], dst, sem).wait()

    def scatter_start(idx_ref, src, sem):
        for r in range(TB):
            pltpu.make_async_copy(src.at[pl.ds(r, 1)], y_hbm.at[pl.ds(idx_ref[0, 0, r], 1)], sem).start(
                priority=r % 2)

    def scatter_wait(src, sem):
        pltpu.make_async_copy(src, y_hbm.at[pl.ds(0, TB)], sem).wait()

    def experts(x_ref, y_ref):
        x = x_ref[...].astype(BF16)
        g = _dot(x, wg_ref[0])
        u = _dot(x, wu_ref[0])
        a = (g * jax.nn.sigmoid(g) * u).astype(BF16)
        y_ref[...] = _dot(a, wd_ref[0])

    for p in range(3):
        @pl.when(jnp.logical_and(i > 0, lax.rem(i, 3) == p))
        def _(p=p):
            prv, nx2 = (p + 2) % 3, (p + 2) % 3
            gather_wait(xs[p], gsem.at[p])

            @pl.when(i >= 3)
            def _():
                scatter_wait(ys[p], ssem.at[p])

            gather_start(src_nx2, xs[nx2], gsem.at[nx2])
            scatter_start(slot_prv, ys[prv], ssem.at[prv])
            experts(xs[p], ys[p])

    for p in range(3):
        @pl.when(jnp.logical_and(i == last, lax.rem(i, 3) == p))
        def _(p=p):
            nxt, prv = (p + 1) % 3, (p + 2) % 3
            scatter_start(slot_cur, ys[p], ssem.at[p])
            scatter_wait(ys[nxt], ssem.at[nxt])
            scatter_wait(ys[prv], ssem.at[prv])
            scatter_wait(ys[p], ssem.at[p])
            gather_wait(xs[nxt], gsem.at[nxt])
            gather_wait(xs[prv], gsem.at[prv])

    @pl.when(i == 0)
    def _():
        gather_start(src_cur, x0, gsem.at[0])
        gather_start(src_nxt, x1, gsem.at[1])
        gather_wait(x0, gsem.at[0])
        gather_start(src_nx2, x2, gsem.at[2])
        experts(x0, y0)


def _moe(block_e, slot3, src3, h2, wg, wu, wd):
    nb = slot3.shape[0]
    d = h2.shape[1]
    de = wg.shape[2]
    idx_spec = lambda f: pl.BlockSpec((1, 1, TB), f, memory_space=pltpu.SMEM)
    assert nb >= 3
    cur = lambda i, be: (i, 0, 0)
    nxt = lambda i, be: (jnp.minimum(i + 1, nb - 1), 0, 0)
    nx2 = lambda i, be: (jnp.minimum(i + 2, nb - 1), 0, 0)
    prv = lambda i, be: (jnp.maximum(i - 1, 0), 0, 0)
    vm = pltpu.VMEM((TB, d), F32)
    return pl.pallas_call(
        _moe_kernel,
        grid_spec=pltpu.PrefetchScalarGridSpec(
            num_scalar_prefetch=1,
            grid=(nb,),
            in_specs=[idx_spec(cur), idx_spec(nxt), idx_spec(nx2), idx_spec(prv), idx_spec(cur),
                      pl.BlockSpec(memory_space=pl.ANY),
                      pl.BlockSpec((1, d, de), lambda i, be: (be[i], 0, 0)),
                      pl.BlockSpec((1, d, de), lambda i, be: (be[i], 0, 0)),
                      pl.BlockSpec((1, de, d), lambda i, be: (be[i], 0, 0))],
            out_specs=pl.BlockSpec(memory_space=pl.ANY),
            scratch_shapes=[vm] * 6 + [pltpu.SemaphoreType.DMA((3,)), pltpu.SemaphoreType.DMA((3,))],
        ),
        out_shape=jax.ShapeDtypeStruct((nb * TB, d), F32),
        compiler_params=_params(("arbitrary",)),
        name="moe",
    )(block_e, src3, src3, src3, slot3, slot3, h2, wg, wu, wd)


def _combine_kernel(y0_ref, y1_ref, x1_ref, route_ref, mod_ref, gf_ref, o_ref, *, final_norm):
    d = x1_ref.shape[1]
    y = route_ref[:, 2:3] * y0_ref[...] + route_ref[:, 3:4] * y1_ref[...]
    x2 = x1_ref[...] + mod_ref[:, 5 * d:6 * d] * y
    if final_norm:
        x2 = x2 * lax.rsqrt(jnp.mean(x2 * x2, axis=-1, keepdims=True) + EPS) * gf_ref[...]
    o_ref[...] = x2


def _combine(y_rows, x1, route, mod3, gf, n_batch, final_norm):
    nt, d = x1.shape
    nj = nt // n_batch // TM
    ctx_row = n_batch

    def modrow(i):
        return (jnp.where(i % nj == 0, ctx_row, i // nj), 0, 0)

    if final_norm:
        out_rows = nt - n_batch * TM
        out_map = lambda i: ((i // nj) * (nj - 1) + jnp.maximum(i % nj - 1, 0), 0)
    else:
        out_rows = nt
        out_map = lambda i: (i, 0)
    return pl.pallas_call(
        functools.partial(_combine_kernel, final_norm=final_norm),
        grid=(nt // TM,),
        in_specs=[pl.BlockSpec((TM, d), lambda i: (i, 0)),
                  pl.BlockSpec((TM, d), lambda i: (nt // TM + i, 0)),
                  pl.BlockSpec((TM, d), lambda i: (i, 0)),
                  pl.BlockSpec((TM, LANES), lambda i: (i, 0)),
                  pl.BlockSpec((None, 1, mod3.shape[2]), modrow),
                  _const_spec(gf.shape)],
        out_specs=pl.BlockSpec((TM, d), out_map),
        out_shape=jax.ShapeDtypeStruct((out_rows, d), F32),
        compiler_params=_params(("arbitrary",)),
        name="combine",
    )(y_rows, y_rows, x1, route, mod3, gf)


def kernel(x, c, ctx, c_ctx, w_mod, b_mod, g_norm1, g_norm2, w_in, w_gate2, b_gate2, gla_norm_g,
           sgu_ln_g, sgu_ln_b, sgu_w, sgu_b, w_branch_a, w_branch_b, b_branch, w_out,
           w_router, b_router, w_exp_gate, w_exp_up, w_exp_down, g_final):
    n_batch, seq, d = x.shape
    ctx_len = ctx.shape[1]
    depth = w_mod.shape[0]
    dk = w_gate2.shape[3]
    rank_lr = w_gate2.shape[2]
    assert ctx_len == TM and seq % TM == 0 and n_batch < MOD_ROWS

    nt = n_batch * (ctx_len + seq)
    streams = (ctx.reshape(-1, d), x.reshape(-1, d), False)
    n_blocks = TOP_K * nt // TB + N_EXPERTS
    cc = jnp.zeros((MOD_ROWS, d), F32).at[:n_batch].set(c).at[n_batch].set(c_ctx)
    mod = _modulation(cc, w_mod, b_mod)

    wr = jnp.zeros((d, LANES), F32).at[:, :N_EXPERTS].set(w_router)
    wr_hi = wr.astype(BF16)
    wr = jnp.stack([wr_hi, (wr - wr_hi.astype(F32)).astype(BF16)])
    br = jnp.zeros((1, LANES), F32).at[0, :N_EXPERTS].set(b_router)
    row = lambda a: a.reshape(1, -1)
    n_main = 2 * d + 2 * dk + 2 * d

    out = None
    for l in range(depth):
        last = l == depth - 1
        mod3 = mod[l].reshape(MOD_ROWS, 1, 6 * d)
        wmain = w_in[l][:, :n_main].astype(BF16)
        wlr = w_in[l][:, n_main:n_main + 2 * rank_lr].astype(BF16)
        wgates = w_in[l][:, n_main + 2 * rank_lr:].astype(BF16)
        wg2 = jnp.zeros((2 * rank_lr, 2 * dk), F32)
        wg2 = wg2.at[:rank_lr, :dk].set(w_gate2[l, 0]).at[rank_lr:, dk:].set(w_gate2[l, 1]).astype(BF16)
        bg2 = b_gate2[l].reshape(1, 2 * dk)
        sgub = jnp.repeat(sgu_b[l].T, d // A_GROUPS, axis=1)

        yag, gb, q, k, v, sr, la = _inproj(
            *streams, nt, mod3, row(g_norm1[l]), wmain, wlr, wgates, wg2, bg2, row(sgu_ln_g[l]), row(sgu_ln_b[l]),
            sgu_w[l].astype(BF16), sgub, w_branch_a[l].astype(BF16), row(b_branch[l]), n_batch)
        ob = _gla_bwd(q, k, v, la, n_batch)
        x1, h2, route, route_t = _merge(
            *streams, nt, mod3, q, k, v, la, ob, yag, gb, sr, row(gla_norm_g[l]), w_branch_b[l].astype(BF16),
            w_out[l].astype(BF16), row(g_norm2[l]), wr, br, n_batch)
        dest, block_e, cnt, pst = _plan(route_t, n_blocks)
        slot3, src3 = _sorted_rows(dest, block_e, cnt, pst, nt, n_blocks)
        y_rows = _moe(block_e.reshape(-1), slot3, src3, h2, w_exp_gate[l].astype(BF16),
                      w_exp_up[l].astype(BF16), w_exp_down[l].astype(BF16))
        res = _combine(y_rows, x1, route, mod3, row(g_final), n_batch, final_norm=last)
        if last:
            out = res.reshape(n_batch, seq, d)
        else:
            streams = (res, res, True)
    return out
```

```python
import functools

import jax
import jax.numpy as jnp
from jax import lax
from jax.experimental import pallas as pl
from jax.experimental.pallas import tpu as pltpu

F32 = jnp.float32
BF16 = jnp.bfloat16
HIGHEST = lax.Precision.HIGHEST

A_CHUNK = 128
A_GROUPS = 8
GLA_HEADS = 4
GLA_TAU = 16.0
GLA_CHUNK = 64
N_EXPERTS = 16
N_EXPERT_GROUPS = 4
EXPERTS_PER_GROUP = N_EXPERTS // N_EXPERT_GROUPS
TOP_K = 2
EPS = 1e-6

LANES = 128
SUBLANES = 8
TM = 256
TB = 256
BG = 2
MOD_ROWS = 16
VMEM_LIMIT = 56 * 1024 * 1024


def _dot(a, b):
    return jnp.dot(a, b, preferred_element_type=F32)


def _const_spec(shape):
    nd = len(shape)
    return pl.BlockSpec(shape, lambda *_: (0,) * nd)


def _params(sem):
    return pltpu.CompilerParams(dimension_semantics=sem, vmem_limit_bytes=VMEM_LIMIT)


def _stream_specs(joined, nj, d):
    if joined:
        ctx_map = lambda b, j: (b * nj, 0)
        lat_map = lambda b, j: (b * nj + jnp.maximum(j, 1), 0)
    else:
        ctx_map = lambda b, j: (b, 0)
        lat_map = lambda b, j: (b * (nj - 1) + jnp.maximum(j - 1, 0), 0)
    return pl.BlockSpec((TM, d), ctx_map), pl.BlockSpec((TM, d), lat_map)


def _mod_kernel(cc_ref, w_ref, b_ref, o_ref):
    cc = cc_ref[...]
    s = cc * jax.nn.sigmoid(cc)
    o_ref[0] = jnp.dot(s, w_ref[0], preferred_element_type=F32, precision=HIGHEST) + b_ref[0]


def _modulation(cc, w_mod, b_mod):
    n_layer, d, six_d = w_mod.shape
    return pl.pallas_call(
        _mod_kernel,
        grid=(n_layer, six_d // d),
        in_specs=[
            pl.BlockSpec((MOD_ROWS, d), lambda l, j: (0, 0)),
            pl.BlockSpec((1, d, d), lambda l, j: (l, 0, j)),
            pl.BlockSpec((1, 1, d), lambda l, j: (l, 0, j)),
        ],
        out_specs=pl.BlockSpec((1, MOD_ROWS, d), lambda l, j: (l, 0, j)),
        out_shape=jax.ShapeDtypeStruct((n_layer, MOD_ROWS, six_d), F32),
        compiler_params=_params(("arbitrary", "arbitrary")),
        name="modulation",
    )(cc, w_mod, b_mod.reshape(n_layer, 1, six_d))


def _gelu_tanh(x):
    c = 0.7978845608028654
    return 0.5 * x * (1.0 + jnp.tanh(c * (x + 0.044715 * (x * x * x))))


def _log_sigmoid(z):
    return jnp.minimum(z, 0.0) - jnp.log1p(jnp.exp(-jnp.abs(z)))


def _inproj_kernel(xc_ref, xl_ref, mod_ref, g1_ref, wmain_ref, wlr_ref, wgates_ref, wg2_ref, bg2_ref,
                   lng_ref, lnb_ref, sguw_ref, sgub_ref, wa_ref, bbr_ref,
                   yag_ref, gb_ref, q_ref, k_ref, v_ref, sr_ref, la_ref, sa_ref):
    d = xc_ref.shape[1]
    dk = q_ref.shape[1]
    x = jnp.where(pl.program_id(1) == 0, xc_ref[...], xl_ref[...])
    sh1 = mod_ref[:, 0:d]
    sc1 = mod_ref[:, d:2 * d]
    h = x * lax.rsqrt(jnp.mean(x * x, axis=-1, keepdims=True) + EPS) * g1_ref[...]
    hb = (h * (1.0 + sc1) + sh1).astype(BF16)

    u = _gelu_tanh(_dot(hb, wmain_ref[:, 0:d]))
    vv = _gelu_tanh(_dot(hb, wmain_ref[:, d:2 * d]))
    mu = jnp.mean(vv, axis=-1, keepdims=True)
    vc = vv - mu
    vn = vc * lax.rsqrt(jnp.mean(vc * vc, axis=-1, keepdims=True) + EPS) * lng_ref[...] + lnb_ref[...]
    vnb = vn.astype(BF16)
    gdim = d // A_GROUPS
    for n in range(x.shape[0] // A_CHUNK):
        rs = slice(n * A_CHUNK, (n + 1) * A_CHUNK)
        for g in range(A_GROUPS):
            cs = slice(g * gdim, (g + 1) * gdim)
            mixed = _dot(sguw_ref[g], vnb[rs, cs]) + sgub_ref[:, cs]
            sa_ref[rs, cs] = (u[rs, cs] * mixed).astype(BF16)
    ya = _dot(sa_ref[...], wa_ref[...])
    gates = _dot(hb, wgates_ref[...]) + bbr_ref[...]
    yag_ref[...] = (jax.nn.sigmoid(gates[:, 0:d]) * ya).astype(BF16)
    gb_ref[...] = jax.nn.sigmoid(gates[:, d:2 * d]).astype(BF16)

    o = 2 * d
    head_k = dk // GLA_HEADS
    q_ref[...] = (_dot(hb, wmain_ref[:, o:o + dk]) * (head_k ** -0.5)).astype(BF16)
    k_ref[...] = _dot(hb, wmain_ref[:, o + dk:o + 2 * dk]).astype(BF16)
    o = o + 2 * dk
    v_ref[...] = _dot(hb, wmain_ref[:, o:o + d]).astype(BF16)
    r = _dot(hb, wmain_ref[:, o + d:o + 2 * d])
    sr_ref[...] = (r * jax.nn.sigmoid(r)).astype(BF16)
    lr = _dot(hb, wlr_ref[...]).astype(BF16)
    z = _dot(lr, wg2_ref[...]) + bg2_ref[...]
    la_ref[...] = _log_sigmoid(z) * (1.0 / GLA_TAU)


def _inproj(xc, xl, joined, nt, mod3, g1, wmain, wlr, wgates, wg2, bg2, lng, lnb, sguw, sgub, wa, bbr, n_batch):
    d = xc.shape[1]
    dk = wg2.shape[1] // 2
    nj = nt // n_batch // TM
    ctx_row = n_batch

    def row(b, j):
        return (b * nj + j, 0)

    def modrow(b, j):
        return (jnp.where(j == 0, ctx_row, b), 0, 0)

    tile = lambda w: pl.BlockSpec((TM, w), row)
    outs = [(d, BF16), (d, BF16), (dk, BF16), (dk, BF16), (d, BF16), (d, BF16), (2 * dk, F32)]
    return pl.pallas_call(
        _inproj_kernel,
        grid=(n_batch, nj),
        in_specs=[*_stream_specs(joined, nj, d), pl.BlockSpec((None, 1, mod3.shape[2]), modrow)]
        + [_const_spec(a.shape) for a in (g1, wmain, wlr, wgates, wg2, bg2, lng, lnb, sguw, sgub, wa, bbr)],
        out_specs=[tile(w) for w, _ in outs],
        out_shape=[jax.ShapeDtypeStruct((nt, w), t) for w, t in outs],
        scratch_shapes=[pltpu.VMEM((TM, d), BF16)],
        compiler_params=_params(("arbitrary", "arbitrary")),
        name="inproj",
    )(xc, xl, mod3, g1, wmain, wlr, wgates, wg2, bg2, lng, lnb, sguw, sgub, wa, bbr)


def _gla_tiles(q_refs, k_refs, v_refs, la_refs, s_refs, o_refs, qin_sc, kv_sc, dm_sc, reverse):
    n_g = len(q_refs)
    rows, dk = q_refs[0].shape
    dv = v_refs[0].shape[1]
    hk = dk // GLA_HEADS
    hv = dv // GLA_HEADS
    c = GLA_CHUNK
    ri = lax.broadcasted_iota(jnp.int32, (c, c), 0)
    ci = lax.broadcasted_iota(jnp.int32, (c, c), 1)
    tri = (ci >= ri) if reverse else (ci <= ri)
    rt = lax.broadcasted_iota(jnp.int32, (rows, rows), 0)
    ct = lax.broadcasted_iota(jnp.int32, (rows, rows), 1)
    in_chunk = (rt // c) == (ct // c)
    tri_t = jnp.where(jnp.logical_and(in_chunk, (ct >= rt) if reverse else (ct <= rt)), 1.0, 0.0).astype(BF16)
    chunks = list(range(rows // c))
    for g in range(n_g):
        la = la_refs[g][...]
        la_hi = la.astype(BF16)
        la_lo = (la - la_hi.astype(F32)).astype(BF16)
        b_all = _dot(tri_t, la_hi) + _dot(tri_t, la_lo)
        for n in chunks:
            rs = slice(n * c, (n + 1) * c)
            b = b_all[rs, :]
            b_end = b[0:1, :] if reverse else b[c - 1:c, :]
            q = q_refs[g][rs, :].astype(F32)
            k = k_refs[g][rs, :].astype(F32)
            q_in = (q * jnp.exp(b)).astype(BF16)
            k_in = (k * jnp.exp(-b)).astype(BF16)
            k_st = (k * jnp.exp(b_end - b)).astype(BF16)
            decay = jnp.exp(b_end)
            qin_sc[g, rs, :] = q_in
            for h in range(GLA_HEADS):
                ks = slice(h * hk, (h + 1) * hk)
                vs = slice(h * hv, (h + 1) * hv)
                vh = v_refs[g][rs, vs]
                att = lax.dot_general(q_in[:, ks], k_in[:, ks], (((1,), (1,)), ((), ())),
                                      preferred_element_type=F32)
                o_refs[g][rs, vs] = _dot(jnp.where(tri, att, 0.0).astype(BF16), vh)
                kv_sc[g, n, h] = lax.dot_general(k_st[:, ks], vh, (((0,), (0,)), ((), ())),
                                                 preferred_element_type=F32)
                dm_sc[g, n, h] = jnp.transpose(jnp.broadcast_to(decay[:, ks], (hk, hk)))
    for n in (reversed(chunks) if reverse else chunks):
        rs = slice(n * c, (n + 1) * c)
        for g in range(n_g):
            for h in range(GLA_HEADS):
                ks = slice(h * hk, (h + 1) * hk)
                vs = slice(h * hv, (h + 1) * hv)
                state = s_refs[g][h]
                o_refs[g][rs, vs] += _dot(qin_sc[g, rs, ks], state.astype(BF16))
                dmat = jnp.concatenate([dm_sc[g, n, h]] * (hv // hk), axis=1)
                s_refs[g][h] = dmat * state + kv_sc[g, n, h]


def _gla_scratch(n_g, rows, dk, dv):
    hk, hv, n_chunks = dk // GLA_HEADS, dv // GLA_HEADS, rows // GLA_CHUNK
    return [pltpu.VMEM((n_g, GLA_HEADS, hk, hv), F32),
            pltpu.VMEM((n_g, rows, dv), F32),
            pltpu.VMEM((n_g, rows, dk), BF16),
            pltpu.VMEM((n_g, n_chunks, GLA_HEADS, hk, hv), F32),
            pltpu.VMEM((n_g, n_chunks, GLA_HEADS, hk, hk), F32)]


def _per_batch(ref):
    return [ref.at[g] for g in range(ref.shape[0])]


def _gla_bwd_kernel(q_ref, k_ref, v_ref, la_ref, ob_ref, s_ref, o_sc, qin_sc, kv_sc, dm_sc):
    @pl.when(pl.program_id(1) == 0)
    def _():
        s_ref[...] = jnp.zeros_like(s_ref)

    _gla_tiles(_per_batch(q_ref), _per_batch(k_ref), _per_batch(v_ref), _per_batch(la_ref),
               _per_batch(s_ref), _per_batch(o_sc), qin_sc, kv_sc, dm_sc, reverse=True)
    ob_ref[...] = o_sc[...].astype(BF16)


def _gla_bwd(q, k, v, la, n_batch):
    nt, dk = q.shape
    dv = v.shape[1]
    t_all = nt // n_batch
    nj = t_all // TM
    per_batch = lambda a: a.reshape(n_batch, t_all, a.shape[1])

    def tile(bp, jj):
        return (bp, jnp.where(jj == 0, 0, nj - jj), 0)

    def tile_la(bp, jj):
        return (bp, jnp.where(jj == 0, 0, nj - jj), 1)

    ob = pl.pallas_call(
        _gla_bwd_kernel,
        grid=(n_batch // BG, nj),
        in_specs=[pl.BlockSpec((BG, TM, dk), tile), pl.BlockSpec((BG, TM, dk), tile),
                  pl.BlockSpec((BG, TM, dv), tile), pl.BlockSpec((BG, TM, dk), tile_la)],
        out_specs=pl.BlockSpec((BG, TM, dv), tile),
        out_shape=jax.ShapeDtypeStruct((n_batch, t_all, dv), BF16),
        scratch_shapes=_gla_scratch(BG, TM, dk, dv),
        compiler_params=_params(("arbitrary", "arbitrary")),
        name="gla_bwd",
    )(per_batch(q), per_batch(k), per_batch(v), per_batch(la))
    return ob.reshape(nt, dv)


def _route(logits_t):
    n_tok = logits_t.shape[1]
    eid = lax.broadcasted_iota(jnp.int32, logits_t.shape, 0)
    ex = jnp.exp(logits_t - jnp.max(logits_t, axis=0, keepdims=True))
    p = ex / jnp.sum(ex, axis=0, keepdims=True)
    grp = eid // EXPERTS_PER_GROUP
    none = -1.0
    far = 2 * N_EXPERTS
    best = None
    for g in range(N_EXPERT_GROUPS):
        pg = jnp.where(grp == g, p, none)
        m1 = jnp.max(pg, axis=0, keepdims=True)
        i1 = jnp.min(jnp.where(pg == m1, eid, far), axis=0, keepdims=True)
        pg2 = jnp.where(eid == i1, none, pg)
        m2 = jnp.max(pg2, axis=0, keepdims=True)
        i2 = jnp.min(jnp.where(pg2 == m2, eid, far), axis=0, keepdims=True)
        cand = (m1 + m2, m1, i1, m2, i2)
        if best is None:
            best = cand
        else:
            better = cand[0] > best[0]
            best = tuple(jnp.where(better, c, o) for c, o in zip(cand, best))
    _, m1, i1, m2, i2 = best
    tot = m1 + m2
    sub = lax.broadcasted_iota(jnp.int32, (SUBLANES, n_tok), 0)
    out = jnp.where(sub == 0, i1.astype(F32), 0.0)
    out = jnp.where(sub == 1, i2.astype(F32), out)
    out = jnp.where(sub == 2, m1 / tot, out)
    out = jnp.where(sub == 3, m2 / tot, out)
    return out


def _merge_kernel(xc_ref, xl_ref, modc_ref, modl_ref, q_ref, k_ref, v_ref, la_ref, ob_ref, yag_ref, gb_ref,
                  sr_ref, glag_ref, wb_ref, wout_ref, g2_ref, wr_ref, br_ref,
                  x1_ref, h2_ref, route_ref, routet_ref, s_ref, o_sc, qin_sc, kv_sc, dm_sc):
    first = pl.program_id(1) == 0

    @pl.when(first)
    def _():
        s_ref[...] = jnp.zeros_like(s_ref)

    _gla_tiles(_per_batch(q_ref), _per_batch(k_ref), _per_batch(v_ref), _per_batch(la_ref),
               _per_batch(s_ref), _per_batch(o_sc), qin_sc, kv_sc, dm_sc, reverse=False)

    d = xc_ref.shape[2]
    hv = d // GLA_HEADS
    for g in range(xc_ref.shape[0]):
        mod = jnp.where(first, modc_ref[0], modl_ref[g])
        o = o_sc[g] + ob_ref[g].astype(F32)
        parts = []
        for h in range(GLA_HEADS):
            oh = o[:, h * hv:(h + 1) * hv]
            parts.append(oh * lax.rsqrt(jnp.mean(oh * oh, axis=-1, keepdims=True) + EPS))
        on = jnp.concatenate(parts, axis=1) * glag_ref[...]
        yb = _dot((on * sr_ref[g].astype(F32)).astype(BF16), wb_ref[...])
        m = yag_ref[g].astype(F32) + gb_ref[g].astype(F32) * yb
        y = _dot(m.astype(BF16), wout_ref[...])
        gt1 = mod[:, 2 * d:3 * d]
        sh2 = mod[:, 3 * d:4 * d]
        sc2 = mod[:, 4 * d:5 * d]
        x1 = jnp.where(first, xc_ref[g], xl_ref[g]) + gt1 * y
        x1_ref[g] = x1
        h2 = x1 * lax.rsqrt(jnp.mean(x1 * x1, axis=-1, keepdims=True) + EPS) * g2_ref[...]
        h2 = h2 * (1.0 + sc2) + sh2
        h2_ref[g] = h2
        h_hi = h2.astype(BF16)
        h_lo = (h2 - h_hi.astype(F32)).astype(BF16)
        logits = _dot(h_hi, wr_ref[0]) + _dot(h_lo, wr_ref[0]) + _dot(h_hi, wr_ref[1]) + br_ref[...]
        rt = _route(jnp.transpose(logits)[0:N_EXPERTS, :])
        routet_ref[g, 0] = rt
        sub = lax.broadcasted_iota(jnp.int32, (LANES, rt.shape[1]), 0)
        padded = jnp.zeros((LANES, rt.shape[1]), F32)
        for r in range(4):
            padded = jnp.where(sub == r, rt[r:r + 1, :], padded)
        route_ref[g] = jnp.transpose(padded)


def _merge(xc, xl, joined, nt, mod3, q, k, v, la, ob, yag, gb, sr, glag, wb, wout, g2, wr, br, n_batch):
    d = xc.shape[2]
    dk = q.shape[1]
    t_all = nt // n_batch
    nj = t_all // TM
    ctx_row = n_batch
    per_batch = lambda a: a.reshape(n_batch, t_all, a.shape[1])
    tile = lambda w: pl.BlockSpec((BG, TM, w), lambda bp, j: (bp, j, 0))
    lat_map = (lambda bp, j: (bp, jnp.maximum(j, 1), 0)) if joined else (lambda bp, j: (bp, jnp.maximum(j - 1, 0), 0))
    six_d = mod3.shape[2]
    x1, h2, route, route_t = pl.pallas_call(
        _merge_kernel,
        grid=(n_batch // BG, nj),
        in_specs=[pl.BlockSpec((BG, TM, d), lambda bp, j: (bp, 0, 0)), pl.BlockSpec((BG, TM, d), lat_map),
                  pl.BlockSpec((1, 1, six_d), lambda bp, j: (ctx_row, 0, 0)),
                  pl.BlockSpec((BG, 1, six_d), lambda bp, j: (bp, 0, 0)),
                  tile(dk), tile(dk), tile(d), tile(dk), tile(d), tile(d), tile(d), tile(d)]
        + [_const_spec(a.shape) for a in (glag, wb, wout, g2, wr, br)],
        out_specs=[tile(d), tile(d), tile(LANES),
                   pl.BlockSpec((BG, 1, SUBLANES, TM), lambda bp, j: (bp, j, 0, 0))],
        out_shape=[jax.ShapeDtypeStruct((n_batch, t_all, d), F32), jax.ShapeDtypeStruct((n_batch, t_all, d), F32),
                   jax.ShapeDtypeStruct((n_batch, t_all, LANES), F32),
                   jax.ShapeDtypeStruct((n_batch, nj, SUBLANES, TM), F32)],
        scratch_shapes=_gla_scratch(BG, TM, dk, d),
        compiler_params=_params(("arbitrary", "arbitrary")),
        name="merge",
    )(xc, xl, mod3, mod3, per_batch(q), per_batch(k), per_batch(v), per_batch(la), per_batch(ob),
      per_batch(yag), per_batch(gb), per_batch(sr), glag, wb, wout, g2, wr, br)
    return (x1.reshape(nt, d), h2.reshape(nt, d), route.reshape(nt, LANES),
            route_t.reshape(nt // TM, SUBLANES, TM))


def _plan_kernel(rt_ref, dest_ref, be_ref, cnt_ref, pst_ref):
    n_tiles, _, rows = rt_ref.shape
    sub = lax.broadcasted_iota(jnp.int32, (N_EXPERTS, rows), 0).astype(F32)
    wide = lambda col: jnp.broadcast_to(col, (N_EXPERTS, LANES))

    def one_hots(i):
        rt = rt_ref[i]
        return jnp.where(sub == rt[0:1, :], 1.0, 0.0), jnp.where(sub == rt[1:2, :], 1.0, 0.0)

    def count(i, acc):
        oh0, oh1 = one_hots(i)
        return acc + jnp.sum(oh0 + oh1, axis=1, keepdims=True)

    cnt = wide(lax.fori_loop(0, n_tiles, count, jnp.zeros((N_EXPERTS, 1), F32)))
    padded = jnp.floor((cnt + (TB - 1)) * (1.0 / TB)) * TB
    ri = lax.broadcasted_iota(jnp.int32, (N_EXPERTS, N_EXPERTS), 0)
    ci = lax.broadcasted_iota(jnp.int32, (N_EXPERTS, N_EXPERTS), 1)
    p_end = jnp.dot(jnp.where(ci <= ri, 1.0, 0.0), padded, preferred_element_type=F32, precision=HIGHEST)
    p_start = p_end - padded
    cnt_ref[...] = cnt
    pst_ref[...] = p_start
    starts = lax.broadcasted_iota(jnp.int32, (N_EXPERTS, be_ref.shape[1]), 1).astype(F32) * TB
    done = jnp.sum(jnp.where(p_end[:, 0:1] <= starts, 1.0, 0.0), axis=0, keepdims=True)
    be_ref[...] = jnp.minimum(done, N_EXPERTS - 1.0).astype(jnp.int32)

    rr = lax.broadcasted_iota(jnp.int32, (rows, rows), 0)
    cc = lax.broadcasted_iota(jnp.int32, (rows, rows), 1)
    earlier = jnp.where(rr < cc, 1.0, 0.0).astype(BF16)

    def place(i, run):
        oh0, oh1 = one_hots(i)
        c0 = _dot(oh0.astype(BF16), earlier)
        c1 = _dot(oh1.astype(BF16), earlier)
        tot0 = jnp.sum(oh0, axis=1, keepdims=True)
        base = p_start[:, 0:1] + run
        d0 = jnp.sum(oh0 * (c0 + base), axis=0, keepdims=True)
        d1 = jnp.sum(oh1 * (c1 + base + tot0), axis=0, keepdims=True)
        dest_ref[i] = jnp.concatenate([d0, d1], axis=0).astype(jnp.int32)
        return run + tot0 + jnp.sum(oh1, axis=1, keepdims=True)

    lax.fori_loop(0, n_tiles, place, jnp.zeros((N_EXPERTS, 1), F32))


def _plan(route_t, n_blocks):
    n_tiles = route_t.shape[0]
    nbp = -(-n_blocks // LANES) * LANES
    small = jax.ShapeDtypeStruct((N_EXPERTS, LANES), F32)
    return pl.pallas_call(
        _plan_kernel,
        out_shape=[jax.ShapeDtypeStruct((n_tiles, TOP_K, TM), jnp.int32),
                   jax.ShapeDtypeStruct((1, nbp), jnp.int32), small, small],
        compiler_params=pltpu.CompilerParams(vmem_limit_bytes=VMEM_LIMIT),
        name="plan",
    )(route_t)


def _sorted_rows(dest, block_e, cnt, pst, n_tok, n_blocks):
    cnt = cnt[:, 0].astype(jnp.int32)
    pst = pst[:, 0].astype(jnp.int32)
    pad = (cnt + TB - 1) // TB * TB - cnt
    pad_before = jnp.cumsum(pad) - pad
    be = block_e[0, :n_blocks]
    base = TOP_K * n_tok - pst[be] - cnt[be] + pad_before[be]
    init = base[:, None] + jnp.arange(n_blocks * TB, dtype=jnp.int32).reshape(n_blocks, TB)
    tok = jnp.arange(n_tok, dtype=jnp.int32).reshape(-1, 1, TM)
    slot_id = tok + n_tok * jnp.arange(TOP_K, dtype=jnp.int32)[None, :, None]
    slot = init.reshape(-1).at[dest.reshape(-1)].set(slot_id.reshape(-1), unique_indices=True)
    src = jnp.where(slot < TOP_K * n_tok, slot % n_tok, 0)
    return slot.reshape(n_blocks, 1, TB), src.reshape(n_blocks, 1, TB)


def _moe_kernel(be_ref, src_cur, src_nxt, src_nx2, slot_prv, slot_cur, h_hbm, wg_ref, wu_ref, wd_ref, y_hbm,
                x0, x1, x2, y0, y1, y2, gsem, ssem):
    del be_ref
    i = pl.program_id(0)
    last = pl.num_programs(0) - 1
    xs, ys = (x0, x1, x2), (y0, y1, y2)

    def gather_start(idx_ref, dst, sem):
        for r in range(TB):
            pltpu.make_async_copy(h_hbm.at[pl.ds(idx_ref[0, 0, r], 1)], dst.at[pl.ds(r, 1)], sem).start(
                priority=r % 2)

    def gather_wait(dst, sem):
        pltpu.make_async_copy(h_hbm.at[pl.ds(0, TB)], dst, sem).wait()

    def scatter_start(idx_ref, src, sem):
        for r in range(TB):
            pltpu.make_async_copy(src.at[pl.ds(r, 1)], y_hbm.at[pl.ds(idx_ref[0, 0, r], 1)], sem).start(
                priority=r % 2)

    def scatter_wait(src, sem):
        pltpu.make_async_copy(src, y_hbm.at[pl.ds(0, TB)], sem).wait()

    def experts(x_ref, y_ref):
        x = x_ref[...].astype(BF16)
        g = _dot(x, wg_ref[0])
        u = _dot(x, wu_ref[0])
        a = (g * jax.nn.sigmoid(g) * u).astype(BF16)
        y_ref[...] = _dot(a, wd_ref[0])

    for p in range(3):
        @pl.when(jnp.logical_and(i > 0, lax.rem(i, 3) == p))
        def _(p=p):
            prv, nx2 = (p + 2) % 3, (p + 2) % 3
            gather_wait(xs[p], gsem.at[p])

            @pl.when(i >= 3)
            def _():
                scatter_wait(ys[p], ssem.at[p])

            gather_start(src_nx2, xs[nx2], gsem.at[nx2])
            scatter_start(slot_prv, ys[prv], ssem.at[prv])
            experts(xs[p], ys[p])

    for p in range(3):
        @pl.when(jnp.logical_and(i == last, lax.rem(i, 3) == p))
        def _(p=p):
            nxt, prv = (p + 1) % 3, (p + 2) % 3
            scatter_start(slot_cur, ys[p], ssem.at[p])
            scatter_wait(ys[nxt], ssem.at[nxt])
            scatter_wait(ys[prv], ssem.at[prv])
            scatter_wait(ys[p], ssem.at[p])
            gather_wait(xs[nxt], gsem.at[nxt])
            gather_wait(xs[prv], gsem.at[prv])

    @pl.when(i == 0)
    def _():
        gather_start(src_cur, x0, gsem.at[0])
        gather_start(src_nxt, x1, gsem.at[1])
        gather_wait(x0, gsem.at[0])
        gather_start(src_nx2, x2, gsem.at[2])
        experts(x0, y0)


def _moe(block_e, slot3, src3, h2, wg, wu, wd):
    nb = slot3.shape[0]
    d = h2.shape[1]
    de = wg.shape[2]
    idx_spec = lambda f: pl.BlockSpec((1, 1, TB), f, memory_space=pltpu.SMEM)
    assert nb >= 3
    cur = lambda i, be: (i, 0, 0)
    nxt = lambda i, be: (jnp.minimum(i + 1, nb - 1), 0, 0)
    nx2 = lambda i, be: (jnp.minimum(i + 2, nb - 1), 0, 0)
    prv = lambda i, be: (jnp.maximum(i - 1, 0), 0, 0)
    vm = pltpu.VMEM((TB, d), F32)
    return pl.pallas_call(
        _moe_kernel,
        grid_spec=pltpu.PrefetchScalarGridSpec(
            num_scalar_prefetch=1,
            grid=(nb,),
            in_specs=[idx_spec(cur), idx_spec(nxt), idx_spec(nx2), idx_spec(prv), idx_spec(cur),
                      pl.BlockSpec(memory_space=pl.ANY),
                      pl.BlockSpec((1, d, de), lambda i, be: (be[i], 0, 0)),
                      pl.BlockSpec((1, d, de), lambda i, be: (be[i], 0, 0)),
                      pl.BlockSpec((1, de, d), lambda i, be: (be[i], 0, 0))],
            out_specs=pl.BlockSpec(memory_space=pl.ANY),
            scratch_shapes=[vm] * 6 + [pltpu.SemaphoreType.DMA((3,)), pltpu.SemaphoreType.DMA((3,))],
        ),
        out_shape=jax.ShapeDtypeStruct((nb * TB, d), F32),
        compiler_params=_params(("arbitrary",)),
        name="moe",
    )(block_e, src3, src3, src3, slot3, slot3, h2, wg, wu, wd)


def _combine_kernel(y0_ref, y1_ref, x1_ref, route_ref, mod_ref, gf_ref, o_ref, *, final_norm):
    d = x1_ref.shape[1]
    y = route_ref[:, 2:3] * y0_ref[...] + route_ref[:, 3:4] * y1_ref[...]
    x2 = x1_ref[...] + mod_ref[:, 5 * d:6 * d] * y
    if final_norm:
        x2 = x2 * lax.rsqrt(jnp.mean(x2 * x2, axis=-1, keepdims=True) + EPS) * gf_ref[...]
    o_ref[...] = x2


def _combine(y_rows, x1, route, mod3, gf, n_batch, final_norm):
    nt, d = x1.shape
    nj = nt // n_batch // TM
    ctx_row = n_batch

    def modrow(i):
        return (jnp.where(i % nj == 0, ctx_row, i // nj), 0, 0)

    if final_norm:
        out_rows = nt - n_batch * TM
        out_map = lambda i: ((i // nj) * (nj - 1) + jnp.maximum(i % nj - 1, 0), 0)
    else:
        out_rows = nt
        out_map = lambda i: (i, 0)
    return pl.pallas_call(
        functools.partial(_combine_kernel, final_norm=final_norm),
        grid=(nt // TM,),
        in_specs=[pl.BlockSpec((TM, d), lambda i: (i, 0)),
                  pl.BlockSpec((TM, d), lambda i: (nt // TM + i, 0)),
                  pl.BlockSpec((TM, d), lambda i: (i, 0)),
                  pl.BlockSpec((TM, LANES), lambda i: (i, 0)),
                  pl.BlockSpec((None, 1, mod3.shape[2]), modrow),
                  _const_spec(gf.shape)],
        out_specs=pl.BlockSpec((TM, d), out_map),
        out_shape=jax.ShapeDtypeStruct((out_rows, d), F32),
        compiler_params=_params(("arbitrary",)),
        name="combine",
    )(y_rows, y_rows, x1, route, mod3, gf)


def kernel(x, c, ctx, c_ctx, w_mod, b_mod, g_norm1, g_norm2, w_in, w_gate2, b_gate2, gla_norm_g,
           sgu_ln_g, sgu_ln_b, sgu_w, sgu_b, w_branch_a, w_branch_b, b_branch, w_out,
           w_router, b_router, w_exp_gate, w_exp_up, w_exp_down, g_final):
    n_batch, seq, d = x.shape
    ctx_len = ctx.shape[1]
    depth = w_mod.shape[0]
    dk = w_gate2.shape[3]
    rank_lr = w_gate2.shape[2]
    assert ctx_len == TM and seq % TM == 0 and n_batch < MOD_ROWS

    nt = n_batch * (ctx_len + seq)
    streams = (ctx, x, False)
    n_blocks = TOP_K * nt // TB + N_EXPERTS
    cc = jnp.zeros((MOD_ROWS, d), F32).at[:n_batch].set(c).at[n_batch].set(c_ctx)
    mod = _modulation(cc, w_mod, b_mod)

    wr = jnp.zeros((d, LANES), F32).at[:, :N_EXPERTS].set(w_router)
    wr_hi = wr.astype(BF16)
    wr = jnp.stack([wr_hi, (wr - wr_hi.astype(F32)).astype(BF16)])
    br = jnp.zeros((1, LANES), F32).at[0, :N_EXPERTS].set(b_router)
    row = lambda a: a.reshape(1, -1)
    n_main = 2 * d + 2 * dk + 2 * d

    out = None
    for l in range(depth):
        last = l == depth - 1
        mod3 = mod[l].reshape(MOD_ROWS, 1, 6 * d)
        wmain = w_in[l][:, :n_main].astype(BF16)
        wlr = w_in[l][:, n_main:n_main + 2 * rank_lr].astype(BF16)
        wgates = w_in[l][:, n_main + 2 * rank_lr:].astype(BF16)
        wg2 = jnp.zeros((2 * rank_lr, 2 * dk), F32)
        wg2 = wg2.at[:rank_lr, :dk].set(w_gate2[l, 0]).at[rank_lr:, dk:].set(w_gate2[l, 1]).astype(BF16)
        bg2 = b_gate2[l].reshape(1, 2 * dk)
        sgub = jnp.repeat(sgu_b[l].T, d // A_GROUPS, axis=1)

        yag, gb, q, k, v, sr, la = _inproj(
            streams[0].reshape(-1, d), streams[1].reshape(-1, d), streams[2], nt, mod3, row(g_norm1[l]), wmain, wlr, wgates, wg2, bg2, row(sgu_ln_g[l]), row(sgu_ln_b[l]),
            sgu_w[l].astype(BF16), sgub, w_branch_a[l].astype(BF16), row(b_branch[l]), n_batch)
        ob = _gla_bwd(q, k, v, la, n_batch)
        x1, h2, route, route_t = _merge(
            *streams, nt, mod3, q, k, v, la, ob, yag, gb, sr, row(gla_norm_g[l]), w_branch_b[l].astype(BF16),
            w_out[l].astype(BF16), row(g_norm2[l]), wr, br, n_batch)
        dest, block_e, cnt, pst = _plan(route_t, n_blocks)
        slot3, src3 = _sorted_rows(dest, block_e, cnt, pst, nt, n_blocks)
        y_rows = _moe(block_e.reshape(-1), slot3, src3, h2, w_exp_gate[l].astype(BF16),
                      w_exp_up[l].astype(BF16), w_exp_down[l].astype(BF16))
        res = _combine(y_rows, x1, route, mod3, row(g_final), n_batch, final_norm=last)
        if last:
            out = res.reshape(n_batch, seq, d)
        else:
            res = res.reshape(n_batch, -1, d)
            streams = (res, res, True)
    return out
```

```python
import functools

import jax
import jax.numpy as jnp
from jax import lax
from jax.experimental import pallas as pl
from jax.experimental.pallas import tpu as pltpu

F32 = jnp.float32
BF16 = jnp.bfloat16
HIGHEST = lax.Precision.HIGHEST

A_CHUNK = 128
A_GROUPS = 8
GLA_HEADS = 4
GLA_TAU = 16.0
GLA_CHUNK = 64
N_EXPERTS = 16
N_EXPERT_GROUPS = 4
EXPERTS_PER_GROUP = N_EXPERTS // N_EXPERT_GROUPS
TOP_K = 2
EPS = 1e-6

LANES = 128
SUBLANES = 8
TM = 256
TB = 256
BG = 2
MOD_ROWS = 16
VMEM_LIMIT = 56 * 1024 * 1024


def _dot(a, b):
    return jnp.dot(a, b, preferred_element_type=F32)


def _const_spec(shape):
    nd = len(shape)
    return pl.BlockSpec(shape, lambda *_: (0,) * nd)


def _params(sem):
    return pltpu.CompilerParams(dimension_semantics=sem, vmem_limit_bytes=VMEM_LIMIT)


def _mod_kernel(cc_ref, w_ref, b_ref, o_ref):
    cc = cc_ref[...]
    s = cc * jax.nn.sigmoid(cc)
    o_ref[0] = jnp.dot(s, w_ref[0], preferred_element_type=F32, precision=HIGHEST) + b_ref[0]


def _modulation(cc, w_mod, b_mod):
    n_layer, d, six_d = w_mod.shape
    return pl.pallas_call(
        _mod_kernel,
        grid=(n_layer, six_d // d),
        in_specs=[
            pl.BlockSpec((MOD_ROWS, d), lambda l, j: (0, 0)),
            pl.BlockSpec((1, d, d), lambda l, j: (l, 0, j)),
            pl.BlockSpec((1, 1, d), lambda l, j: (l, 0, j)),
        ],
        out_specs=pl.BlockSpec((1, MOD_ROWS, d), lambda l, j: (l, 0, j)),
        out_shape=jax.ShapeDtypeStruct((n_layer, MOD_ROWS, six_d), F32),
        compiler_params=_params(("arbitrary", "arbitrary")),
        name="modulation",
    )(cc, w_mod, b_mod.reshape(n_layer, 1, six_d))


def _gelu_tanh(x):
    c = 0.7978845608028654
    return 0.5 * x * (1.0 + jnp.tanh(c * (x + 0.044715 * (x * x * x))))


def _log_sigmoid(z):
    return jnp.minimum(z, 0.0) - jnp.log1p(jnp.exp(-jnp.abs(z)))


def _inproj_kernel(xc_ref, xl_ref, modc_ref, modl_ref, g1_ref, wmain_ref, wlr_ref, wgates_ref, wg2_ref, bg2_ref,
                   lng_ref, lnb_ref, sguw_ref, sgub_ref, wa_ref, bbr_ref,
                   yag_ref, gb_ref, q_ref, k_ref, v_ref, sr_ref, la_ref, sa_ref):
    first = pl.program_id(1) == 0
    n_g, rows, d = xc_ref.shape
    dk = q_ref.shape[2]
    gdim = d // A_GROUPS
    head_k = dk // GLA_HEADS
    c_q, c_v = 2 * d, 2 * d + 2 * dk
    hb, u, vnb, ya = {}, {}, {}, {}

    def norm(g):
        mod = jnp.where(first, modc_ref[0], modl_ref[g])
        x = jnp.where(first, xc_ref[g], xl_ref[g])
        h = x * lax.rsqrt(jnp.mean(x * x, axis=-1, keepdims=True) + EPS) * g1_ref[...]
        hb[g] = (h * (1.0 + mod[:, d:2 * d]) + mod[:, 0:d]).astype(BF16)

    def gate_u(g):
        u[g] = _gelu_tanh(_dot(hb[g], wmain_ref[:, 0:d]))

    def gate_v(g):
        vv = _gelu_tanh(_dot(hb[g], wmain_ref[:, d:2 * d]))
        vc = vv - jnp.mean(vv, axis=-1, keepdims=True)
        vn = vc * lax.rsqrt(jnp.mean(vc * vc, axis=-1, keepdims=True) + EPS) * lng_ref[...] + lnb_ref[...]
        vnb[g] = vn.astype(BF16)

    def spatial(g):
        for n in range(rows // A_CHUNK):
            rs = slice(n * A_CHUNK, (n + 1) * A_CHUNK)
            for a in range(A_GROUPS):
                cs = slice(a * gdim, (a + 1) * gdim)
                mixed = _dot(sguw_ref[a], vnb[g][rs, cs]) + sgub_ref[:, cs]
                sa_ref[g, rs, cs] = (u[g][rs, cs] * mixed).astype(BF16)

    def proj_a(g):
        ya[g] = _dot(sa_ref[g], wa_ref[...])

    def branch_gates(g):
        gates = _dot(hb[g], wgates_ref[...]) + bbr_ref[...]
        yag_ref[g] = (jax.nn.sigmoid(gates[:, 0:d]) * ya[g]).astype(BF16)
        gb_ref[g] = jax.nn.sigmoid(gates[:, d:2 * d]).astype(BF16)

    def qk(g):
        q_ref[g] = (_dot(hb[g], wmain_ref[:, c_q:c_q + dk]) * (head_k ** -0.5)).astype(BF16)
        k_ref[g] = _dot(hb[g], wmain_ref[:, c_q + dk:c_q + 2 * dk]).astype(BF16)

    def val(g):
        v_ref[g] = _dot(hb[g], wmain_ref[:, c_v:c_v + d]).astype(BF16)

    def out_gate(g):
        r = _dot(hb[g], wmain_ref[:, c_v + d:c_v + 2 * d])
        sr_ref[g] = (r * jax.nn.sigmoid(r)).astype(BF16)

    def decay(g):
        lr = _dot(hb[g], wlr_ref[...]).astype(BF16)
        z = _dot(lr, wg2_ref[...]) + bg2_ref[...]
        la_ref[g] = _log_sigmoid(z) * (1.0 / GLA_TAU)

    order = [norm, gate_u, decay, gate_v, qk, spatial, out_gate, proj_a, branch_gates, val]
    for stage in order:
        for g in range(n_g):
            stage(g)


def _stream_specs(joined, ctx_row, six_d, d):
    lat_map = (lambda bp, j: (bp, jnp.maximum(j, 1), 0)) if joined else (lambda bp, j: (bp, jnp.maximum(j - 1, 0), 0))
    return [pl.BlockSpec((BG, TM, d), lambda bp, j: (bp, 0, 0)), pl.BlockSpec((BG, TM, d), lat_map),
            pl.BlockSpec((1, 1, six_d), lambda bp, j: (ctx_row, 0, 0)),
            pl.BlockSpec((BG, 1, six_d), lambda bp, j: (bp, 0, 0))]


def _resident_spec(shape):
    nd = len(shape)
    return pl.BlockSpec(shape, lambda *_: (0,) * nd, pipeline_mode=pl.Buffered(1))


def _inproj(xc, xl, joined, nt, mod3, g1, wmain, wlr, wgates, wg2, bg2, lng, lnb, sguw, sgub, wa, bbr, n_batch):
    d = xc.shape[2]
    dk = wg2.shape[1] // 2
    t_all = nt // n_batch
    nj = t_all // TM
    tile = lambda w: pl.BlockSpec((BG, TM, w), lambda bp, j: (bp, j, 0))
    outs = [(d, BF16), (d, BF16), (dk, BF16), (dk, BF16), (d, BF16), (d, BF16), (2 * dk, F32)]
    res = pl.pallas_call(
        _inproj_kernel,
        grid=(n_batch // BG, nj),
        in_specs=_stream_specs(joined, n_batch, mod3.shape[2], d)
        + [_resident_spec(a.shape) for a in (g1, wmain, wlr, wgates, wg2, bg2, lng, lnb, sguw, sgub, wa, bbr)],
        out_specs=[tile(w) for w, _ in outs],
        out_shape=[jax.ShapeDtypeStruct((n_batch, t_all, w), t) for w, t in outs],
        scratch_shapes=[pltpu.VMEM((BG, TM, d), BF16)],
        compiler_params=_params(("arbitrary", "arbitrary")),
        name="inproj",
    )(xc, xl, mod3, mod3, g1, wmain, wlr, wgates, wg2, bg2, lng, lnb, sguw, sgub, wa, bbr)
    return [a.reshape(nt, a.shape[2]) for a in res]


def _gla_tiles(q_refs, k_refs, v_refs, la_refs, s_refs, o_refs, qin_sc, kv_sc, dm_sc, reverse):
    n_g = len(q_refs)
    rows, dk = q_refs[0].shape
    dv = v_refs[0].shape[1]
    hk = dk // GLA_HEADS
    hv = dv // GLA_HEADS
    c = GLA_CHUNK
    ri = lax.broadcasted_iota(jnp.int32, (c, c), 0)
    ci = lax.broadcasted_iota(jnp.int32, (c, c), 1)
    tri = (ci >= ri) if reverse else (ci <= ri)
    rt = lax.broadcasted_iota(jnp.int32, (rows, rows), 0)
    ct = lax.broadcasted_iota(jnp.int32, (rows, rows), 1)
    in_chunk = (rt // c) == (ct // c)
    tri_t = jnp.where(jnp.logical_and(in_chunk, (ct >= rt) if reverse else (ct <= rt)), 1.0, 0.0).astype(BF16)
    chunks = list(range(rows // c))
    for g in range(n_g):
        la = la_refs[g][...]
        la_hi = la.astype(BF16)
        la_lo = (la - la_hi.astype(F32)).astype(BF16)
        b_all = _dot(tri_t, la_hi) + _dot(tri_t, la_lo)
        for n in chunks:
            rs = slice(n * c, (n + 1) * c)
            b = b_all[rs, :]
            b_end = b[0:1, :] if reverse else b[c - 1:c, :]
            q = q_refs[g][rs, :].astype(F32)
            k = k_refs[g][rs, :].astype(F32)
            q_in = (q * jnp.exp(b)).astype(BF16)
            k_in = (k * jnp.exp(-b)).astype(BF16)
            k_st = (k * jnp.exp(b_end - b)).astype(BF16)
            decay = jnp.exp(b_end)
            qin_sc[g, rs, :] = q_in
            for h in range(GLA_HEADS):
                ks = slice(h * hk, (h + 1) * hk)
                vs = slice(h * hv, (h + 1) * hv)
                vh = v_refs[g][rs, vs]
                att = lax.dot_general(q_in[:, ks], k_in[:, ks], (((1,), (1,)), ((), ())),
                                      preferred_element_type=F32)
                o_refs[g][rs, vs] = _dot(jnp.where(tri, att, 0.0).astype(BF16), vh)
                kv_sc[g, n, h] = lax.dot_general(k_st[:, ks], vh, (((0,), (0,)), ((), ())),
                                                 preferred_element_type=F32)
                dm_sc[g, n, h] = jnp.transpose(jnp.broadcast_to(decay[:, ks], (hk, hk)))
    for n in (reversed(chunks) if reverse else chunks):
        rs = slice(n * c, (n + 1) * c)
        for g in range(n_g):
            for h in range(GLA_HEADS):
                ks = slice(h * hk, (h + 1) * hk)
                vs = slice(h * hv, (h + 1) * hv)
                state = s_refs[g][h]
                o_refs[g][rs, vs] += _dot(qin_sc[g, rs, ks], state.astype(BF16))
                dmat = jnp.concatenate([dm_sc[g, n, h]] * (hv // hk), axis=1)
                s_refs[g][h] = dmat * state + kv_sc[g, n, h]


def _gla_scratch(n_g, rows, dk, dv):
    hk, hv, n_chunks = dk // GLA_HEADS, dv // GLA_HEADS, rows // GLA_CHUNK
    return [pltpu.VMEM((n_g, GLA_HEADS, hk, hv), F32),
            pltpu.VMEM((n_g, rows, dv), F32),
            pltpu.VMEM((n_g, rows, dk), BF16),
            pltpu.VMEM((n_g, n_chunks, GLA_HEADS, hk, hv), F32),
            pltpu.VMEM((n_g, n_chunks, GLA_HEADS, hk, hk), F32)]


def _per_batch(ref):
    return [ref.at[g] for g in range(ref.shape[0])]


def _gla_bwd_kernel(q_ref, k_ref, v_ref, la_ref, ob_ref, s_ref, o_sc, qin_sc, kv_sc, dm_sc):
    @pl.when(pl.program_id(1) == 0)
    def _():
        s_ref[...] = jnp.zeros_like(s_ref)

    _gla_tiles(_per_batch(q_ref), _per_batch(k_ref), _per_batch(v_ref), _per_batch(la_ref),
               _per_batch(s_ref), _per_batch(o_sc), qin_sc, kv_sc, dm_sc, reverse=True)
    ob_ref[...] = o_sc[...].astype(BF16)


def _gla_bwd(q, k, v, la, n_batch):
    nt, dk = q.shape
    dv = v.shape[1]
    t_all = nt // n_batch
    nj = t_all // TM
    per_batch = lambda a: a.reshape(n_batch, t_all, a.shape[1])

    def tile(bp, jj):
        return (bp, jnp.where(jj == 0, 0, nj - jj), 0)

    def tile_la(bp, jj):
        return (bp, jnp.where(jj == 0, 0, nj - jj), 1)

    ob = pl.pallas_call(
        _gla_bwd_kernel,
        grid=(n_batch // BG, nj),
        in_specs=[pl.BlockSpec((BG, TM, dk), tile), pl.BlockSpec((BG, TM, dk), tile),
                  pl.BlockSpec((BG, TM, dv), tile), pl.BlockSpec((BG, TM, dk), tile_la)],
        out_specs=pl.BlockSpec((BG, TM, dv), tile),
        out_shape=jax.ShapeDtypeStruct((n_batch, t_all, dv), BF16),
        scratch_shapes=_gla_scratch(BG, TM, dk, dv),
        compiler_params=_params(("arbitrary", "arbitrary")),
        name="gla_bwd",
    )(per_batch(q), per_batch(k), per_batch(v), per_batch(la))
    return ob.reshape(nt, dv)


def _route(logits_t):
    n_tok = logits_t.shape[1]
    eid = lax.broadcasted_iota(jnp.int32, logits_t.shape, 0)
    ex = jnp.exp(logits_t - jnp.max(logits_t, axis=0, keepdims=True))
    p = ex / jnp.sum(ex, axis=0, keepdims=True)
    grp = eid // EXPERTS_PER_GROUP
    none = -1.0
    far = 2 * N_EXPERTS
    best = None
    for g in range(N_EXPERT_GROUPS):
        pg = jnp.where(grp == g, p, none)
        m1 = jnp.max(pg, axis=0, keepdims=True)
        i1 = jnp.min(jnp.where(pg == m1, eid, far), axis=0, keepdims=True)
        pg2 = jnp.where(eid == i1, none, pg)
        m2 = jnp.max(pg2, axis=0, keepdims=True)
        i2 = jnp.min(jnp.where(pg2 == m2, eid, far), axis=0, keepdims=True)
        cand = (m1 + m2, m1, i1, m2, i2)
        if best is None:
            best = cand
        else:
            better = cand[0] > best[0]
            best = tuple(jnp.where(better, c, o) for c, o in zip(cand, best))
    _, m1, i1, m2, i2 = best
    tot = m1 + m2
    sub = lax.broadcasted_iota(jnp.int32, (SUBLANES, n_tok), 0)
    out = jnp.where(sub == 0, i1.astype(F32), 0.0)
    out = jnp.where(sub == 1, i2.astype(F32), out)
    out = jnp.where(sub == 2, m1 / tot, out)
    out = jnp.where(sub == 3, m2 / tot, out)
    return out


def _merge_kernel(xc_ref, xl_ref, modc_ref, modl_ref, q_ref, k_ref, v_ref, la_ref, ob_ref, yag_ref, gb_ref,
                  sr_ref, glag_ref, wb_ref, wout_ref, g2_ref, wr_ref, br_ref,
                  x1_ref, h2_ref, route_ref, routet_ref, s_ref, o_sc, qin_sc, kv_sc, dm_sc):
    first = pl.program_id(1) == 0

    @pl.when(first)
    def _():
        s_ref[...] = jnp.zeros_like(s_ref)

    _gla_tiles(_per_batch(q_ref), _per_batch(k_ref), _per_batch(v_ref), _per_batch(la_ref),
               _per_batch(s_ref), _per_batch(o_sc), qin_sc, kv_sc, dm_sc, reverse=False)

    d = xc_ref.shape[2]
    hv = d // GLA_HEADS
    for g in range(xc_ref.shape[0]):
        mod = jnp.where(first, modc_ref[0], modl_ref[g])
        o = o_sc[g] + ob_ref[g].astype(F32)
        parts = []
        for h in range(GLA_HEADS):
            oh = o[:, h * hv:(h + 1) * hv]
            parts.append(oh * lax.rsqrt(jnp.mean(oh * oh, axis=-1, keepdims=True) + EPS))
        on = jnp.concatenate(parts, axis=1) * glag_ref[...]
        yb = _dot((on * sr_ref[g].astype(F32)).astype(BF16), wb_ref[...])
        m = yag_ref[g].astype(F32) + gb_ref[g].astype(F32) * yb
        y = _dot(m.astype(BF16), wout_ref[...])
        gt1 = mod[:, 2 * d:3 * d]
        sh2 = mod[:, 3 * d:4 * d]
        sc2 = mod[:, 4 * d:5 * d]
        x1 = jnp.where(first, xc_ref[g], xl_ref[g]) + gt1 * y
        x1_ref[g] = x1
        h2 = x1 * lax.rsqrt(jnp.mean(x1 * x1, axis=-1, keepdims=True) + EPS) * g2_ref[...]
        h2 = h2 * (1.0 + sc2) + sh2
        h2_ref[g] = h2
        h_hi = h2.astype(BF16)
        h_lo = (h2 - h_hi.astype(F32)).astype(BF16)
        logits = _dot(h_hi, wr_ref[0]) + _dot(h_lo, wr_ref[0]) + _dot(h_hi, wr_ref[1]) + br_ref[...]
        rt = _route(jnp.transpose(logits)[0:N_EXPERTS, :])
        routet_ref[g, 0] = rt
        sub = lax.broadcasted_iota(jnp.int32, (LANES, rt.shape[1]), 0)
        padded = jnp.zeros((LANES, rt.shape[1]), F32)
        for r in range(4):
            padded = jnp.where(sub == r, rt[r:r + 1, :], padded)
        route_ref[g] = jnp.transpose(padded)


def _merge(xc, xl, joined, nt, mod3, q, k, v, la, ob, yag, gb, sr, glag, wb, wout, g2, wr, br, n_batch):
    d = xc.shape[2]
    dk = q.shape[1]
    t_all = nt // n_batch
    nj = t_all // TM
    per_batch = lambda a: a.reshape(n_batch, t_all, a.shape[1])
    tile = lambda w: pl.BlockSpec((BG, TM, w), lambda bp, j: (bp, j, 0))
    x1, h2, route, route_t = pl.pallas_call(
        _merge_kernel,
        grid=(n_batch // BG, nj),
        in_specs=_stream_specs(joined, n_batch, mod3.shape[2], d)
        + [tile(dk), tile(dk), tile(d), tile(dk), tile(d), tile(d), tile(d), tile(d)]
        + [_const_spec(a.shape) for a in (glag, wb, wout, g2, wr, br)],
        out_specs=[tile(d), tile(d), tile(LANES),
                   pl.BlockSpec((BG, 1, SUBLANES, TM), lambda bp, j: (bp, j, 0, 0))],
        out_shape=[jax.ShapeDtypeStruct((n_batch, t_all, d), F32), jax.ShapeDtypeStruct((n_batch, t_all, d), F32),
                   jax.ShapeDtypeStruct((n_batch, t_all, LANES), F32),
                   jax.ShapeDtypeStruct((n_batch, nj, SUBLANES, TM), F32)],
        scratch_shapes=_gla_scratch(BG, TM, dk, d),
        compiler_params=_params(("arbitrary", "arbitrary")),
        name="merge",
    )(xc, xl, mod3, mod3, per_batch(q), per_batch(k), per_batch(v), per_batch(la), per_batch(ob),
      per_batch(yag), per_batch(gb), per_batch(sr), glag, wb, wout, g2, wr, br)
    return (x1.reshape(nt, d), h2.reshape(nt, d), route.reshape(nt, LANES),
            route_t.reshape(nt // TM, SUBLANES, TM))


def _plan_kernel(rt_ref, dest_ref, be_ref, cnt_ref, pst_ref):
    n_tiles, _, rows = rt_ref.shape
    sub = lax.broadcasted_iota(jnp.int32, (N_EXPERTS, rows), 0).astype(F32)
    wide = lambda col: jnp.broadcast_to(col, (N_EXPERTS, LANES))

    def one_hots(i):
        rt = rt_ref[i]
        return jnp.where(sub == rt[0:1, :], 1.0, 0.0), jnp.where(sub == rt[1:2, :], 1.0, 0.0)

    def count(i, acc):
        oh0, oh1 = one_hots(i)
        return acc + jnp.sum(oh0 + oh1, axis=1, keepdims=True)

    cnt = wide(lax.fori_loop(0, n_tiles, count, jnp.zeros((N_EXPERTS, 1), F32)))
    padded = jnp.floor((cnt + (TB - 1)) * (1.0 / TB)) * TB
    ri = lax.broadcasted_iota(jnp.int32, (N_EXPERTS, N_EXPERTS), 0)
    ci = lax.broadcasted_iota(jnp.int32, (N_EXPERTS, N_EXPERTS), 1)
    p_end = jnp.dot(jnp.where(ci <= ri, 1.0, 0.0), padded, preferred_element_type=F32, precision=HIGHEST)
    p_start = p_end - padded
    cnt_ref[...] = cnt
    pst_ref[...] = p_start
    starts = lax.broadcasted_iota(jnp.int32, (N_EXPERTS, be_ref.shape[1]), 1).astype(F32) * TB
    done = jnp.sum(jnp.where(p_end[:, 0:1] <= starts, 1.0, 0.0), axis=0, keepdims=True)
    be_ref[...] = jnp.minimum(done, N_EXPERTS - 1.0).astype(jnp.int32)

    rr = lax.broadcasted_iota(jnp.int32, (rows, rows), 0)
    cc = lax.broadcasted_iota(jnp.int32, (rows, rows), 1)
    earlier = jnp.where(rr < cc, 1.0, 0.0).astype(BF16)

    def place(i, run):
        oh0, oh1 = one_hots(i)
        c0 = _dot(oh0.astype(BF16), earlier)
        c1 = _dot(oh1.astype(BF16), earlier)
        tot0 = jnp.sum(oh0, axis=1, keepdims=True)
        base = p_start[:, 0:1] + run
        d0 = jnp.sum(oh0 * (c0 + base), axis=0, keepdims=True)
        d1 = jnp.sum(oh1 * (c1 + base + tot0), axis=0, keepdims=True)
        dest_ref[i] = jnp.concatenate([d0, d1], axis=0).astype(jnp.int32)
        return run + tot0 + jnp.sum(oh1, axis=1, keepdims=True)

    lax.fori_loop(0, n_tiles, place, jnp.zeros((N_EXPERTS, 1), F32))


def _plan(route_t, n_blocks):
    n_tiles = route_t.shape[0]
    nbp = -(-n_blocks // LANES) * LANES
    small = jax.ShapeDtypeStruct((N_EXPERTS, LANES), F32)
    return pl.pallas_call(
        _plan_kernel,
        out_shape=[jax.ShapeDtypeStruct((n_tiles, TOP_K, TM), jnp.int32),
                   jax.ShapeDtypeStruct((1, nbp), jnp.int32), small, small],
        compiler_params=pltpu.CompilerParams(vmem_limit_bytes=VMEM_LIMIT),
        name="plan",
    )(route_t)


def _sorted_rows(dest, block_e, cnt, pst, n_tok, n_blocks):
    cnt = cnt[:, 0].astype(jnp.int32)
    pst = pst[:, 0].astype(jnp.int32)
    pad = (cnt + TB - 1) // TB * TB - cnt
    pad_before = jnp.cumsum(pad) - pad
    be = block_e[0, :n_blocks]
    base = TOP_K * n_tok - pst[be] - cnt[be] + pad_before[be]
    init = base[:, None] + jnp.arange(n_blocks * TB, dtype=jnp.int32).reshape(n_blocks, TB)
    tok = jnp.arange(n_tok, dtype=jnp.int32).reshape(-1, 1, TM)
    slot_id = tok + n_tok * jnp.arange(TOP_K, dtype=jnp.int32)[None, :, None]
    slot = init.reshape(-1).at[dest.reshape(-1)].set(slot_id.reshape(-1), unique_indices=True)
    src = jnp.where(slot < TOP_K * n_tok, slot % n_tok, 0)
    return slot.reshape(n_blocks, 1, TB), src.reshape(n_blocks, 1, TB)


def _moe_kernel(be_ref, src_cur, src_nxt, src_nx2, slot_prv, slot_cur, h_hbm, wg_ref, wu_ref, wd_ref, y_hbm,
                x0, x1, x2, y0, y1, y2, gsem, ssem):
    del be_ref
    i = pl.program_id(0)
    last = pl.num_programs(0) - 1
    xs, ys = (x0, x1, x2), (y0, y1, y2)

    def gather_start(idx_ref, dst, sem):
        for r in range(TB):
            pltpu.make_async_copy(h_hbm.at[pl.ds(idx_ref[0, 0, r], 1)], dst.at[pl.ds(r, 1)], sem).start(
                priority=r % 2)

    def gather_wait(dst, sem):
        pltpu.make_async_copy(h_hbm.at[pl.ds(0, TB)], dst, sem).wait()

    def scatter_start(idx_ref, src, sem):
        for r in range(TB):
            pltpu.make_async_copy(src.at[pl.ds(r, 1)], y_hbm.at[pl.ds(idx_ref[0, 0, r], 1)], sem).start(
                priority=r % 2)

    def scatter_wait(src, sem):
        pltpu.make_async_copy(src, y_hbm.at[pl.ds(0, TB)], sem).wait()

    def experts(x_ref, y_ref):
        x = x_ref[...].astype(BF16)
        g = _dot(x, wg_ref[0])
        u = _dot(x, wu_ref[0])
        a = (g * jax.nn.sigmoid(g) * u).astype(BF16)
        y_ref[...] = _dot(a, wd_ref[0])

    for p in range(3):
        @pl.when(jnp.logical_and(i > 0, lax.rem(i, 3) == p))
        def _(p=p):
            prv, nx2 = (p + 2) % 3, (p + 2) % 3
            gather_wait(xs[p], gsem.at[p])

            @pl.when(i >= 3)
            def _():
                scatter_wait(ys[p], ssem.at[p])

            gather_start(src_nx2, xs[nx2], gsem.at[nx2])
            scatter_start(slot_prv, ys[prv], ssem.at[prv])
            experts(xs[p], ys[p])

    for p in range(3):
        @pl.when(jnp.logical_and(i == last, lax.rem(i, 3) == p))
        def _(p=p):
            nxt, prv = (p + 1) % 3, (p + 2) % 3
            scatter_start(slot_cur, ys[p], ssem.at[p])
            scatter_wait(ys[nxt], ssem.at[nxt])
            scatter_wait(ys[prv], ssem.at[prv])
            scatter_wait(ys[p], ssem.at[p])
            gather_wait(xs[nxt], gsem.at[nxt])
            gather_wait(xs[prv], gsem.at[prv])

    @pl.when(i == 0)
    def _():
        gather_start(src_cur, x0, gsem.at[0])
        gather_start(src_nxt, x1, gsem.at[1])
        gather_wait(x0, gsem.at[0])
        gather_start(src_nx2, x2, gsem.at[2])
        experts(x0, y0)


def _moe(block_e, slot3, src3, h2, wg, wu, wd):
    nb = slot3.shape[0]
    d = h2.shape[1]
    de = wg.shape[2]
    idx_spec = lambda f: pl.BlockSpec((1, 1, TB), f, memory_space=pltpu.SMEM)
    assert nb >= 3
    cur = lambda i, be: (i, 0, 0)
    nxt = lambda i, be: (jnp.minimum(i + 1, nb - 1), 0, 0)
    nx2 = lambda i, be: (jnp.minimum(i + 2, nb - 1), 0, 0)
    prv = lambda i, be: (jnp.maximum(i - 1, 0), 0, 0)
    vm = pltpu.VMEM((TB, d), F32)
    return pl.pallas_call(
        _moe_kernel,
        grid_spec=pltpu.PrefetchScalarGridSpec(
            num_scalar_prefetch=1,
            grid=(nb,),
            in_specs=[idx_spec(cur), idx_spec(nxt), idx_spec(nx2), idx_spec(prv), idx_spec(cur),
                      pl.BlockSpec(memory_space=pl.ANY),
                      pl.BlockSpec((1, d, de), lambda i, be: (be[i], 0, 0)),
                      pl.BlockSpec((1, d, de), lambda i, be: (be[i], 0, 0)),
                      pl.BlockSpec((1, de, d), lambda i, be: (be[i], 0, 0))],
            out_specs=pl.BlockSpec(memory_space=pl.ANY),
            scratch_shapes=[vm] * 6 + [pltpu.SemaphoreType.DMA((3,)), pltpu.SemaphoreType.DMA((3,))],
        ),
        out_shape=jax.ShapeDtypeStruct((nb * TB, d), F32),
        compiler_params=_params(("arbitrary",)),
        name="moe",
    )(block_e, src3, src3, src3, slot3, slot3, h2, wg, wu, wd)


def _combine_kernel(y0_ref, y1_ref, x1_ref, route_ref, mod_ref, gf_ref, o_ref, *, final_norm):
    d = x1_ref.shape[1]
    y = route_ref[:, 2:3] * y0_ref[...] + route_ref[:, 3:4] * y1_ref[...]
    x2 = x1_ref[...] + mod_ref[:, 5 * d:6 * d] * y
    if final_norm:
        x2 = x2 * lax.rsqrt(jnp.mean(x2 * x2, axis=-1, keepdims=True) + EPS) * gf_ref[...]
    o_ref[...] = x2


def _combine(y_rows, x1, route, mod3, gf, n_batch, final_norm):
    nt, d = x1.shape
    nj = nt // n_batch // TM
    ctx_row = n_batch

    def modrow(i):
        return (jnp.where(i % nj == 0, ctx_row, i // nj), 0, 0)

    if final_norm:
        out_rows = nt - n_batch * TM
        out_map = lambda i: ((i // nj) * (nj - 1) + jnp.maximum(i % nj - 1, 0), 0)
    else:
        out_rows = nt
        out_map = lambda i: (i, 0)
    return pl.pallas_call(
        functools.partial(_combine_kernel, final_norm=final_norm),
        grid=(nt // TM,),
        in_specs=[pl.BlockSpec((TM, d), lambda i: (i, 0)),
                  pl.BlockSpec((TM, d), lambda i: (nt // TM + i, 0)),
                  pl.BlockSpec((TM, d), lambda i: (i, 0)),
                  pl.BlockSpec((TM, LANES), lambda i: (i, 0)),
                  pl.BlockSpec((None, 1, mod3.shape[2]), modrow),
                  _const_spec(gf.shape)],
        out_specs=pl.BlockSpec((TM, d), out_map),
        out_shape=jax.ShapeDtypeStruct((out_rows, d), F32),
        compiler_params=_params(("arbitrary",)),
        name="combine",
    )(y_rows, y_rows, x1, route, mod3, gf)


def kernel(x, c, ctx, c_ctx, w_mod, b_mod, g_norm1, g_norm2, w_in, w_gate2, b_gate2, gla_norm_g,
           sgu_ln_g, sgu_ln_b, sgu_w, sgu_b, w_branch_a, w_branch_b, b_branch, w_out,
           w_router, b_router, w_exp_gate, w_exp_up, w_exp_down, g_final):
    n_batch, seq, d = x.shape
    ctx_len = ctx.shape[1]
    depth = w_mod.shape[0]
    dk = w_gate2.shape[3]
    rank_lr = w_gate2.shape[2]
    assert ctx_len == TM and seq % TM == 0 and n_batch < MOD_ROWS

    nt = n_batch * (ctx_len + seq)
    streams = (ctx, x, False)
    n_blocks = TOP_K * nt // TB + N_EXPERTS
    cc = jnp.zeros((MOD_ROWS, d), F32).at[:n_batch].set(c).at[n_batch].set(c_ctx)
    mod = _modulation(cc, w_mod, b_mod)

    wr = jnp.zeros((d, LANES), F32).at[:, :N_EXPERTS].set(w_router)
    wr_hi = wr.astype(BF16)
    wr = jnp.stack([wr_hi, (wr - wr_hi.astype(F32)).astype(BF16)])
    br = jnp.zeros((1, LANES), F32).at[0, :N_EXPERTS].set(b_router)
    row = lambda a: a.reshape(1, -1)
    n_main = 2 * d + 2 * dk + 2 * d

    out = None
    for l in range(depth):
        last = l == depth - 1
        mod3 = mod[l].reshape(MOD_ROWS, 1, 6 * d)
        wmain = w_in[l][:, :n_main].astype(BF16)
        wlr = w_in[l][:, n_main:n_main + 2 * rank_lr].astype(BF16)
        wgates = w_in[l][:, n_main + 2 * rank_lr:].astype(BF16)
        wg2 = jnp.zeros((2 * rank_lr, 2 * dk), F32)
        wg2 = wg2.at[:rank_lr, :dk].set(w_gate2[l, 0]).at[rank_lr:, dk:].set(w_gate2[l, 1]).astype(BF16)
        bg2 = b_gate2[l].reshape(1, 2 * dk)
        sgub = jnp.repeat(sgu_b[l].T, d // A_GROUPS, axis=1)

        yag, gb, q, k, v, sr, la = _inproj(
            *streams, nt, mod3, row(g_norm1[l]), wmain, wlr, wgates, wg2, bg2, row(sgu_ln_g[l]), row(sgu_ln_b[l]),
            sgu_w[l].astype(BF16), sgub, w_branch_a[l].astype(BF16), row(b_branch[l]), n_batch)
        ob = _gla_bwd(q, k, v, la, n_batch)
        x1, h2, route, route_t = _merge(
            *streams, nt, mod3, q, k, v, la, ob, yag, gb, sr, row(gla_norm_g[l]), w_branch_b[l].astype(BF16),
            w_out[l].astype(BF16), row(g_norm2[l]), wr, br, n_batch)
        dest, block_e, cnt, pst = _plan(route_t, n_blocks)
        slot3, src3 = _sorted_rows(dest, block_e, cnt, pst, nt, n_blocks)
        y_rows = _moe(block_e.reshape(-1), slot3, src3, h2, w_exp_gate[l].astype(BF16),
                      w_exp_up[l].astype(BF16), w_exp_down[l].astype(BF16))
        res = _combine(y_rows, x1, route, mod3, row(g_final), n_batch, final_norm=last)
        if last:
            out = res.reshape(n_batch, seq, d)
        else:
            res = res.reshape(n_batch, -1, d)
            streams = (res, res, True)
    return out
```

```python
import functools

import jax
import jax.numpy as jnp
from jax import lax
from jax.experimental import pallas as pl
from jax.experimental.pallas import tpu as pltpu

F32 = jnp.float32
BF16 = jnp.bfloat16
HIGHEST = lax.Precision.HIGHEST

A_CHUNK = 128
A_GROUPS = 8
GLA_HEADS = 4
GLA_TAU = 16.0
GLA_CHUNK = 64
N_EXPERTS = 16
N_EXPERT_GROUPS = 4
EXPERTS_PER_GROUP = N_EXPERTS // N_EXPERT_GROUPS
TOP_K = 2
EPS = 1e-6

LANES = 128
SUBLANES = 8
TM = 256
TB = 256
BG = 2
MOD_ROWS = 16
VMEM_LIMIT = 56 * 1024 * 1024


def _dot(a, b):
    return jnp.dot(a, b, preferred_element_type=F32)


def _pack_halves(x):
    n = x.shape[1] // 2
    bits = lambda v: pltpu.bitcast(v.astype(BF16).astype(F32), jnp.uint32)
    return bits(x[:, :n]) | (bits(x[:, n:]) >> 16)


def _unpack_halves(p):
    hi = pltpu.bitcast(p & jnp.uint32(0xFFFF0000), F32)
    lo = pltpu.bitcast(p << 16, F32)
    return jnp.concatenate([hi, lo], axis=1)


def _const_spec(shape):
    nd = len(shape)
    return pl.BlockSpec(shape, lambda *_: (0,) * nd)


def _params(sem):
    return pltpu.CompilerParams(dimension_semantics=sem, vmem_limit_bytes=VMEM_LIMIT)


def _mod_kernel(cc_ref, w_ref, b_ref, o_ref):
    cc = cc_ref[...]
    s = cc * jax.nn.sigmoid(cc)
    o_ref[0] = jnp.dot(s, w_ref[0], preferred_element_type=F32, precision=HIGHEST) + b_ref[0]


def _modulation(cc, w_mod, b_mod):
    n_layer, d, six_d = w_mod.shape
    return pl.pallas_call(
        _mod_kernel,
        grid=(n_layer, six_d // d),
        in_specs=[
            pl.BlockSpec((MOD_ROWS, d), lambda l, j: (0, 0)),
            pl.BlockSpec((1, d, d), lambda l, j: (l, 0, j)),
            pl.BlockSpec((1, 1, d), lambda l, j: (l, 0, j)),
        ],
        out_specs=pl.BlockSpec((1, MOD_ROWS, d), lambda l, j: (l, 0, j)),
        out_shape=jax.ShapeDtypeStruct((n_layer, MOD_ROWS, six_d), F32),
        compiler_params=_params(("arbitrary", "arbitrary")),
        name="modulation",
    )(cc, w_mod, b_mod.reshape(n_layer, 1, six_d))


def _gelu_tanh(x):
    c = 0.7978845608028654
    return 0.5 * x * (1.0 + jnp.tanh(c * (x + 0.044715 * (x * x * x))))


def _log_sigmoid(z):
    return jnp.minimum(z, 0.0) - jnp.log1p(jnp.exp(-jnp.abs(z)))


def _inproj_kernel(xc_ref, xl_ref, modc_ref, modl_ref, g1_ref, wmain_ref, wlr_ref, wgates_ref, wg2_ref, bg2_ref,
                   lng_ref, lnb_ref, sguw_ref, sgub_ref, wa_ref, bbr_ref,
                   yag_ref, gb_ref, q_ref, k_ref, v_ref, sr_ref, la_ref, sa_ref):
    first = pl.program_id(1) == 0
    n_g, rows, d = xc_ref.shape
    dk = q_ref.shape[2]
    gdim = d // A_GROUPS
    head_k = dk // GLA_HEADS
    c_q, c_v = 2 * d, 2 * d + 2 * dk
    hb, u, vnb, ya = {}, {}, {}, {}

    def norm(g):
        mod = jnp.where(first, modc_ref[0], modl_ref[g])
        x = jnp.where(first, xc_ref[g], xl_ref[g])
        h = x * lax.rsqrt(jnp.mean(x * x, axis=-1, keepdims=True) + EPS) * g1_ref[...]
        hb[g] = (h * (1.0 + mod[:, d:2 * d]) + mod[:, 0:d]).astype(BF16)

    def gate_u(g):
        u[g] = _gelu_tanh(_dot(hb[g], wmain_ref[:, 0:d]))

    def gate_v(g):
        vv = _gelu_tanh(_dot(hb[g], wmain_ref[:, d:2 * d]))
        vc = vv - jnp.mean(vv, axis=-1, keepdims=True)
        vn = vc * lax.rsqrt(jnp.mean(vc * vc, axis=-1, keepdims=True) + EPS) * lng_ref[...] + lnb_ref[...]
        vnb[g] = vn.astype(BF16)

    def spatial(g):
        for n in range(rows // A_CHUNK):
            rs = slice(n * A_CHUNK, (n + 1) * A_CHUNK)
            for a in range(A_GROUPS):
                cs = slice(a * gdim, (a + 1) * gdim)
                mixed = _dot(sguw_ref[a], vnb[g][rs, cs]) + sgub_ref[:, cs]
                sa_ref[g, rs, cs] = (u[g][rs, cs] * mixed).astype(BF16)

    def proj_a(g):
        ya[g] = _dot(sa_ref[g], wa_ref[...])

    def branch_gates(g):
        gates = _dot(hb[g], wgates_ref[...]) + bbr_ref[...]
        yag_ref[g] = (jax.nn.sigmoid(gates[:, 0:d]) * ya[g]).astype(BF16)
        gb_ref[g] = jax.nn.sigmoid(gates[:, d:2 * d]).astype(BF16)

    def qk(g):
        q_ref[g] = (_dot(hb[g], wmain_ref[:, c_q:c_q + dk]) * (head_k ** -0.5)).astype(BF16)
        k_ref[g] = _dot(hb[g], wmain_ref[:, c_q + dk:c_q + 2 * dk]).astype(BF16)

    def val(g):
        v_ref[g] = _dot(hb[g], wmain_ref[:, c_v:c_v + d]).astype(BF16)

    def out_gate(g):
        r = _dot(hb[g], wmain_ref[:, c_v + d:c_v + 2 * d])
        sr_ref[g] = (r * jax.nn.sigmoid(r)).astype(BF16)

    def decay(g):
        lr = _dot(hb[g], wlr_ref[...]).astype(BF16)
        z = _dot(lr, wg2_ref[...]) + bg2_ref[...]
        la_ref[g] = _log_sigmoid(z) * (1.0 / GLA_TAU)

    order = [norm, gate_u, decay, gate_v, qk, spatial, out_gate, proj_a, branch_gates, val]
    for stage in order:
        for g in range(n_g):
            stage(g)


def _stream_specs(joined, ctx_row, six_d, d):
    lat_map = (lambda bp, j: (bp, jnp.maximum(j, 1), 0)) if joined else (lambda bp, j: (bp, jnp.maximum(j - 1, 0), 0))
    return [pl.BlockSpec((BG, TM, d), lambda bp, j: (bp, 0, 0)), pl.BlockSpec((BG, TM, d), lat_map),
            pl.BlockSpec((1, 1, six_d), lambda bp, j: (ctx_row, 0, 0)),
            pl.BlockSpec((BG, 1, six_d), lambda bp, j: (bp, 0, 0))]


def _resident_spec(shape):
    nd = len(shape)
    return pl.BlockSpec(shape, lambda *_: (0,) * nd, pipeline_mode=pl.Buffered(1))


def _inproj(xc, xl, joined, nt, mod3, g1, wmain, wlr, wgates, wg2, bg2, lng, lnb, sguw, sgub, wa, bbr, n_batch):
    d = xc.shape[2]
    dk = wg2.shape[1] // 2
    t_all = nt // n_batch
    nj = t_all // TM
    tile = lambda w: pl.BlockSpec((BG, TM, w), lambda bp, j: (bp, j, 0))
    outs = [(d, BF16), (d, BF16), (dk, BF16), (dk, BF16), (d, BF16), (d, BF16), (2 * dk, F32)]
    res = pl.pallas_call(
        _inproj_kernel,
        grid=(n_batch // BG, nj),
        in_specs=_stream_specs(joined, n_batch, mod3.shape[2], d)
        + [_resident_spec(a.shape) for a in (g1, wmain, wlr, wgates, wg2, bg2, lng, lnb, sguw, sgub, wa, bbr)],
        out_specs=[tile(w) for w, _ in outs],
        out_shape=[jax.ShapeDtypeStruct((n_batch, t_all, w), t) for w, t in outs],
        scratch_shapes=[pltpu.VMEM((BG, TM, d), BF16)],
        compiler_params=_params(("arbitrary", "arbitrary")),
        name="inproj",
    )(xc, xl, mod3, mod3, g1, wmain, wlr, wgates, wg2, bg2, lng, lnb, sguw, sgub, wa, bbr)
    return [a.reshape(nt, a.shape[2]) for a in res]


def _gla_tiles(q_refs, k_refs, v_refs, la_refs, s_refs, o_refs, qin_sc, kv_sc, dm_sc, reverse):
    n_g = len(q_refs)
    rows, dk = q_refs[0].shape
    dv = v_refs[0].shape[1]
    hk = dk // GLA_HEADS
    hv = dv // GLA_HEADS
    c = GLA_CHUNK
    ri = lax.broadcasted_iota(jnp.int32, (c, c), 0)
    ci = lax.broadcasted_iota(jnp.int32, (c, c), 1)
    tri = (ci >= ri) if reverse else (ci <= ri)
    rt = lax.broadcasted_iota(jnp.int32, (rows, rows), 0)
    ct = lax.broadcasted_iota(jnp.int32, (rows, rows), 1)
    in_chunk = (rt // c) == (ct // c)
    tri_t = jnp.where(jnp.logical_and(in_chunk, (ct >= rt) if reverse else (ct <= rt)), 1.0, 0.0).astype(BF16)
    chunks = list(range(rows // c))
    for g in range(n_g):
        la = la_refs[g][...]
        la_hi = la.astype(BF16)
        la_lo = (la - la_hi.astype(F32)).astype(BF16)
        b_all = _dot(tri_t, la_hi) + _dot(tri_t, la_lo)
        for n in chunks:
            rs = slice(n * c, (n + 1) * c)
            b = b_all[rs, :]
            b_end = b[0:1, :] if reverse else b[c - 1:c, :]
            q = q_refs[g][rs, :].astype(F32)
            k = k_refs[g][rs, :].astype(F32)
            q_in = (q * jnp.exp(b)).astype(BF16)
            k_in = (k * jnp.exp(-b)).astype(BF16)
            k_st = (k * jnp.exp(b_end - b)).astype(BF16)
            decay = jnp.exp(b_end)
            qin_sc[g, rs, :] = q_in
            for h in range(GLA_HEADS):
                ks = slice(h * hk, (h + 1) * hk)
                vs = slice(h * hv, (h + 1) * hv)
                vh = v_refs[g][rs, vs]
                att = lax.dot_general(q_in[:, ks], k_in[:, ks], (((1,), (1,)), ((), ())),
                                      preferred_element_type=F32)
                o_refs[g][rs, vs] = _dot(jnp.where(tri, att, 0.0).astype(BF16), vh)
                kv_sc[g, n, h] = lax.dot_general(k_st[:, ks], vh, (((0,), (0,)), ((), ())),
                                                 preferred_element_type=F32)
                dm_sc[g, n, h] = jnp.transpose(jnp.broadcast_to(decay[:, ks], (hk, hk)))
    for n in (reversed(chunks) if reverse else chunks):
        rs = slice(n * c, (n + 1) * c)
        for g in range(n_g):
            for h in range(GLA_HEADS):
                ks = slice(h * hk, (h + 1) * hk)
                vs = slice(h * hv, (h + 1) * hv)
                state = s_refs[g][h]
                o_refs[g][rs, vs] += _dot(qin_sc[g, rs, ks], state.astype(BF16))
                dmat = jnp.concatenate([dm_sc[g, n, h]] * (hv // hk), axis=1)
                s_refs[g][h] = dmat * state + kv_sc[g, n, h]


def _gla_scratch(n_g, rows, dk, dv):
    hk, hv, n_chunks = dk // GLA_HEADS, dv // GLA_HEADS, rows // GLA_CHUNK
    return [pltpu.VMEM((n_g, GLA_HEADS, hk, hv), F32),
            pltpu.VMEM((n_g, rows, dv), F32),
            pltpu.VMEM((n_g, rows, dk), BF16),
            pltpu.VMEM((n_g, n_chunks, GLA_HEADS, hk, hv), F32),
            pltpu.VMEM((n_g, n_chunks, GLA_HEADS, hk, hk), F32)]


def _per_batch(ref):
    return [ref.at[g] for g in range(ref.shape[0])]


def _gla_bwd_kernel(q_ref, k_ref, v_ref, la_ref, ob_ref, s_ref, o_sc, qin_sc, kv_sc, dm_sc):
    @pl.when(pl.program_id(1) == 0)
    def _():
        s_ref[...] = jnp.zeros_like(s_ref)

    _gla_tiles(_per_batch(q_ref), _per_batch(k_ref), _per_batch(v_ref), _per_batch(la_ref),
               _per_batch(s_ref), _per_batch(o_sc), qin_sc, kv_sc, dm_sc, reverse=True)
    ob_ref[...] = o_sc[...].astype(BF16)


def _gla_bwd(q, k, v, la, n_batch):
    nt, dk = q.shape
    dv = v.shape[1]
    t_all = nt // n_batch
    nj = t_all // TM
    per_batch = lambda a: a.reshape(n_batch, t_all, a.shape[1])

    def tile(bp, jj):
        return (bp, jnp.where(jj == 0, 0, nj - jj), 0)

    def tile_la(bp, jj):
        return (bp, jnp.where(jj == 0, 0, nj - jj), 1)

    ob = pl.pallas_call(
        _gla_bwd_kernel,
        grid=(n_batch // BG, nj),
        in_specs=[pl.BlockSpec((BG, TM, dk), tile), pl.BlockSpec((BG, TM, dk), tile),
                  pl.BlockSpec((BG, TM, dv), tile), pl.BlockSpec((BG, TM, dk), tile_la)],
        out_specs=pl.BlockSpec((BG, TM, dv), tile),
        out_shape=jax.ShapeDtypeStruct((n_batch, t_all, dv), BF16),
        scratch_shapes=_gla_scratch(BG, TM, dk, dv),
        compiler_params=_params(("arbitrary", "arbitrary")),
        name="gla_bwd",
    )(per_batch(q), per_batch(k), per_batch(v), per_batch(la))
    return ob.reshape(nt, dv)


def _route(logits_t):
    n_tok = logits_t.shape[1]
    eid = lax.broadcasted_iota(jnp.int32, logits_t.shape, 0)
    ex = jnp.exp(logits_t - jnp.max(logits_t, axis=0, keepdims=True))
    p = ex / jnp.sum(ex, axis=0, keepdims=True)
    grp = eid // EXPERTS_PER_GROUP
    none = -1.0
    far = 2 * N_EXPERTS
    best = None
    for g in range(N_EXPERT_GROUPS):
        pg = jnp.where(grp == g, p, none)
        m1 = jnp.max(pg, axis=0, keepdims=True)
        i1 = jnp.min(jnp.where(pg == m1, eid, far), axis=0, keepdims=True)
        pg2 = jnp.where(eid == i1, none, pg)
        m2 = jnp.max(pg2, axis=0, keepdims=True)
        i2 = jnp.min(jnp.where(pg2 == m2, eid, far), axis=0, keepdims=True)
        cand = (m1 + m2, m1, i1, m2, i2)
        if best is None:
            best = cand
        else:
            better = cand[0] > best[0]
            best = tuple(jnp.where(better, c, o) for c, o in zip(cand, best))
    _, m1, i1, m2, i2 = best
    tot = m1 + m2
    sub = lax.broadcasted_iota(jnp.int32, (SUBLANES, n_tok), 0)
    out = jnp.where(sub == 0, i1.astype(F32), 0.0)
    out = jnp.where(sub == 1, i2.astype(F32), out)
    out = jnp.where(sub == 2, m1 / tot, out)
    out = jnp.where(sub == 3, m2 / tot, out)
    return out


def _merge_kernel(xc_ref, xl_ref, modc_ref, modl_ref, q_ref, k_ref, v_ref, la_ref, ob_ref, yag_ref, gb_ref,
                  sr_ref, glag_ref, wb_ref, wout_ref, g2_ref, wr_ref, br_ref,
                  x1_ref, h2_ref, route_ref, routet_ref, s_ref, o_sc, qin_sc, kv_sc, dm_sc):
    first = pl.program_id(1) == 0

    @pl.when(first)
    def _():
        s_ref[...] = jnp.zeros_like(s_ref)

    _gla_tiles(_per_batch(q_ref), _per_batch(k_ref), _per_batch(v_ref), _per_batch(la_ref),
               _per_batch(s_ref), _per_batch(o_sc), qin_sc, kv_sc, dm_sc, reverse=False)

    d = xc_ref.shape[2]
    hv = d // GLA_HEADS
    for g in range(xc_ref.shape[0]):
        mod = jnp.where(first, modc_ref[0], modl_ref[g])
        o = o_sc[g] + ob_ref[g].astype(F32)
        parts = []
        for h in range(GLA_HEADS):
            oh = o[:, h * hv:(h + 1) * hv]
            parts.append(oh * lax.rsqrt(jnp.mean(oh * oh, axis=-1, keepdims=True) + EPS))
        on = jnp.concatenate(parts, axis=1) * glag_ref[...]
        yb = _dot((on * sr_ref[g].astype(F32)).astype(BF16), wb_ref[...])
        m = yag_ref[g].astype(F32) + gb_ref[g].astype(F32) * yb
        y = _dot(m.astype(BF16), wout_ref[...])
        gt1 = mod[:, 2 * d:3 * d]
        sh2 = mod[:, 3 * d:4 * d]
        sc2 = mod[:, 4 * d:5 * d]
        x1 = jnp.where(first, xc_ref[g], xl_ref[g]) + gt1 * y
        x1_ref[g] = x1
        h2 = x1 * lax.rsqrt(jnp.mean(x1 * x1, axis=-1, keepdims=True) + EPS) * g2_ref[...]
        h2 = h2 * (1.0 + sc2) + sh2
        h2_ref[g] = _pack_halves(h2)
        h_hi = h2.astype(BF16)
        h_lo = (h2 - h_hi.astype(F32)).astype(BF16)
        logits = _dot(h_hi, wr_ref[0]) + _dot(h_lo, wr_ref[0]) + _dot(h_hi, wr_ref[1]) + br_ref[...]
        rt = _route(jnp.transpose(logits)[0:N_EXPERTS, :])
        routet_ref[g, 0] = rt
        sub = lax.broadcasted_iota(jnp.int32, (LANES, rt.shape[1]), 0)
        padded = jnp.zeros((LANES, rt.shape[1]), F32)
        for r in range(4):
            padded = jnp.where(sub == r, rt[r:r + 1, :], padded)
        route_ref[g] = jnp.transpose(padded)


def _merge(xc, xl, joined, nt, mod3, q, k, v, la, ob, yag, gb, sr, glag, wb, wout, g2, wr, br, n_batch):
    d = xc.shape[2]
    dk = q.shape[1]
    t_all = nt // n_batch
    nj = t_all // TM
    per_batch = lambda a: a.reshape(n_batch, t_all, a.shape[1])
    tile = lambda w: pl.BlockSpec((BG, TM, w), lambda bp, j: (bp, j, 0))
    x1, h2, route, route_t = pl.pallas_call(
        _merge_kernel,
        grid=(n_batch // BG, nj),
        in_specs=_stream_specs(joined, n_batch, mod3.shape[2], d)
        + [tile(dk), tile(dk), tile(d), tile(dk), tile(d), tile(d), tile(d), tile(d)]
        + [_const_spec(a.shape) for a in (glag, wb, wout, g2, wr, br)],
        out_specs=[tile(d), tile(d // 2), tile(LANES),
                   pl.BlockSpec((BG, 1, SUBLANES, TM), lambda bp, j: (bp, j, 0, 0))],
        out_shape=[jax.ShapeDtypeStruct((n_batch, t_all, d), F32),
                   jax.ShapeDtypeStruct((n_batch, t_all, d // 2), jnp.uint32),
                   jax.ShapeDtypeStruct((n_batch, t_all, LANES), F32),
                   jax.ShapeDtypeStruct((n_batch, nj, SUBLANES, TM), F32)],
        scratch_shapes=_gla_scratch(BG, TM, dk, d),
        compiler_params=_params(("arbitrary", "arbitrary")),
        name="merge",
    )(xc, xl, mod3, mod3, per_batch(q), per_batch(k), per_batch(v), per_batch(la), per_batch(ob),
      per_batch(yag), per_batch(gb), per_batch(sr), glag, wb, wout, g2, wr, br)
    return (x1.reshape(nt, d), h2.reshape(nt, d // 2), route.reshape(nt, LANES),
            route_t.reshape(nt // TM, SUBLANES, TM))


def _plan_kernel(rt_ref, dest_ref, be_ref, cnt_ref, pst_ref):
    n_tiles, _, rows = rt_ref.shape
    sub = lax.broadcasted_iota(jnp.int32, (N_EXPERTS, rows), 0).astype(F32)
    wide = lambda col: jnp.broadcast_to(col, (N_EXPERTS, LANES))

    def one_hots(i):
        rt = rt_ref[i]
        return jnp.where(sub == rt[0:1, :], 1.0, 0.0), jnp.where(sub == rt[1:2, :], 1.0, 0.0)

    def count(i, acc):
        oh0, oh1 = one_hots(i)
        return acc + jnp.sum(oh0 + oh1, axis=1, keepdims=True)

    cnt = wide(lax.fori_loop(0, n_tiles, count, jnp.zeros((N_EXPERTS, 1), F32)))
    padded = jnp.floor((cnt + (TB - 1)) * (1.0 / TB)) * TB
    ri = lax.broadcasted_iota(jnp.int32, (N_EXPERTS, N_EXPERTS), 0)
    ci = lax.broadcasted_iota(jnp.int32, (N_EXPERTS, N_EXPERTS), 1)
    p_end = jnp.dot(jnp.where(ci <= ri, 1.0, 0.0), padded, preferred_element_type=F32, precision=HIGHEST)
    p_start = p_end - padded
    cnt_ref[...] = cnt
    pst_ref[...] = p_start
    starts = lax.broadcasted_iota(jnp.int32, (N_EXPERTS, be_ref.shape[1]), 1).astype(F32) * TB
    done = jnp.sum(jnp.where(p_end[:, 0:1] <= starts, 1.0, 0.0), axis=0, keepdims=True)
    be_ref[...] = jnp.minimum(done, N_EXPERTS - 1.0).astype(jnp.int32)

    rr = lax.broadcasted_iota(jnp.int32, (rows, rows), 0)
    cc = lax.broadcasted_iota(jnp.int32, (rows, rows), 1)
    earlier = jnp.where(rr < cc, 1.0, 0.0).astype(BF16)

    def place(i, run):
        oh0, oh1 = one_hots(i)
        c0 = _dot(oh0.astype(BF16), earlier)
        c1 = _dot(oh1.astype(BF16), earlier)
        tot0 = jnp.sum(oh0, axis=1, keepdims=True)
        base = p_start[:, 0:1] + run
        d0 = jnp.sum(oh0 * (c0 + base), axis=0, keepdims=True)
        d1 = jnp.sum(oh1 * (c1 + base + tot0), axis=0, keepdims=True)
        dest_ref[i] = jnp.concatenate([d0, d1], axis=0).astype(jnp.int32)
        return run + tot0 + jnp.sum(oh1, axis=1, keepdims=True)

    lax.fori_loop(0, n_tiles, place, jnp.zeros((N_EXPERTS, 1), F32))


def _plan(route_t, n_blocks):
    n_tiles = route_t.shape[0]
    nbp = -(-n_blocks // LANES) * LANES
    small = jax.ShapeDtypeStruct((N_EXPERTS, LANES), F32)
    return pl.pallas_call(
        _plan_kernel,
        out_shape=[jax.ShapeDtypeStruct((n_tiles, TOP_K, TM), jnp.int32),
                   jax.ShapeDtypeStruct((1, nbp), jnp.int32), small, small],
        compiler_params=pltpu.CompilerParams(vmem_limit_bytes=VMEM_LIMIT),
        name="plan",
    )(route_t)


def _sorted_rows(dest, block_e, cnt, pst, n_tok, n_blocks):
    cnt = cnt[:, 0].astype(jnp.int32)
    pst = pst[:, 0].astype(jnp.int32)
    pad = (cnt + TB - 1) // TB * TB - cnt
    pad_before = jnp.cumsum(pad) - pad
    be = block_e[0, :n_blocks]
    base = TOP_K * n_tok - pst[be] - cnt[be] + pad_before[be]
    init = base[:, None] + jnp.arange(n_blocks * TB, dtype=jnp.int32).reshape(n_blocks, TB)
    tok = jnp.arange(n_tok, dtype=jnp.int32).reshape(-1, 1, TM)
    slot_id = tok + n_tok * jnp.arange(TOP_K, dtype=jnp.int32)[None, :, None]
    slot = init.reshape(-1).at[dest.reshape(-1)].set(slot_id.reshape(-1), unique_indices=True)
    src = jnp.where(slot < TOP_K * n_tok, slot % n_tok, 0)
    return slot.reshape(n_blocks, 1, TB), src.reshape(n_blocks, 1, TB)


def _moe_kernel(be_ref, src_cur, src_nxt, src_nx2, slot_prv, slot_cur, h_hbm, wg_ref, wu_ref, wd_ref, y_hbm,
                x0, x1, x2, y0, y1, y2, gsem, ssem):
    del be_ref
    i = pl.program_id(0)
    last = pl.num_programs(0) - 1
    xs, ys = (x0, x1, x2), (y0, y1, y2)

    def gather_start(idx_ref, dst, sem):
        for r in range(TB):
            pltpu.make_async_copy(h_hbm.at[pl.ds(idx_ref[0, 0, r], 1)], dst.at[pl.ds(r, 1)], sem).start(
                priority=r % 2)

    def gather_wait(dst, sem):
        pltpu.make_async_copy(h_hbm.at[pl.ds(0, TB)], dst, sem).wait()

    def scatter_start(idx_ref, src, sem):
        for r in range(TB):
            pltpu.make_async_copy(src.at[pl.ds(r, 1)], y_hbm.at[pl.ds(idx_ref[0, 0, r], 1)], sem).start(
                priority=r % 2)

    def scatter_wait(src, sem):
        pltpu.make_async_copy(src, y_hbm.at[pl.ds(0, TB)], sem).wait()

    def experts(x_ref, y_ref):
        x = _unpack_halves(x_ref[...]).astype(BF16)
        g = _dot(x, wg_ref[0])
        u = _dot(x, wu_ref[0])
        a = (g * jax.nn.sigmoid(g) * u).astype(BF16)
        y_ref[...] = _pack_halves(_dot(a, wd_ref[0]))

    for p in range(3):
        @pl.when(jnp.logical_and(i > 0, lax.rem(i, 3) == p))
        def _(p=p):
            prv, nx2 = (p + 2) % 3, (p + 2) % 3
            gather_wait(xs[p], gsem.at[p])

            @pl.when(i >= 3)
            def _():
                scatter_wait(ys[p], ssem.at[p])

            gather_start(src_nx2, xs[nx2], gsem.at[nx2])
            scatter_start(slot_prv, ys[prv], ssem.at[prv])
            experts(xs[p], ys[p])

    for p in range(3):
        @pl.when(jnp.logical_and(i == last, lax.rem(i, 3) == p))
        def _(p=p):
            nxt, prv = (p + 1) % 3, (p + 2) % 3
            scatter_start(slot_cur, ys[p], ssem.at[p])
            scatter_wait(ys[nxt], ssem.at[nxt])
            scatter_wait(ys[prv], ssem.at[prv])
            scatter_wait(ys[p], ssem.at[p])
            gather_wait(xs[nxt], gsem.at[nxt])
            gather_wait(xs[prv], gsem.at[prv])

    @pl.when(i == 0)
    def _():
        gather_start(src_cur, x0, gsem.at[0])
        gather_start(src_nxt, x1, gsem.at[1])
        gather_wait(x0, gsem.at[0])
        gather_start(src_nx2, x2, gsem.at[2])
        experts(x0, y0)


def _moe(block_e, slot3, src3, h2, wg, wu, wd, layer):
    nb = slot3.shape[0]
    dh = h2.shape[1]
    d, de = wg.shape[1], wg.shape[2]
    expert = lambda i, be: (layer * N_EXPERTS + be[i], 0, 0)
    idx_spec = lambda f: pl.BlockSpec((1, 1, TB), f, memory_space=pltpu.SMEM)
    assert nb >= 3
    cur = lambda i, be: (i, 0, 0)
    nxt = lambda i, be: (jnp.minimum(i + 1, nb - 1), 0, 0)
    nx2 = lambda i, be: (jnp.minimum(i + 2, nb - 1), 0, 0)
    prv = lambda i, be: (jnp.maximum(i - 1, 0), 0, 0)
    vm = pltpu.VMEM((TB, dh), jnp.uint32)
    return pl.pallas_call(
        _moe_kernel,
        grid_spec=pltpu.PrefetchScalarGridSpec(
            num_scalar_prefetch=1,
            grid=(nb,),
            in_specs=[idx_spec(cur), idx_spec(nxt), idx_spec(nx2), idx_spec(prv), idx_spec(cur),
                      pl.BlockSpec(memory_space=pl.ANY),
                      pl.BlockSpec((1, d, de), expert),
                      pl.BlockSpec((1, d, de), expert),
                      pl.BlockSpec((1, de, d), expert)],
            out_specs=pl.BlockSpec(memory_space=pl.ANY),
            scratch_shapes=[vm] * 6 + [pltpu.SemaphoreType.DMA((3,)), pltpu.SemaphoreType.DMA((3,))],
        ),
        out_shape=jax.ShapeDtypeStruct((nb * TB, dh), jnp.uint32),
        compiler_params=_params(("arbitrary",)),
        name="moe",
    )(block_e, src3, src3, src3, slot3, slot3, h2, wg, wu, wd)


def _combine_kernel(y0_ref, y1_ref, x1_ref, route_ref, mod_ref, gf_ref, o_ref, *, final_norm):
    d = x1_ref.shape[1]
    y = route_ref[:, 2:3] * _unpack_halves(y0_ref[...]) + route_ref[:, 3:4] * _unpack_halves(y1_ref[...])
    x2 = x1_ref[...] + mod_ref[:, 5 * d:6 * d] * y
    if final_norm:
        x2 = x2 * lax.rsqrt(jnp.mean(x2 * x2, axis=-1, keepdims=True) + EPS) * gf_ref[...]
    o_ref[...] = x2


def _combine(y_rows, x1, route, mod3, gf, n_batch, final_norm):
    nt, d = x1.shape
    nj = nt // n_batch // TM
    ctx_row = n_batch

    def modrow(i):
        return (jnp.where(i % nj == 0, ctx_row, i // nj), 0, 0)

    if final_norm:
        out_rows = nt - n_batch * TM
        out_map = lambda i: ((i // nj) * (nj - 1) + jnp.maximum(i % nj - 1, 0), 0)
    else:
        out_rows = nt
        out_map = lambda i: (i, 0)
    return pl.pallas_call(
        functools.partial(_combine_kernel, final_norm=final_norm),
        grid=(nt // TM,),
        in_specs=[pl.BlockSpec((TM, d // 2), lambda i: (i, 0)),
                  pl.BlockSpec((TM, d // 2), lambda i: (nt // TM + i, 0)),
                  pl.BlockSpec((TM, d), lambda i: (i, 0)),
                  pl.BlockSpec((TM, LANES), lambda i: (i, 0)),
                  pl.BlockSpec((None, 1, mod3.shape[2]), modrow),
                  _const_spec(gf.shape)],
        out_specs=pl.BlockSpec((TM, d), out_map),
        out_shape=jax.ShapeDtypeStruct((out_rows, d), F32),
        compiler_params=_params(("arbitrary",)),
        name="combine",
    )(y_rows, y_rows, x1, route, mod3, gf)


def kernel(x, c, ctx, c_ctx, w_mod, b_mod, g_norm1, g_norm2, w_in, w_gate2, b_gate2, gla_norm_g,
           sgu_ln_g, sgu_ln_b, sgu_w, sgu_b, w_branch_a, w_branch_b, b_branch, w_out,
           w_router, b_router, w_exp_gate, w_exp_up, w_exp_down, g_final):
    n_batch, seq, d = x.shape
    ctx_len = ctx.shape[1]
    depth = w_mod.shape[0]
    dk = w_gate2.shape[3]
    rank_lr = w_gate2.shape[2]
    assert ctx_len == TM and seq % TM == 0 and n_batch < MOD_ROWS

    nt = n_batch * (ctx_len + seq)
    streams = (ctx, x, False)
    n_blocks = TOP_K * nt // TB + N_EXPERTS
    cc = jnp.zeros((MOD_ROWS, d), F32).at[:n_batch].set(c).at[n_batch].set(c_ctx)
    mod = _modulation(cc, w_mod, b_mod)

    wr = jnp.zeros((d, LANES), F32).at[:, :N_EXPERTS].set(w_router)
    wr_hi = wr.astype(BF16)
    wr = jnp.stack([wr_hi, (wr - wr_hi.astype(F32)).astype(BF16)])
    br = jnp.zeros((1, LANES), F32).at[0, :N_EXPERTS].set(b_router)
    row = lambda a: a.reshape(1, -1)
    n_main = 2 * d + 2 * dk + 2 * d
    all_layers = lambda w: w.astype(BF16).reshape(-1, w.shape[2], w.shape[3])
    wg_all, wu_all, wd_all = all_layers(w_exp_gate), all_layers(w_exp_up), all_layers(w_exp_down)

    out = None
    for l in range(depth):
        last = l == depth - 1
        mod3 = mod[l].reshape(MOD_ROWS, 1, 6 * d)
        wmain = w_in[l][:, :n_main].astype(BF16)
        wlr = w_in[l][:, n_main:n_main + 2 * rank_lr].astype(BF16)
        wgates = w_in[l][:, n_main + 2 * rank_lr:].astype(BF16)
        wg2 = jnp.zeros((2 * rank_lr, 2 * dk), F32)
        wg2 = wg2.at[:rank_lr, :dk].set(w_gate2[l, 0]).at[rank_lr:, dk:].set(w_gate2[l, 1]).astype(BF16)
        bg2 = b_gate2[l].reshape(1, 2 * dk)
        sgub = jnp.repeat(sgu_b[l].T, d // A_GROUPS, axis=1)

        yag, gb, q, k, v, sr, la = _inproj(
            *streams, nt, mod3, row(g_norm1[l]), wmain, wlr, wgates, wg2, bg2, row(sgu_ln_g[l]), row(sgu_ln_b[l]),
            sgu_w[l].astype(BF16), sgub, w_branch_a[l].astype(BF16), row(b_branch[l]), n_batch)
        ob = _gla_bwd(q, k, v, la, n_batch)
        x1, h2, route, route_t = _merge(
            *streams, nt, mod3, q, k, v, la, ob, yag, gb, sr, row(gla_norm_g[l]), w_branch_b[l].astype(BF16),
            w_out[l].astype(BF16), row(g_norm2[l]), wr, br, n_batch)
        dest, block_e, cnt, pst = _plan(route_t, n_blocks)
        slot3, src3 = _sorted_rows(dest, block_e, cnt, pst, nt, n_blocks)
        y_rows = _moe(block_e.reshape(-1), slot3, src3, h2, wg_all, wu_all, wd_all, l)
        res = _combine(y_rows, x1, route, mod3, row(g_final), n_batch, final_norm=last)
        if last:
            out = res.reshape(n_batch, seq, d)
        else:
            res = res.reshape(n_batch, -1, d)
            streams = (res, res, True)
    return out
```

```python
import functools

import jax
import jax.numpy as jnp
from jax import lax
from jax.experimental import pallas as pl
from jax.experimental.pallas import tpu as pltpu

F32 = jnp.float32
BF16 = jnp.bfloat16
HIGHEST = lax.Precision.HIGHEST

A_CHUNK = 128
A_GROUPS = 8
GLA_HEADS = 4
GLA_TAU = 16.0
GLA_CHUNK = 64
N_EXPERTS = 16
N_EXPERT_GROUPS = 4
EXPERTS_PER_GROUP = N_EXPERTS // N_EXPERT_GROUPS
TOP_K = 2
EPS = 1e-6

LANES = 128
SUBLANES = 8
TM = 256
TB = 256
BG = 2
MOD_ROWS = 16
VMEM_LIMIT = 56 * 1024 * 1024


def _dot(a, b):
    return jnp.dot(a, b, preferred_element_type=F32)


def _pack_halves(x):
    n = x.shape[1] // 2
    bits = lambda v: pltpu.bitcast(v.astype(BF16).astype(F32), jnp.uint32)
    return bits(x[:, :n]) | (bits(x[:, n:]) >> 16)


def _unpack_halves(p):
    hi = pltpu.bitcast(p & jnp.uint32(0xFFFF0000), F32)
    lo = pltpu.bitcast(p << 16, F32)
    return jnp.concatenate([hi, lo], axis=1)


def _const_spec(shape):
    nd = len(shape)
    return pl.BlockSpec(shape, lambda *_: (0,) * nd)


def _params(sem):
    return pltpu.CompilerParams(dimension_semantics=sem, vmem_limit_bytes=VMEM_LIMIT)


def _mod_kernel(cc_ref, w_ref, b_ref, o_ref):
    cc = cc_ref[...]
    s = cc * jax.nn.sigmoid(cc)
    o_ref[0] = jnp.dot(s, w_ref[0], preferred_element_type=F32, precision=HIGHEST) + b_ref[0]


def _modulation(cc, w_mod, b_mod):
    n_layer, d, six_d = w_mod.shape
    return pl.pallas_call(
        _mod_kernel,
        grid=(n_layer, six_d // d),
        in_specs=[
            pl.BlockSpec((MOD_ROWS, d), lambda l, j: (0, 0)),
            pl.BlockSpec((1, d, d), lambda l, j: (l, 0, j)),
            pl.BlockSpec((1, 1, d), lambda l, j: (l, 0, j)),
        ],
        out_specs=pl.BlockSpec((1, MOD_ROWS, d), lambda l, j: (l, 0, j)),
        out_shape=jax.ShapeDtypeStruct((n_layer, MOD_ROWS, six_d), F32),
        compiler_params=_params(("arbitrary", "arbitrary")),
        name="modulation",
    )(cc, w_mod, b_mod.reshape(n_layer, 1, six_d))


def _gelu_tanh(x):
    c = 0.7978845608028654
    return 0.5 * x * (1.0 + jnp.tanh(c * (x + 0.044715 * (x * x * x))))


def _log_sigmoid(z):
    return jnp.minimum(z, 0.0) - jnp.log1p(jnp.exp(-jnp.abs(z)))


def _inproj_kernel(xc_ref, xl_ref, modc_ref, modl_ref, g1_ref, wmain_ref, wlr_ref, wgates_ref, wg2_ref, bg2_ref,
                   lng_ref, lnb_ref, sguw_ref, sgub_ref, wa_ref, bbr_ref,
                   yag_ref, gb_ref, q_ref, k_ref, v_ref, sr_ref, la_ref, sa_ref):
    first = pl.program_id(1) == 0
    n_g, rows, d = xc_ref.shape
    dk = q_ref.shape[2]
    gdim = d // A_GROUPS
    head_k = dk // GLA_HEADS
    c_q, c_v = 2 * d, 2 * d + 2 * dk
    hb, u, vnb, ya = {}, {}, {}, {}

    def norm(g):
        mod = jnp.where(first, modc_ref[0], modl_ref[g])
        x = jnp.where(first, xc_ref[g], xl_ref[g])
        h = x * lax.rsqrt(jnp.mean(x * x, axis=-1, keepdims=True) + EPS) * g1_ref[...]
        hb[g] = (h * (1.0 + mod[:, d:2 * d]) + mod[:, 0:d]).astype(BF16)

    def gate_u(g):
        u[g] = _gelu_tanh(_dot(hb[g], wmain_ref[:, 0:d]))

    def gate_v(g):
        vv = _gelu_tanh(_dot(hb[g], wmain_ref[:, d:2 * d]))
        vc = vv - jnp.mean(vv, axis=-1, keepdims=True)
        vn = vc * lax.rsqrt(jnp.mean(vc * vc, axis=-1, keepdims=True) + EPS) * lng_ref[...] + lnb_ref[...]
        vnb[g] = vn.astype(BF16)

    def spatial(g):
        for n in range(rows // A_CHUNK):
            rs = slice(n * A_CHUNK, (n + 1) * A_CHUNK)
            for a in range(A_GROUPS):
                cs = slice(a * gdim, (a + 1) * gdim)
                mixed = _dot(sguw_ref[a], vnb[g][rs, cs]) + sgub_ref[:, cs]
                sa_ref[g, rs, cs] = (u[g][rs, cs] * mixed).astype(BF16)

    def proj_a(g):
        ya[g] = _dot(sa_ref[g], wa_ref[...])

    def branch_gates(g):
        gates = _dot(hb[g], wgates_ref[...]) + bbr_ref[...]
        yag_ref[g] = (jax.nn.sigmoid(gates[:, 0:d]) * ya[g]).astype(BF16)
        gb_ref[g] = jax.nn.sigmoid(gates[:, d:2 * d]).astype(BF16)

    def qk(g):
        q_ref[g] = (_dot(hb[g], wmain_ref[:, c_q:c_q + dk]) * (head_k ** -0.5)).astype(BF16)
        k_ref[g] = _dot(hb[g], wmain_ref[:, c_q + dk:c_q + 2 * dk]).astype(BF16)

    def val(g):
        v_ref[g] = _dot(hb[g], wmain_ref[:, c_v:c_v + d]).astype(BF16)

    def out_gate(g):
        r = _dot(hb[g], wmain_ref[:, c_v + d:c_v + 2 * d])
        sr_ref[g] = (r * jax.nn.sigmoid(r)).astype(BF16)

    def decay(g):
        lr = _dot(hb[g], wlr_ref[...]).astype(BF16)
        z = _dot(lr, wg2_ref[...]) + bg2_ref[...]
        la_ref[g] = _log_sigmoid(z) * (1.0 / GLA_TAU)

    order = [norm, gate_u, decay, gate_v, qk, spatial, out_gate, proj_a, branch_gates, val]
    for stage in order:
        for g in range(n_g):
            stage(g)


def _stream_specs(joined, ctx_row, six_d, d):
    lat_map = (lambda bp, j: (bp, jnp.maximum(j, 1), 0)) if joined else (lambda bp, j: (bp, jnp.maximum(j - 1, 0), 0))
    return [pl.BlockSpec((BG, TM, d), lambda bp, j: (bp, 0, 0)), pl.BlockSpec((BG, TM, d), lat_map),
            pl.BlockSpec((1, 1, six_d), lambda bp, j: (ctx_row, 0, 0)),
            pl.BlockSpec((BG, 1, six_d), lambda bp, j: (bp, 0, 0))]


def _resident_spec(shape):
    nd = len(shape)
    return pl.BlockSpec(shape, lambda *_: (0,) * nd, pipeline_mode=pl.Buffered(1))


def _inproj(xc, xl, joined, nt, mod3, g1, wmain, wlr, wgates, wg2, bg2, lng, lnb, sguw, sgub, wa, bbr, n_batch):
    d = xc.shape[2]
    dk = wg2.shape[1] // 2
    t_all = nt // n_batch
    nj = t_all // TM
    tile = lambda w: pl.BlockSpec((BG, TM, w), lambda bp, j: (bp, j, 0))
    outs = [(d, BF16), (d, BF16), (dk, BF16), (dk, BF16), (d, BF16), (d, BF16), (2 * dk, F32)]
    res = pl.pallas_call(
        _inproj_kernel,
        grid=(n_batch // BG, nj),
        in_specs=_stream_specs(joined, n_batch, mod3.shape[2], d)
        + [_resident_spec(a.shape) for a in (g1, wmain, wlr, wgates, wg2, bg2, lng, lnb, sguw, sgub, wa, bbr)],
        out_specs=[tile(w) for w, _ in outs],
        out_shape=[jax.ShapeDtypeStruct((n_batch, t_all, w), t) for w, t in outs],
        scratch_shapes=[pltpu.VMEM((BG, TM, d), BF16)],
        compiler_params=_params(("arbitrary", "arbitrary")),
        name="inproj",
    )(xc, xl, mod3, mod3, g1, wmain, wlr, wgates, wg2, bg2, lng, lnb, sguw, sgub, wa, bbr)
    return [a.reshape(nt, a.shape[2]) for a in res]


def _gla_tiles(q_refs, k_refs, v_refs, la_refs, s_refs, o_refs, qin_sc, kv_sc, dm_sc, reverse):
    n_g = len(q_refs)
    rows, dk = q_refs[0].shape
    dv = v_refs[0].shape[1]
    hk = dk // GLA_HEADS
    hv = dv // GLA_HEADS
    c = GLA_CHUNK
    ri = lax.broadcasted_iota(jnp.int32, (c, c), 0)
    ci = lax.broadcasted_iota(jnp.int32, (c, c), 1)
    tri = (ci >= ri) if reverse else (ci <= ri)
    rt = lax.broadcasted_iota(jnp.int32, (rows, rows), 0)
    ct = lax.broadcasted_iota(jnp.int32, (rows, rows), 1)
    in_chunk = (rt // c) == (ct // c)
    tri_t = jnp.where(jnp.logical_and(in_chunk, (ct >= rt) if reverse else (ct <= rt)), 1.0, 0.0).astype(BF16)
    chunks = list(range(rows // c))
    for g in range(n_g):
        la = la_refs[g][...]
        la_hi = la.astype(BF16)
        la_lo = (la - la_hi.astype(F32)).astype(BF16)
        b_all = _dot(tri_t, la_hi) + _dot(tri_t, la_lo)
        for n in chunks:
            rs = slice(n * c, (n + 1) * c)
            b = b_all[rs, :]
            b_end = b[0:1, :] if reverse else b[c - 1:c, :]
            q = q_refs[g][rs, :].astype(F32)
            k = k_refs[g][rs, :].astype(F32)
            q_in = (q * jnp.exp(b)).astype(BF16)
            k_in = (k * jnp.exp(-b)).astype(BF16)
            k_st = (k * jnp.exp(b_end - b)).astype(BF16)
            decay = jnp.exp(b_end)
            qin_sc[g, rs, :] = q_in
            for h in range(GLA_HEADS):
                ks = slice(h * hk, (h + 1) * hk)
                vs = slice(h * hv, (h + 1) * hv)
                vh = v_refs[g][rs, vs]
                att = lax.dot_general(q_in[:, ks], k_in[:, ks], (((1,), (1,)), ((), ())),
                                      preferred_element_type=F32)
                o_refs[g][rs, vs] = _dot(jnp.where(tri, att, 0.0).astype(BF16), vh)
                kv_sc[g, n, h] = lax.dot_general(k_st[:, ks], vh, (((0,), (0,)), ((), ())),
                                                 preferred_element_type=F32)
                dm_sc[g, n, h] = jnp.transpose(jnp.broadcast_to(decay[:, ks], (hk, hk)))
    for n in (reversed(chunks) if reverse else chunks):
        rs = slice(n * c, (n + 1) * c)
        for g in range(n_g):
            for h in range(GLA_HEADS):
                ks = slice(h * hk, (h + 1) * hk)
                vs = slice(h * hv, (h + 1) * hv)
                state = s_refs[g][h]
                o_refs[g][rs, vs] += _dot(qin_sc[g, rs, ks], state.astype(BF16))
                dmat = jnp.concatenate([dm_sc[g, n, h]] * (hv // hk), axis=1)
                s_refs[g][h] = dmat * state + kv_sc[g, n, h]


def _gla_scratch(n_g, rows, dk, dv):
    hk, hv, n_chunks = dk // GLA_HEADS, dv // GLA_HEADS, rows // GLA_CHUNK
    return [pltpu.VMEM((n_g, GLA_HEADS, hk, hv), F32),
            pltpu.VMEM((n_g, rows, dv), F32),
            pltpu.VMEM((n_g, rows, dk), BF16),
            pltpu.VMEM((n_g, n_chunks, GLA_HEADS, hk, hv), F32),
            pltpu.VMEM((n_g, n_chunks, GLA_HEADS, hk, hk), F32)]


def _per_batch(ref):
    return [ref.at[g] for g in range(ref.shape[0])]


def _gla_bwd_kernel(q_ref, k_ref, v_ref, la_ref, ob_ref, s_ref, o_sc, qin_sc, kv_sc, dm_sc):
    @pl.when(pl.program_id(1) == 0)
    def _():
        s_ref[...] = jnp.zeros_like(s_ref)

    _gla_tiles(_per_batch(q_ref), _per_batch(k_ref), _per_batch(v_ref), _per_batch(la_ref),
               _per_batch(s_ref), _per_batch(o_sc), qin_sc, kv_sc, dm_sc, reverse=True)
    ob_ref[...] = o_sc[...].astype(BF16)


def _gla_bwd(q, k, v, la, n_batch):
    nt, dk = q.shape
    dv = v.shape[1]
    t_all = nt // n_batch
    nj = t_all // TM
    per_batch = lambda a: a.reshape(n_batch, t_all, a.shape[1])

    def tile(bp, jj):
        return (bp, jnp.where(jj == 0, 0, nj - jj), 0)

    def tile_la(bp, jj):
        return (bp, jnp.where(jj == 0, 0, nj - jj), 1)

    ob = pl.pallas_call(
        _gla_bwd_kernel,
        grid=(n_batch // BG, nj),
        in_specs=[pl.BlockSpec((BG, TM, dk), tile), pl.BlockSpec((BG, TM, dk), tile),
                  pl.BlockSpec((BG, TM, dv), tile), pl.BlockSpec((BG, TM, dk), tile_la)],
        out_specs=pl.BlockSpec((BG, TM, dv), tile),
        out_shape=jax.ShapeDtypeStruct((n_batch, t_all, dv), BF16),
        scratch_shapes=_gla_scratch(BG, TM, dk, dv),
        compiler_params=_params(("arbitrary", "arbitrary")),
        name="gla_bwd",
    )(per_batch(q), per_batch(k), per_batch(v), per_batch(la))
    return ob.reshape(nt, dv)


def _route(logits_t):
    n_tok = logits_t.shape[1]
    eid = lax.broadcasted_iota(jnp.int32, logits_t.shape, 0)
    ex = jnp.exp(logits_t - jnp.max(logits_t, axis=0, keepdims=True))
    p = ex / jnp.sum(ex, axis=0, keepdims=True)
    grp = eid // EXPERTS_PER_GROUP
    none = -1.0
    far = 2 * N_EXPERTS
    best = None
    for g in range(N_EXPERT_GROUPS):
        pg = jnp.where(grp == g, p, none)
        m1 = jnp.max(pg, axis=0, keepdims=True)
        i1 = jnp.min(jnp.where(pg == m1, eid, far), axis=0, keepdims=True)
        pg2 = jnp.where(eid == i1, none, pg)
        m2 = jnp.max(pg2, axis=0, keepdims=True)
        i2 = jnp.min(jnp.where(pg2 == m2, eid, far), axis=0, keepdims=True)
        cand = (m1 + m2, m1, i1, m2, i2)
        if best is None:
            best = cand
        else:
            better = cand[0] > best[0]
            best = tuple(jnp.where(better, c, o) for c, o in zip(cand, best))
    _, m1, i1, m2, i2 = best
    tot = m1 + m2
    sub = lax.broadcasted_iota(jnp.int32, (SUBLANES, n_tok), 0)
    out = jnp.where(sub == 0, i1.astype(F32), 0.0)
    out = jnp.where(sub == 1, i2.astype(F32), out)
    out = jnp.where(sub == 2, m1 / tot, out)
    out = jnp.where(sub == 3, m2 / tot, out)
    return out


def _merge_kernel(xc_ref, xl_ref, modc_ref, modl_ref, q_ref, k_ref, v_ref, la_ref, ob_ref, yag_ref, gb_ref,
                  sr_ref, glag_ref, wb_ref, wout_ref, g2_ref, wr_ref, br_ref,
                  x1_ref, h2_ref, route_ref, routet_ref, s_ref, o_sc, qin_sc, kv_sc, dm_sc):
    first = pl.program_id(1) == 0

    @pl.when(first)
    def _():
        s_ref[...] = jnp.zeros_like(s_ref)

    _gla_tiles(_per_batch(q_ref), _per_batch(k_ref), _per_batch(v_ref), _per_batch(la_ref),
               _per_batch(s_ref), _per_batch(o_sc), qin_sc, kv_sc, dm_sc, reverse=False)

    d = xc_ref.shape[2]
    hv = d // GLA_HEADS
    mod, yb_in, mix, h2, logits = {}, {}, {}, {}, {}

    def head_norm(g):
        mod[g] = jnp.where(first, modc_ref[0], modl_ref[g])
        o = o_sc[g] + ob_ref[g].astype(F32)
        parts = []
        for h in range(GLA_HEADS):
            oh = o[:, h * hv:(h + 1) * hv]
            parts.append(oh * lax.rsqrt(jnp.mean(oh * oh, axis=-1, keepdims=True) + EPS))
        on = jnp.concatenate(parts, axis=1) * glag_ref[...]
        yb_in[g] = (on * sr_ref[g].astype(F32)).astype(BF16)

    def branch_merge(g):
        yb = _dot(yb_in[g], wb_ref[...])
        mix[g] = (yag_ref[g].astype(F32) + gb_ref[g].astype(F32) * yb).astype(BF16)

    def residual(g):
        y = _dot(mix[g], wout_ref[...])
        x1 = jnp.where(first, xc_ref[g], xl_ref[g]) + mod[g][:, 2 * d:3 * d] * y
        x1_ref[g] = x1
        h = x1 * lax.rsqrt(jnp.mean(x1 * x1, axis=-1, keepdims=True) + EPS) * g2_ref[...]
        h2[g] = h * (1.0 + mod[g][:, 4 * d:5 * d]) + mod[g][:, 3 * d:4 * d]
        h2_ref[g] = _pack_halves(h2[g])

    def router_logits(g):
        h_hi = h2[g].astype(BF16)
        h_lo = (h2[g] - h_hi.astype(F32)).astype(BF16)
        logits[g] = _dot(h_hi, wr_ref[0]) + _dot(h_lo, wr_ref[0]) + _dot(h_hi, wr_ref[1]) + br_ref[...]

    def routing(g):
        rt = _route(jnp.transpose(logits[g])[0:N_EXPERTS, :])
        routet_ref[g, 0] = rt
        sub = lax.broadcasted_iota(jnp.int32, (LANES, rt.shape[1]), 0)
        padded = jnp.zeros((LANES, rt.shape[1]), F32)
        for r in range(4):
            padded = jnp.where(sub == r, rt[r:r + 1, :], padded)
        route_ref[g] = jnp.transpose(padded)

    for stage in (head_norm, branch_merge, residual, router_logits, routing):
        for g in range(xc_ref.shape[0]):
            stage(g)


def _merge(xc, xl, joined, nt, mod3, q, k, v, la, ob, yag, gb, sr, glag, wb, wout, g2, wr, br, n_batch):
    d = xc.shape[2]
    dk = q.shape[1]
    t_all = nt // n_batch
    nj = t_all // TM
    per_batch = lambda a: a.reshape(n_batch, t_all, a.shape[1])
    tile = lambda w: pl.BlockSpec((BG, TM, w), lambda bp, j: (bp, j, 0))
    x1, h2, route, route_t = pl.pallas_call(
        _merge_kernel,
        grid=(n_batch // BG, nj),
        in_specs=_stream_specs(joined, n_batch, mod3.shape[2], d)
        + [tile(dk), tile(dk), tile(d), tile(dk), tile(d), tile(d), tile(d), tile(d)]
        + [_const_spec(a.shape) for a in (glag, wb, wout, g2, wr, br)],
        out_specs=[tile(d), tile(d // 2), tile(LANES),
                   pl.BlockSpec((BG, 1, SUBLANES, TM), lambda bp, j: (bp, j, 0, 0))],
        out_shape=[jax.ShapeDtypeStruct((n_batch, t_all, d), F32),
                   jax.ShapeDtypeStruct((n_batch, t_all, d // 2), jnp.uint32),
                   jax.ShapeDtypeStruct((n_batch, t_all, LANES), F32),
                   jax.ShapeDtypeStruct((n_batch, nj, SUBLANES, TM), F32)],
        scratch_shapes=_gla_scratch(BG, TM, dk, d),
        compiler_params=_params(("arbitrary", "arbitrary")),
        name="merge",
    )(xc, xl, mod3, mod3, per_batch(q), per_batch(k), per_batch(v), per_batch(la), per_batch(ob),
      per_batch(yag), per_batch(gb), per_batch(sr), glag, wb, wout, g2, wr, br)
    return (x1.reshape(nt, d), h2.reshape(nt, d // 2), route.reshape(nt, LANES),
            route_t.reshape(nt // TM, SUBLANES, TM))


def _plan_kernel(rt_ref, slot_ref, be_ref, cnt_ref, pst_ref):
    n_tiles, _, rows = rt_ref.shape
    n_tok = n_tiles * rows
    n_slot = TOP_K * rows
    sub = lax.broadcasted_iota(jnp.int32, (N_EXPERTS, rows), 0).astype(F32)
    wide = lambda col: jnp.broadcast_to(col, (N_EXPERTS, LANES))

    def one_hots(i):
        rt = rt_ref[i]
        return jnp.where(sub == rt[0:1, :], 1.0, 0.0), jnp.where(sub == rt[1:2, :], 1.0, 0.0)

    def count(i, acc):
        oh0, oh1 = one_hots(i)
        return acc + jnp.sum(oh0 + oh1, axis=1, keepdims=True)

    cnt = wide(lax.fori_loop(0, n_tiles, count, jnp.zeros((N_EXPERTS, 1), F32)))
    padded = jnp.floor((cnt + (TB - 1)) * (1.0 / TB)) * TB
    ri = lax.broadcasted_iota(jnp.int32, (N_EXPERTS, N_EXPERTS), 0)
    ci = lax.broadcasted_iota(jnp.int32, (N_EXPERTS, N_EXPERTS), 1)
    p_end = jnp.dot(jnp.where(ci <= ri, 1.0, 0.0), padded, preferred_element_type=F32, precision=HIGHEST)
    p_start = p_end - padded
    cnt_ref[...] = cnt
    pst_ref[...] = p_start
    starts = lax.broadcasted_iota(jnp.int32, (N_EXPERTS, be_ref.shape[1]), 1).astype(F32) * TB
    done = jnp.sum(jnp.where(p_end[:, 0:1] <= starts, 1.0, 0.0), axis=0, keepdims=True)
    be_ref[...] = jnp.minimum(done, N_EXPERTS - 1.0).astype(jnp.int32)
    slot_ref[...] = jnp.full(slot_ref.shape, -1, jnp.int32)

    rr = lax.broadcasted_iota(jnp.int32, (rows, rows), 0)
    cc = lax.broadcasted_iota(jnp.int32, (rows, rows), 1)
    earlier = jnp.where(rr < cc, 1.0, 0.0).astype(BF16)
    before = jnp.where(ci < ri, 1.0, 0.0)
    pos = lax.broadcasted_iota(jnp.int32, (n_slot, n_slot), 0).astype(F32)
    s_idx = lax.broadcasted_iota(jnp.int32, (SUBLANES, n_slot), 1)
    s_sub = lax.broadcasted_iota(jnp.int32, (SUBLANES, n_slot), 0)
    tok_slot = jnp.where(s_sub == 0, s_idx % rows, jnp.where(s_sub == 1, s_idx // rows, 0)).astype(BF16)
    lane = lax.broadcasted_iota(jnp.int32, (1, TB), 1)
    p_start_i = p_start.astype(jnp.int32)

    def place(i, run):
        oh0, oh1 = one_hots(i)
        c0 = _dot(oh0.astype(BF16), earlier)
        c1 = _dot(oh1.astype(BF16), earlier)
        tot0 = jnp.sum(oh0, axis=1, keepdims=True)
        tot = tot0 + jnp.sum(oh1, axis=1, keepdims=True)
        off = jnp.dot(before, wide(tot), preferred_element_type=F32, precision=HIGHEST)[:, 0:1]
        lp = jnp.concatenate([jnp.sum(oh0 * (c0 + off), axis=0, keepdims=True),
                              jnp.sum(oh1 * (c1 + off + tot0), axis=0, keepdims=True)], axis=1)
        perm = jnp.where(pos == lp, 1.0, 0.0).astype(BF16)
        srt = lax.dot_general(tok_slot, perm, (((1,), (1,)), ((), ())), preferred_element_type=F32)
        ids = (srt[0:1, :] + srt[1:2, :] * n_tok).astype(jnp.int32) + i * rows
        ids = jnp.broadcast_to(jnp.concatenate([ids, jnp.zeros_like(ids)], axis=1), (SUBLANES, 2 * n_slot))
        tot_i, run_i, off_i = tot.astype(jnp.int32), run.astype(jnp.int32), off.astype(jnp.int32)
        for e in range(N_EXPERTS):
            n_e = tot_i[e, 0]
            row0 = p_start_i[e, 0] + run_i[e, 0]
            blk = row0 // TB
            at = row0 - blk * TB
            moved = pltpu.roll(ids, at - off_i[e, 0] + 2 * n_slot, axis=1)
            for w in range(-(-(TB + n_slot) // TB)):
                here = jnp.logical_and(lane + w * TB >= at, lane + w * TB < at + n_e)
                old = slot_ref[pl.ds(blk + w, 1), :]
                slot_ref[pl.ds(blk + w, 1), :] = jnp.where(here, moved[0:1, w * TB:(w + 1) * TB], old)
        return run + tot

    lax.fori_loop(0, n_tiles, place, jnp.zeros((N_EXPERTS, 1), F32))


def _plan(route_t, n_blocks):
    nbp = -(-(n_blocks + TOP_K * TM // TB) // LANES) * LANES
    small = jax.ShapeDtypeStruct((N_EXPERTS, LANES), F32)
    return pl.pallas_call(
        _plan_kernel,
        out_shape=[jax.ShapeDtypeStruct((nbp, TB), jnp.int32),
                   jax.ShapeDtypeStruct((1, nbp), jnp.int32), small, small],
        compiler_params=pltpu.CompilerParams(vmem_limit_bytes=VMEM_LIMIT),
        name="plan",
    )(route_t)


def _sorted_rows(slot_raw, block_e, cnt, pst, n_tok, n_blocks):
    cnt = cnt[:, 0].astype(jnp.int32)
    pst = pst[:, 0].astype(jnp.int32)
    pad = (cnt + TB - 1) // TB * TB - cnt
    pad_before = jnp.cumsum(pad) - pad
    be = block_e[0, :n_blocks]
    base = TOP_K * n_tok - pst[be] - cnt[be] + pad_before[be]
    pad_id = base[:, None] + jnp.arange(n_blocks * TB, dtype=jnp.int32).reshape(n_blocks, TB)
    raw = slot_raw[:n_blocks]
    slot = jnp.where(raw < 0, pad_id, raw)
    src = jnp.where(raw < 0, 0, raw % n_tok)
    return slot.reshape(n_blocks, 1, TB), src.reshape(n_blocks, 1, TB)


def _moe_kernel(be_ref, src_cur, src_nxt, src_nx2, slot_prv, slot_cur, h_hbm, wg_ref, wu_ref, wd_ref, y_hbm,
                x0, x1, x2, y0, y1, y2, gsem, ssem):
    del be_ref
    i = pl.program_id(0)
    last = pl.num_programs(0) - 1
    xs, ys = (x0, x1, x2), (y0, y1, y2)

    def gather_start(idx_ref, dst, sem):
        for r in range(TB):
            pltpu.make_async_copy(h_hbm.at[pl.ds(idx_ref[0, 0, r], 1)], dst.at[pl.ds(r, 1)], sem).start(
                priority=r % 2)

    def gather_wait(dst, sem):
        pltpu.make_async_copy(h_hbm.at[pl.ds(0, TB)], dst, sem).wait()

    def scatter_start(idx_ref, src, sem):
        for r in range(TB):
            pltpu.make_async_copy(src.at[pl.ds(r, 1)], y_hbm.at[pl.ds(idx_ref[0, 0, r], 1)], sem).start(
                priority=r % 2)

    def scatter_wait(src, sem):
        pltpu.make_async_copy(src, y_hbm.at[pl.ds(0, TB)], sem).wait()

    def experts(x_ref, y_ref):
        x = _unpack_halves(x_ref[...]).astype(BF16)
        g = _dot(x, wg_ref[0])
        u = _dot(x, wu_ref[0])
        a = (g * jax.nn.sigmoid(g) * u).astype(BF16)
        y_ref[...] = _pack_halves(_dot(a, wd_ref[0]))

    for p in range(3):
        @pl.when(jnp.logical_and(i > 0, lax.rem(i, 3) == p))
        def _(p=p):
            prv, nx2 = (p + 2) % 3, (p + 2) % 3
            gather_wait(xs[p], gsem.at[p])

            @pl.when(i >= 3)
            def _():
                scatter_wait(ys[p], ssem.at[p])

            gather_start(src_nx2, xs[nx2], gsem.at[nx2])
            scatter_start(slot_prv, ys[prv], ssem.at[prv])
            experts(xs[p], ys[p])

    for p in range(3):
        @pl.when(jnp.logical_and(i == last, lax.rem(i, 3) == p))
        def _(p=p):
            nxt, prv = (p + 1) % 3, (p + 2) % 3
            scatter_start(slot_cur, ys[p], ssem.at[p])
            scatter_wait(ys[nxt], ssem.at[nxt])
            scatter_wait(ys[prv], ssem.at[prv])
            scatter_wait(ys[p], ssem.at[p])
            gather_wait(xs[nxt], gsem.at[nxt])
            gather_wait(xs[prv], gsem.at[prv])

    @pl.when(i == 0)
    def _():
        gather_start(src_cur, x0, gsem.at[0])
        gather_start(src_nxt, x1, gsem.at[1])
        gather_wait(x0, gsem.at[0])
        gather_start(src_nx2, x2, gsem.at[2])
        experts(x0, y0)


def _moe(block_e, slot3, src3, h2, wg, wu, wd, layer):
    nb = slot3.shape[0]
    dh = h2.shape[1]
    d, de = wg.shape[1], wg.shape[2]
    expert = lambda i, be: (layer * N_EXPERTS + be[i], 0, 0)
    idx_spec = lambda f: pl.BlockSpec((1, 1, TB), f, memory_space=pltpu.SMEM)
    assert nb >= 3
    cur = lambda i, be: (i, 0, 0)
    nxt = lambda i, be: (jnp.minimum(i + 1, nb - 1), 0, 0)
    nx2 = lambda i, be: (jnp.minimum(i + 2, nb - 1), 0, 0)
    prv = lambda i, be: (jnp.maximum(i - 1, 0), 0, 0)
    vm = pltpu.VMEM((TB, dh), jnp.uint32)
    return pl.pallas_call(
        _moe_kernel,
        grid_spec=pltpu.PrefetchScalarGridSpec(
            num_scalar_prefetch=1,
            grid=(nb,),
            in_specs=[idx_spec(cur), idx_spec(nxt), idx_spec(nx2), idx_spec(prv), idx_spec(cur),
                      pl.BlockSpec(memory_space=pl.ANY),
                      pl.BlockSpec((1, d, de), expert),
                      pl.BlockSpec((1, d, de), expert),
                      pl.BlockSpec((1, de, d), expert)],
            out_specs=pl.BlockSpec(memory_space=pl.ANY),
            scratch_shapes=[vm] * 6 + [pltpu.SemaphoreType.DMA((3,)), pltpu.SemaphoreType.DMA((3,))],
        ),
        out_shape=jax.ShapeDtypeStruct((nb * TB, dh), jnp.uint32),
        compiler_params=_params(("arbitrary",)),
        name="moe",
    )(block_e, src3, src3, src3, slot3, slot3, h2, wg, wu, wd)


def _combine_kernel(y0_ref, y1_ref, x1_ref, route_ref, mod_ref, gf_ref, o_ref, *, final_norm):
    d = x1_ref.shape[1]
    y = route_ref[:, 2:3] * _unpack_halves(y0_ref[...]) + route_ref[:, 3:4] * _unpack_halves(y1_ref[...])
    x2 = x1_ref[...] + mod_ref[:, 5 * d:6 * d] * y
    if final_norm:
        x2 = x2 * lax.rsqrt(jnp.mean(x2 * x2, axis=-1, keepdims=True) + EPS) * gf_ref[...]
    o_ref[...] = x2


def _combine(y_rows, x1, route, mod3, gf, n_batch, final_norm):
    nt, d = x1.shape
    nj = nt // n_batch // TM
    ctx_row = n_batch

    def modrow(i):
        return (jnp.where(i % nj == 0, ctx_row, i // nj), 0, 0)

    if final_norm:
        out_rows = nt - n_batch * TM
        out_map = lambda i: ((i // nj) * (nj - 1) + jnp.maximum(i % nj - 1, 0), 0)
    else:
        out_rows = nt
        out_map = lambda i: (i, 0)
    return pl.pallas_call(
        functools.partial(_combine_kernel, final_norm=final_norm),
        grid=(nt // TM,),
        in_specs=[pl.BlockSpec((TM, d // 2), lambda i: (i, 0)),
                  pl.BlockSpec((TM, d // 2), lambda i: (nt // TM + i, 0)),
                  pl.BlockSpec((TM, d), lambda i: (i, 0)),
                  pl.BlockSpec((TM, LANES), lambda i: (i, 0)),
                  pl.BlockSpec((None, 1, mod3.shape[2]), modrow),
                  _const_spec(gf.shape)],
        out_specs=pl.BlockSpec((TM, d), out_map),
        out_shape=jax.ShapeDtypeStruct((out_rows, d), F32),
        compiler_params=_params(("arbitrary",)),
        name="combine",
    )(y_rows, y_rows, x1, route, mod3, gf)


def kernel(x, c, ctx, c_ctx, w_mod, b_mod, g_norm1, g_norm2, w_in, w_gate2, b_gate2, gla_norm_g,
           sgu_ln_g, sgu_ln_b, sgu_w, sgu_b, w_branch_a, w_branch_b, b_branch, w_out,
           w_router, b_router, w_exp_gate, w_exp_up, w_exp_down, g_final):
    n_batch, seq, d = x.shape
    ctx_len = ctx.shape[1]
    depth = w_mod.shape[0]
    dk = w_gate2.shape[3]
    rank_lr = w_gate2.shape[2]
    assert ctx_len == TM and seq % TM == 0 and n_batch < MOD_ROWS

    nt = n_batch * (ctx_len + seq)
    streams = (ctx, x, False)
    n_blocks = TOP_K * nt // TB + N_EXPERTS
    cc = jnp.zeros((MOD_ROWS, d), F32).at[:n_batch].set(c).at[n_batch].set(c_ctx)
    mod = _modulation(cc, w_mod, b_mod)

    wr = jnp.zeros((d, LANES), F32).at[:, :N_EXPERTS].set(w_router)
    wr_hi = wr.astype(BF16)
    wr = jnp.stack([wr_hi, (wr - wr_hi.astype(F32)).astype(BF16)])
    br = jnp.zeros((1, LANES), F32).at[0, :N_EXPERTS].set(b_router)
    row = lambda a: a.reshape(1, -1)
    n_main = 2 * d + 2 * dk + 2 * d
    all_layers = lambda w: w.astype(BF16).reshape(-1, w.shape[2], w.shape[3])
    wg_all, wu_all, wd_all = all_layers(w_exp_gate), all_layers(w_exp_up), all_layers(w_exp_down)

    out = None
    for l in range(depth):
        last = l == depth - 1
        mod3 = mod[l].reshape(MOD_ROWS, 1, 6 * d)
        wmain = w_in[l][:, :n_main].astype(BF16)
        wlr = w_in[l][:, n_main:n_main + 2 * rank_lr].astype(BF16)
        wgates = w_in[l][:, n_main + 2 * rank_lr:].astype(BF16)
        wg2 = jnp.zeros((2 * rank_lr, 2 * dk), F32)
        wg2 = wg2.at[:rank_lr, :dk].set(w_gate2[l, 0]).at[rank_lr:, dk:].set(w_gate2[l, 1]).astype(BF16)
        bg2 = b_gate2[l].reshape(1, 2 * dk)
        sgub = jnp.repeat(sgu_b[l].T, d // A_GROUPS, axis=1)

        yag, gb, q, k, v, sr, la = _inproj(
            *streams, nt, mod3, row(g_norm1[l]), wmain, wlr, wgates, wg2, bg2, row(sgu_ln_g[l]), row(sgu_ln_b[l]),
            sgu_w[l].astype(BF16), sgub, w_branch_a[l].astype(BF16), row(b_branch[l]), n_batch)
        ob = _gla_bwd(q, k, v, la, n_batch)
        x1, h2, route, route_t = _merge(
            *streams, nt, mod3, q, k, v, la, ob, yag, gb, sr, row(gla_norm_g[l]), w_branch_b[l].astype(BF16),
            w_out[l].astype(BF16), row(g_norm2[l]), wr, br, n_batch)
        slot_raw, block_e, cnt, pst = _plan(route_t, n_blocks)
        slot3, src3 = _sorted_rows(slot_raw, block_e, cnt, pst, nt, n_blocks)
        y_rows = _moe(block_e.reshape(-1), slot3, src3, h2, wg_all, wu_all, wd_all, l)
        res = _combine(y_rows, x1, route, mod3, row(g_final), n_batch, final_norm=last)
        if last:
            out = res.reshape(n_batch, seq, d)
        else:
            res = res.reshape(n_batch, -1, d)
            streams = (res, res, True)
    return out
```

```python
import functools

import jax
import jax.numpy as jnp
from jax import lax
from jax.experimental import pallas as pl
from jax.experimental.pallas import tpu as pltpu

F32 = jnp.float32
BF16 = jnp.bfloat16
HIGHEST = lax.Precision.HIGHEST

A_CHUNK = 128
A_GROUPS = 8
GLA_HEADS = 4
GLA_TAU = 16.0
GLA_CHUNK = 64
N_EXPERTS = 16
N_EXPERT_GROUPS = 4
EXPERTS_PER_GROUP = N_EXPERTS // N_EXPERT_GROUPS
TOP_K = 2
EPS = 1e-6

LANES = 128
SUBLANES = 8
TM = 256
TB = 256
BG = 2
MOD_ROWS = 16
VMEM_LIMIT = 56 * 1024 * 1024


def _dot(a, b):
    return jnp.dot(a, b, preferred_element_type=F32)


def _pack_halves(x):
    n = x.shape[1] // 2
    bits = lambda v: pltpu.bitcast(v.astype(BF16).astype(F32), jnp.uint32)
    return bits(x[:, :n]) | (bits(x[:, n:]) >> 16)


def _unpack_halves(p):
    hi = pltpu.bitcast(p & jnp.uint32(0xFFFF0000), F32)
    lo = pltpu.bitcast(p << 16, F32)
    return jnp.concatenate([hi, lo], axis=1)


def _const_spec(shape):
    nd = len(shape)
    return pl.BlockSpec(shape, lambda *_: (0,) * nd)


def _params(sem):
    return pltpu.CompilerParams(dimension_semantics=sem, vmem_limit_bytes=VMEM_LIMIT)


def _mod_kernel(cc_ref, w_ref, b_ref, o_ref):
    cc = cc_ref[...]
    s = cc * jax.nn.sigmoid(cc)
    o_ref[0] = jnp.dot(s, w_ref[0], preferred_element_type=F32, precision=HIGHEST) + b_ref[0]


def _modulation(cc, w_mod, b_mod):
    n_layer, d, six_d = w_mod.shape
    return pl.pallas_call(
        _mod_kernel,
        grid=(n_layer, six_d // d),
        in_specs=[
            pl.BlockSpec((MOD_ROWS, d), lambda l, j: (0, 0)),
            pl.BlockSpec((1, d, d), lambda l, j: (l, 0, j)),
            pl.BlockSpec((1, 1, d), lambda l, j: (l, 0, j)),
        ],
        out_specs=pl.BlockSpec((1, MOD_ROWS, d), lambda l, j: (l, 0, j)),
        out_shape=jax.ShapeDtypeStruct((n_layer, MOD_ROWS, six_d), F32),
        compiler_params=_params(("arbitrary", "arbitrary")),
        name="modulation",
    )(cc, w_mod, b_mod.reshape(n_layer, 1, six_d))


def _gelu_tanh(x):
    c = 0.7978845608028654
    return 0.5 * x * (1.0 + jnp.tanh(c * (x + 0.044715 * (x * x * x))))


def _log_sigmoid(z):
    return jnp.minimum(z, 0.0) - jnp.log1p(jnp.exp(-jnp.abs(z)))


def _inproj_kernel(xc_ref, xl_ref, modc_ref, modl_ref, g1_ref, wmain_ref, wlr_ref, wgates_ref, wg2_ref, bg2_ref,
                   lng_ref, lnb_ref, sguw_ref, sgub_ref, wa_ref, bbr_ref,
                   yag_ref, gb_ref, q_ref, k_ref, v_ref, sr_ref, la_ref, sa_ref):
    first = pl.program_id(1) == 0
    n_g, rows, d = xc_ref.shape
    dk = q_ref.shape[2]
    gdim = d // A_GROUPS
    head_k = dk // GLA_HEADS
    c_q, c_v = 2 * d, 2 * d + 2 * dk
    hb, u, vnb, ya = {}, {}, {}, {}

    def norm(g):
        mod = jnp.where(first, modc_ref[0], modl_ref[g])
        x = jnp.where(first, xc_ref[g], xl_ref[g])
        h = x * lax.rsqrt(jnp.mean(x * x, axis=-1, keepdims=True) + EPS) * g1_ref[...]
        hb[g] = (h * (1.0 + mod[:, d:2 * d]) + mod[:, 0:d]).astype(BF16)

    def gate_u(g):
        u[g] = _gelu_tanh(_dot(hb[g], wmain_ref[:, 0:d]))

    def gate_v(g):
        vv = _gelu_tanh(_dot(hb[g], wmain_ref[:, d:2 * d]))
        vc = vv - jnp.mean(vv, axis=-1, keepdims=True)
        vn = vc * lax.rsqrt(jnp.mean(vc * vc, axis=-1, keepdims=True) + EPS) * lng_ref[...] + lnb_ref[...]
        vnb[g] = vn.astype(BF16)

    def spatial(g):
        for n in range(rows // A_CHUNK):
            rs = slice(n * A_CHUNK, (n + 1) * A_CHUNK)
            for a in range(A_GROUPS):
                cs = slice(a * gdim, (a + 1) * gdim)
                mixed = _dot(sguw_ref[a], vnb[g][rs, cs]) + sgub_ref[:, cs]
                sa_ref[g, rs, cs] = (u[g][rs, cs] * mixed).astype(BF16)

    def proj_a(g):
        ya[g] = _dot(sa_ref[g], wa_ref[...])

    def branch_gates(g):
        gates = _dot(hb[g], wgates_ref[...]) + bbr_ref[...]
        yag_ref[g] = (jax.nn.sigmoid(gates[:, 0:d]) * ya[g]).astype(BF16)
        gb_ref[g] = jax.nn.sigmoid(gates[:, d:2 * d]).astype(BF16)

    def qk(g):
        q_ref[g] = (_dot(hb[g], wmain_ref[:, c_q:c_q + dk]) * (head_k ** -0.5)).astype(BF16)
        k_ref[g] = _dot(hb[g], wmain_ref[:, c_q + dk:c_q + 2 * dk]).astype(BF16)

    def val(g):
        v_ref[g] = _dot(hb[g], wmain_ref[:, c_v:c_v + d]).astype(BF16)

    def out_gate(g):
        r = _dot(hb[g], wmain_ref[:, c_v + d:c_v + 2 * d])
        sr_ref[g] = (r * jax.nn.sigmoid(r)).astype(BF16)

    def decay(g):
        lr = _dot(hb[g], wlr_ref[...]).astype(BF16)
        z = _dot(lr, wg2_ref[...]) + bg2_ref[...]
        la_ref[g] = _log_sigmoid(z) * (1.0 / GLA_TAU)

    order = [norm, gate_u, decay, gate_v, qk, spatial, out_gate, proj_a, branch_gates, val]
    for stage in order:
        for g in range(n_g):
            stage(g)


def _stream_specs(joined, ctx_row, six_d, d):
    lat_map = (lambda bp, j: (bp, jnp.maximum(j, 1), 0)) if joined else (lambda bp, j: (bp, jnp.maximum(j - 1, 0), 0))
    return [pl.BlockSpec((BG, TM, d), lambda bp, j: (bp, 0, 0)), pl.BlockSpec((BG, TM, d), lat_map),
            pl.BlockSpec((1, 1, six_d), lambda bp, j: (ctx_row, 0, 0)),
            pl.BlockSpec((BG, 1, six_d), lambda bp, j: (bp, 0, 0))]


def _resident_spec(shape):
    nd = len(shape)
    return pl.BlockSpec(shape, lambda *_: (0,) * nd, pipeline_mode=pl.Buffered(1))


def _inproj(xc, xl, joined, nt, mod3, g1, wmain, wlr, wgates, wg2, bg2, lng, lnb, sguw, sgub, wa, bbr, n_batch):
    d = xc.shape[2]
    dk = wg2.shape[1] // 2
    t_all = nt // n_batch
    nj = t_all // TM
    tile = lambda w: pl.BlockSpec((BG, TM, w), lambda bp, j: (bp, j, 0))
    outs = [(d, BF16), (d, BF16), (dk, BF16), (dk, BF16), (d, BF16), (d, BF16), (2 * dk, F32)]
    res = pl.pallas_call(
        _inproj_kernel,
        grid=(n_batch // BG, nj),
        in_specs=_stream_specs(joined, n_batch, mod3.shape[2], d)
        + [_resident_spec(a.shape) for a in (g1, wmain, wlr, wgates, wg2, bg2, lng, lnb, sguw, sgub, wa, bbr)],
        out_specs=[tile(w) for w, _ in outs],
        out_shape=[jax.ShapeDtypeStruct((n_batch, t_all, w), t) for w, t in outs],
        scratch_shapes=[pltpu.VMEM((BG, TM, d), BF16)],
        compiler_params=_params(("arbitrary", "arbitrary")),
        name="inproj",
    )(xc, xl, mod3, mod3, g1, wmain, wlr, wgates, wg2, bg2, lng, lnb, sguw, sgub, wa, bbr)
    return [a.reshape(nt, a.shape[2]) for a in res]


def _gla_tiles(q_refs, k_refs, v_refs, la_refs, s_refs, o_refs, qin_sc, kv_sc, dm_sc, reverse):
    n_g = len(q_refs)
    rows, dk = q_refs[0].shape
    dv = v_refs[0].shape[1]
    hk = dk // GLA_HEADS
    hv = dv // GLA_HEADS
    c = GLA_CHUNK
    ri = lax.broadcasted_iota(jnp.int32, (c, c), 0)
    ci = lax.broadcasted_iota(jnp.int32, (c, c), 1)
    tri = (ci >= ri) if reverse else (ci <= ri)
    rt = lax.broadcasted_iota(jnp.int32, (rows, rows), 0)
    ct = lax.broadcasted_iota(jnp.int32, (rows, rows), 1)
    in_chunk = (rt // c) == (ct // c)
    tri_t = jnp.where(jnp.logical_and(in_chunk, (ct >= rt) if reverse else (ct <= rt)), 1.0, 0.0).astype(BF16)
    chunks = list(range(rows // c))
    for g in range(n_g):
        la = la_refs[g][...]
        la_hi = la.astype(BF16)
        la_lo = (la - la_hi.astype(F32)).astype(BF16)
        b_all = _dot(tri_t, la_hi) + _dot(tri_t, la_lo)
        for n in chunks:
            rs = slice(n * c, (n + 1) * c)
            b = b_all[rs, :]
            b_end = b[0:1, :] if reverse else b[c - 1:c, :]
            q = q_refs[g][rs, :].astype(F32)
            k = k_refs[g][rs, :].astype(F32)
            q_in = (q * jnp.exp(b)).astype(BF16)
            k_in = (k * jnp.exp(-b)).astype(BF16)
            k_st = (k * jnp.exp(b_end - b)).astype(BF16)
            decay = jnp.exp(b_end)
            qin_sc[g, rs, :] = q_in
            for h in range(GLA_HEADS):
                ks = slice(h * hk, (h + 1) * hk)
                vs = slice(h * hv, (h + 1) * hv)
                vh = v_refs[g][rs, vs]
                att = lax.dot_general(q_in[:, ks], k_in[:, ks], (((1,), (1,)), ((), ())),
                                      preferred_element_type=F32)
                o_refs[g][rs, vs] = _dot(jnp.where(tri, att, 0.0).astype(BF16), vh)
                kv_sc[g, n, h] = lax.dot_general(k_st[:, ks], vh, (((0,), (0,)), ((), ())),
                                                 preferred_element_type=F32)
                dm_sc[g, n, h] = jnp.transpose(jnp.broadcast_to(decay[:, ks], (hk, hk)))
    for n in (reversed(chunks) if reverse else chunks):
        rs = slice(n * c, (n + 1) * c)
        for g in range(n_g):
            for h in range(GLA_HEADS):
                ks = slice(h * hk, (h + 1) * hk)
                vs = slice(h * hv, (h + 1) * hv)
                state = s_refs[g][h]
                o_refs[g][rs, vs] += _dot(qin_sc[g, rs, ks], state.astype(BF16))
                dmat = jnp.concatenate([dm_sc[g, n, h]] * (hv // hk), axis=1)
                s_refs[g][h] = dmat * state + kv_sc[g, n, h]


def _gla_scratch(n_g, rows, dk, dv):
    hk, hv, n_chunks = dk // GLA_HEADS, dv // GLA_HEADS, rows // GLA_CHUNK
    return [pltpu.VMEM((n_g, GLA_HEADS, hk, hv), F32),
            pltpu.VMEM((n_g, rows, dv), F32),
            pltpu.VMEM((n_g, rows, dk), BF16),
            pltpu.VMEM((n_g, n_chunks, GLA_HEADS, hk, hv), F32),
            pltpu.VMEM((n_g, n_chunks, GLA_HEADS, hk, hk), F32)]


def _per_batch(ref):
    return [ref.at[g] for g in range(ref.shape[0])]


def _gla_bwd_kernel(q_ref, k_ref, v_ref, la_ref, ob_ref, s_ref, o_sc, qin_sc, kv_sc, dm_sc):
    @pl.when(pl.program_id(1) == 0)
    def _():
        s_ref[...] = jnp.zeros_like(s_ref)

    _gla_tiles(_per_batch(q_ref), _per_batch(k_ref), _per_batch(v_ref), _per_batch(la_ref),
               _per_batch(s_ref), _per_batch(o_sc), qin_sc, kv_sc, dm_sc, reverse=True)
    ob_ref[...] = o_sc[...].astype(BF16)


def _gla_bwd(q, k, v, la, n_batch):
    nt, dk = q.shape
    dv = v.shape[1]
    t_all = nt // n_batch
    nj = t_all // TM
    per_batch = lambda a: a.reshape(n_batch, t_all, a.shape[1])

    def tile(bp, jj):
        return (bp, jnp.where(jj == 0, 0, nj - jj), 0)

    def tile_la(bp, jj):
        return (bp, jnp.where(jj == 0, 0, nj - jj), 1)

    ob = pl.pallas_call(
        _gla_bwd_kernel,
        grid=(n_batch // BG, nj),
        in_specs=[pl.BlockSpec((BG, TM, dk), tile), pl.BlockSpec((BG, TM, dk), tile),
                  pl.BlockSpec((BG, TM, dv), tile), pl.BlockSpec((BG, TM, dk), tile_la)],
        out_specs=pl.BlockSpec((BG, TM, dv), tile),
        out_shape=jax.ShapeDtypeStruct((n_batch, t_all, dv), BF16),
        scratch_shapes=_gla_scratch(BG, TM, dk, dv),
        compiler_params=_params(("arbitrary", "arbitrary")),
        name="gla_bwd",
    )(per_batch(q), per_batch(k), per_batch(v), per_batch(la))
    return ob.reshape(nt, dv)


def _route(logits_t):
    n_tok = logits_t.shape[1]
    eid = lax.broadcasted_iota(jnp.int32, logits_t.shape, 0)
    ex = jnp.exp(logits_t - jnp.max(logits_t, axis=0, keepdims=True))
    p = ex / jnp.sum(ex, axis=0, keepdims=True)
    grp = eid // EXPERTS_PER_GROUP
    none = -1.0
    far = 2 * N_EXPERTS
    best = None
    for g in range(N_EXPERT_GROUPS):
        pg = jnp.where(grp == g, p, none)
        m1 = jnp.max(pg, axis=0, keepdims=True)
        i1 = jnp.min(jnp.where(pg == m1, eid, far), axis=0, keepdims=True)
        pg2 = jnp.where(eid == i1, none, pg)
        m2 = jnp.max(pg2, axis=0, keepdims=True)
        i2 = jnp.min(jnp.where(pg2 == m2, eid, far), axis=0, keepdims=True)
        cand = (m1 + m2, m1, i1, m2, i2)
        if best is None:
            best = cand
        else:
            better = cand[0] > best[0]
            best = tuple(jnp.where(better, c, o) for c, o in zip(cand, best))
    _, m1, i1, m2, i2 = best
    tot = m1 + m2
    sub = lax.broadcasted_iota(jnp.int32, (SUBLANES, n_tok), 0)
    out = jnp.where(sub == 0, i1.astype(F32), 0.0)
    out = jnp.where(sub == 1, i2.astype(F32), out)
    out = jnp.where(sub == 2, m1 / tot, out)
    out = jnp.where(sub == 3, m2 / tot, out)
    return out


def _merge_kernel(xc_ref, xl_ref, modc_ref, modl_ref, q_ref, k_ref, v_ref, la_ref, ob_ref, yag_ref, gb_ref,
                  sr_ref, glag_ref, wb_ref, wout_ref, g2_ref, wr_ref, br_ref,
                  x1_ref, h2_ref, route_ref, routet_ref, s_ref, o_sc, qin_sc, kv_sc, dm_sc):
    first = pl.program_id(1) == 0

    @pl.when(first)
    def _():
        s_ref[...] = jnp.zeros_like(s_ref)

    _gla_tiles(_per_batch(q_ref), _per_batch(k_ref), _per_batch(v_ref), _per_batch(la_ref),
               _per_batch(s_ref), _per_batch(o_sc), qin_sc, kv_sc, dm_sc, reverse=False)

    d = xc_ref.shape[2]
    hv = d // GLA_HEADS
    mod, yb_in, mix, h2, logits = {}, {}, {}, {}, {}

    def head_norm(g):
        mod[g] = jnp.where(first, modc_ref[0], modl_ref[g])
        o = o_sc[g] + ob_ref[g].astype(F32)
        parts = []
        for h in range(GLA_HEADS):
            oh = o[:, h * hv:(h + 1) * hv]
            parts.append(oh * lax.rsqrt(jnp.mean(oh * oh, axis=-1, keepdims=True) + EPS))
        on = jnp.concatenate(parts, axis=1) * glag_ref[...]
        yb_in[g] = (on * sr_ref[g].astype(F32)).astype(BF16)

    def branch_merge(g):
        yb = _dot(yb_in[g], wb_ref[...])
        mix[g] = (yag_ref[g].astype(F32) + gb_ref[g].astype(F32) * yb).astype(BF16)

    def residual(g):
        y = _dot(mix[g], wout_ref[...])
        x1 = jnp.where(first, xc_ref[g], xl_ref[g]) + mod[g][:, 2 * d:3 * d] * y
        x1_ref[g] = x1
        h = x1 * lax.rsqrt(jnp.mean(x1 * x1, axis=-1, keepdims=True) + EPS) * g2_ref[...]
        h2[g] = h * (1.0 + mod[g][:, 4 * d:5 * d]) + mod[g][:, 3 * d:4 * d]
        h2_ref[g] = _pack_halves(h2[g])

    def router_logits(g):
        h_hi = h2[g].astype(BF16)
        h_lo = (h2[g] - h_hi.astype(F32)).astype(BF16)
        logits[g] = _dot(h_hi, wr_ref[0]) + _dot(h_lo, wr_ref[0]) + _dot(h_hi, wr_ref[1]) + br_ref[...]

    def routing(g):
        rt = _route(jnp.transpose(logits[g])[0:N_EXPERTS, :])
        routet_ref[g, 0] = rt
        sub = lax.broadcasted_iota(jnp.int32, (LANES, rt.shape[1]), 0)
        padded = jnp.zeros((LANES, rt.shape[1]), F32)
        for r in range(4):
            padded = jnp.where(sub == r, rt[r:r + 1, :], padded)
        route_ref[g] = jnp.transpose(padded)

    for stage in (head_norm, branch_merge, residual, router_logits, routing):
        for g in range(xc_ref.shape[0]):
            stage(g)


def _merge(xc, xl, joined, nt, mod3, q, k, v, la, ob, yag, gb, sr, glag, wb, wout, g2, wr, br, n_batch):
    d = xc.shape[2]
    dk = q.shape[1]
    t_all = nt // n_batch
    nj = t_all // TM
    per_batch = lambda a: a.reshape(n_batch, t_all, a.shape[1])
    tile = lambda w: pl.BlockSpec((BG, TM, w), lambda bp, j: (bp, j, 0))
    x1, h2, route, route_t = pl.pallas_call(
        _merge_kernel,
        grid=(n_batch // BG, nj),
        in_specs=_stream_specs(joined, n_batch, mod3.shape[2], d)
        + [tile(dk), tile(dk), tile(d), tile(dk), tile(d), tile(d), tile(d), tile(d)]
        + [_const_spec(a.shape) for a in (glag, wb, wout, g2, wr, br)],
        out_specs=[tile(d), tile(d // 2), tile(LANES),
                   pl.BlockSpec((BG, 1, SUBLANES, TM), lambda bp, j: (bp, j, 0, 0))],
        out_shape=[jax.ShapeDtypeStruct((n_batch, t_all, d), F32),
                   jax.ShapeDtypeStruct((n_batch, t_all, d // 2), jnp.uint32),
                   jax.ShapeDtypeStruct((n_batch, t_all, LANES), F32),
                   jax.ShapeDtypeStruct((n_batch, nj, SUBLANES, TM), F32)],
        scratch_shapes=_gla_scratch(BG, TM, dk, d),
        compiler_params=_params(("arbitrary", "arbitrary")),
        name="merge",
    )(xc, xl, mod3, mod3, per_batch(q), per_batch(k), per_batch(v), per_batch(la), per_batch(ob),
      per_batch(yag), per_batch(gb), per_batch(sr), glag, wb, wout, g2, wr, br)
    return (x1.reshape(nt, d), h2.reshape(nt, d // 2), route.reshape(nt, LANES),
            route_t.reshape(nt // TM, SUBLANES, TM))


def _plan_kernel(rt_ref, slot_ref, be_ref, cnt_ref, pst_ref):
    n_tiles, _, rows = rt_ref.shape
    n_tok = n_tiles * rows
    n_slot = TOP_K * rows
    sub = lax.broadcasted_iota(jnp.int32, (N_EXPERTS, rows), 0).astype(F32)
    wide = lambda col: jnp.broadcast_to(col, (N_EXPERTS, LANES))

    def one_hots(i):
        rt = rt_ref[i]
        return jnp.where(sub == rt[0:1, :], 1.0, 0.0), jnp.where(sub == rt[1:2, :], 1.0, 0.0)

    def count(i, acc):
        oh0, oh1 = one_hots(i)
        return acc + jnp.sum(oh0 + oh1, axis=1, keepdims=True)

    cnt = wide(lax.fori_loop(0, n_tiles, count, jnp.zeros((N_EXPERTS, 1), F32)))
    padded = jnp.floor((cnt + (TB - 1)) * (1.0 / TB)) * TB
    ri = lax.broadcasted_iota(jnp.int32, (N_EXPERTS, N_EXPERTS), 0)
    ci = lax.broadcasted_iota(jnp.int32, (N_EXPERTS, N_EXPERTS), 1)
    p_end = jnp.dot(jnp.where(ci <= ri, 1.0, 0.0), padded, preferred_element_type=F32, precision=HIGHEST)
    p_start = p_end - padded
    cnt_ref[...] = cnt
    pst_ref[...] = p_start
    starts = lax.broadcasted_iota(jnp.int32, (N_EXPERTS, be_ref.shape[1]), 1).astype(F32) * TB
    done = jnp.sum(jnp.where(p_end[:, 0:1] <= starts, 1.0, 0.0), axis=0, keepdims=True)
    be_ref[...] = jnp.minimum(done, N_EXPERTS - 1.0).astype(jnp.int32)
    slot_ref[...] = jnp.full(slot_ref.shape, -1, jnp.int32)

    rr = lax.broadcasted_iota(jnp.int32, (rows, rows), 0)
    cc = lax.broadcasted_iota(jnp.int32, (rows, rows), 1)
    earlier = jnp.where(rr < cc, 1.0, 0.0).astype(BF16)
    before = jnp.where(ci < ri, 1.0, 0.0)
    pos = lax.broadcasted_iota(jnp.int32, (n_slot, n_slot), 0).astype(F32)
    s_idx = lax.broadcasted_iota(jnp.int32, (SUBLANES, n_slot), 1)
    s_sub = lax.broadcasted_iota(jnp.int32, (SUBLANES, n_slot), 0)
    tok_slot = jnp.where(s_sub == 0, s_idx % rows, jnp.where(s_sub == 1, s_idx // rows, 0)).astype(BF16)
    lane = lax.broadcasted_iota(jnp.int32, (1, TB), 1)
    p_start_i = p_start.astype(jnp.int32)

    n_win = -(-(TB + n_slot) // TB)
    spare = slot_ref.shape[0] - 1
    empty = jnp.full((1, TB), -1, jnp.int32)

    def sort_tile(i):
        oh0, oh1 = one_hots(i)
        c0 = _dot(oh0.astype(BF16), earlier)
        c1 = _dot(oh1.astype(BF16), earlier)
        tot0 = jnp.sum(oh0, axis=1, keepdims=True)
        tot = tot0 + jnp.sum(oh1, axis=1, keepdims=True)
        off = jnp.dot(before, wide(tot), preferred_element_type=F32, precision=HIGHEST)[:, 0:1]
        lp = jnp.concatenate([jnp.sum(oh0 * (c0 + off), axis=0, keepdims=True),
                              jnp.sum(oh1 * (c1 + off + tot0), axis=0, keepdims=True)], axis=1)
        perm = jnp.where(pos == lp, 1.0, 0.0).astype(BF16)
        srt = lax.dot_general(tok_slot, perm, (((1,), (1,)), ((), ())), preferred_element_type=F32)
        ids = (srt[0:1, :] + srt[1:2, :] * n_tok).astype(jnp.int32) + i * rows
        ids = jnp.broadcast_to(jnp.concatenate([ids, jnp.zeros_like(ids)], axis=1), (SUBLANES, 2 * n_slot))
        return ids, tot, off

    def place(sorted_tile, carry):
        ids, tot, off = sorted_tile
        run, open_rows = carry
        tot_i, run_i, off_i = tot.astype(jnp.int32), run.astype(jnp.int32), off.astype(jnp.int32)
        new_rows = []
        for e in range(N_EXPERTS):
            n_e = tot_i[e, 0]
            row0 = p_start_i[e, 0] + run_i[e, 0]
            blk = row0 // TB
            at = row0 - blk * TB
            end = at + n_e
            moved = pltpu.roll(ids, at - off_i[e, 0] + 2 * n_slot, axis=1)
            keep = open_rows[e]
            for w in range(n_win):
                here = jnp.logical_and(lane + w * TB >= at, lane + w * TB < end)
                row = jnp.where(here, moved[0:1, w * TB:(w + 1) * TB], open_rows[e] if w == 0 else empty)
                slot_ref[pl.ds(blk if w == 0 else jnp.where(end > w * TB, blk + w, spare), 1), :] = row
                if w > 0:
                    keep = jnp.where(end >= w * TB, row, keep)
                else:
                    keep = row
            new_rows.append(keep)
        return run + tot, tuple(new_rows)

    def two_tiles(j, carry):
        first, second = sort_tile(2 * j), sort_tile(2 * j + 1)
        return place(second, place(first, carry))

    lax.fori_loop(0, n_tiles // 2, two_tiles, (jnp.zeros((N_EXPERTS, 1), F32), (empty,) * N_EXPERTS))


def _plan(route_t, n_blocks):
    nbp = -(-(n_blocks + TOP_K * TM // TB) // LANES) * LANES
    assert route_t.shape[0] % 2 == 0
    small = jax.ShapeDtypeStruct((N_EXPERTS, LANES), F32)
    return pl.pallas_call(
        _plan_kernel,
        out_shape=[jax.ShapeDtypeStruct((nbp, TB), jnp.int32),
                   jax.ShapeDtypeStruct((1, nbp), jnp.int32), small, small],
        compiler_params=pltpu.CompilerParams(vmem_limit_bytes=VMEM_LIMIT),
        name="plan",
    )(route_t)


def _sorted_rows(slot_raw, block_e, cnt, pst, n_tok, n_blocks):
    cnt = cnt[:, 0].astype(jnp.int32)
    pst = pst[:, 0].astype(jnp.int32)
    pad = (cnt + TB - 1) // TB * TB - cnt
    pad_before = jnp.cumsum(pad) - pad
    be = block_e[0, :n_blocks]
    base = TOP_K * n_tok - pst[be] - cnt[be] + pad_before[be]
    pad_id = base[:, None] + jnp.arange(n_blocks * TB, dtype=jnp.int32).reshape(n_blocks, TB)
    raw = slot_raw[:n_blocks]
    slot = jnp.where(raw < 0, pad_id, raw)
    src = jnp.where(raw < 0, 0, raw % n_tok)
    return slot.reshape(n_blocks, 1, TB), src.reshape(n_blocks, 1, TB)


def _moe_kernel(be_ref, src_cur, src_nxt, src_nx2, slot_prv, slot_cur, h_hbm, wg_ref, wu_ref, wd_ref, y_hbm,
                x0, x1, x2, y0, y1, y2, gsem, ssem):
    del be_ref
    i = pl.program_id(0)
    last = pl.num_programs(0) - 1
    xs, ys = (x0, x1, x2), (y0, y1, y2)

    def gather_start(idx_ref, dst, sem):
        for r in range(TB):
            pltpu.make_async_copy(h_hbm.at[pl.ds(idx_ref[0, 0, r], 1)], dst.at[pl.ds(r, 1)], sem).start(
                priority=r % 2)

    def gather_wait(dst, sem):
        pltpu.make_async_copy(h_hbm.at[pl.ds(0, TB)], dst, sem).wait()

    def scatter_start(idx_ref, src, sem):
        for r in range(TB):
            pltpu.make_async_copy(src.at[pl.ds(r, 1)], y_hbm.at[pl.ds(idx_ref[0, 0, r], 1)], sem).start(
                priority=r % 2)

    def scatter_wait(src, sem):
        pltpu.make_async_copy(src, y_hbm.at[pl.ds(0, TB)], sem).wait()

    def experts(x_ref, y_ref):
        x = _unpack_halves(x_ref[...]).astype(BF16)
        g = _dot(x, wg_ref[0])
        u = _dot(x, wu_ref[0])
        a = (g * jax.nn.sigmoid(g) * u).astype(BF16)
        y_ref[...] = _pack_halves(_dot(a, wd_ref[0]))

    for p in range(3):
        @pl.when(jnp.logical_and(i > 0, lax.rem(i, 3) == p))
        def _(p=p):
            prv, nx2 = (p + 2) % 3, (p + 2) % 3
            gather_wait(xs[p], gsem.at[p])

            @pl.when(i >= 3)
            def _():
                scatter_wait(ys[p], ssem.at[p])

            gather_start(src_nx2, xs[nx2], gsem.at[nx2])
            scatter_start(slot_prv, ys[prv], ssem.at[prv])
            experts(xs[p], ys[p])

    for p in range(3):
        @pl.when(jnp.logical_and(i == last, lax.rem(i, 3) == p))
        def _(p=p):
            nxt, prv = (p + 1) % 3, (p + 2) % 3
            scatter_start(slot_cur, ys[p], ssem.at[p])
            scatter_wait(ys[nxt], ssem.at[nxt])
            scatter_wait(ys[prv], ssem.at[prv])
            scatter_wait(ys[p], ssem.at[p])
            gather_wait(xs[nxt], gsem.at[nxt])
            gather_wait(xs[prv], gsem.at[prv])

    @pl.when(i == 0)
    def _():
        gather_start(src_cur, x0, gsem.at[0])
        gather_start(src_nxt, x1, gsem.at[1])
        gather_wait(x0, gsem.at[0])
        gather_start(src_nx2, x2, gsem.at[2])
        experts(x0, y0)


def _moe(block_e, slot3, src3, h2, wg, wu, wd, layer):
    nb = slot3.shape[0]
    dh = h2.shape[1]
    d, de = wg.shape[1], wg.shape[2]
    expert = lambda i, be: (layer * N_EXPERTS + be[i], 0, 0)
    idx_spec = lambda f: pl.BlockSpec((1, 1, TB), f, memory_space=pltpu.SMEM)
    assert nb >= 3
    cur = lambda i, be: (i, 0, 0)
    nxt = lambda i, be: (jnp.minimum(i + 1, nb - 1), 0, 0)
    nx2 = lambda i, be: (jnp.minimum(i + 2, nb - 1), 0, 0)
    prv = lambda i, be: (jnp.maximum(i - 1, 0), 0, 0)
    vm = pltpu.VMEM((TB, dh), jnp.uint32)
    return pl.pallas_call(
        _moe_kernel,
        grid_spec=pltpu.PrefetchScalarGridSpec(
            num_scalar_prefetch=1,
            grid=(nb,),
            in_specs=[idx_spec(cur), idx_spec(nxt), idx_spec(nx2), idx_spec(prv), idx_spec(cur),
                      pl.BlockSpec(memory_space=pl.ANY),
                      pl.BlockSpec((1, d, de), expert),
                      pl.BlockSpec((1, d, de), expert),
                      pl.BlockSpec((1, de, d), expert)],
            out_specs=pl.BlockSpec(memory_space=pl.ANY),
            scratch_shapes=[vm] * 6 + [pltpu.SemaphoreType.DMA((3,)), pltpu.SemaphoreType.DMA((3,))],
        ),
        out_shape=jax.ShapeDtypeStruct((nb * TB, dh), jnp.uint32),
        compiler_params=_params(("arbitrary",)),
        name="moe",
    )(block_e, src3, src3, src3, slot3, slot3, h2, wg, wu, wd)


def _combine_kernel(y0_ref, y1_ref, x1_ref, route_ref, mod_ref, gf_ref, o_ref, *, final_norm):
    d = x1_ref.shape[1]
    y = route_ref[:, 2:3] * _unpack_halves(y0_ref[...]) + route_ref[:, 3:4] * _unpack_halves(y1_ref[...])
    x2 = x1_ref[...] + mod_ref[:, 5 * d:6 * d] * y
    if final_norm:
        x2 = x2 * lax.rsqrt(jnp.mean(x2 * x2, axis=-1, keepdims=True) + EPS) * gf_ref[...]
    o_ref[...] = x2


def _combine(y_rows, x1, route, mod3, gf, n_batch, final_norm):
    nt, d = x1.shape
    nj = nt // n_batch // TM
    ctx_row = n_batch

    def modrow(i):
        return (jnp.where(i % nj == 0, ctx_row, i // nj), 0, 0)

    if final_norm:
        out_rows = nt - n_batch * TM
        out_map = lambda i: ((i // nj) * (nj - 1) + jnp.maximum(i % nj - 1, 0), 0)
    else:
        out_rows = nt
        out_map = lambda i: (i, 0)
    return pl.pallas_call(
        functools.partial(_combine_kernel, final_norm=final_norm),
        grid=(nt // TM,),
        in_specs=[pl.BlockSpec((TM, d // 2), lambda i: (i, 0)),
                  pl.BlockSpec((TM, d // 2), lambda i: (nt // TM + i, 0)),
                  pl.BlockSpec((TM, d), lambda i: (i, 0)),
                  pl.BlockSpec((TM, LANES), lambda i: (i, 0)),
                  pl.BlockSpec((None, 1, mod3.shape[2]), modrow),
                  _const_spec(gf.shape)],
        out_specs=pl.BlockSpec((TM, d), out_map),
        out_shape=jax.ShapeDtypeStruct((out_rows, d), F32),
        compiler_params=_params(("arbitrary",)),
        name="combine",
    )(y_rows, y_rows, x1, route, mod3, gf)


def kernel(x, c, ctx, c_ctx, w_mod, b_mod, g_norm1, g_norm2, w_in, w_gate2, b_gate2, gla_norm_g,
           sgu_ln_g, sgu_ln_b, sgu_w, sgu_b, w_branch_a, w_branch_b, b_branch, w_out,
           w_router, b_router, w_exp_gate, w_exp_up, w_exp_down, g_final):
    n_batch, seq, d = x.shape
    ctx_len = ctx.shape[1]
    depth = w_mod.shape[0]
    dk = w_gate2.shape[3]
    rank_lr = w_gate2.shape[2]
    assert ctx_len == TM and seq % TM == 0 and n_batch < MOD_ROWS

    nt = n_batch * (ctx_len + seq)
    streams = (ctx, x, False)
    n_blocks = TOP_K * nt // TB + N_EXPERTS
    cc = jnp.zeros((MOD_ROWS, d), F32).at[:n_batch].set(c).at[n_batch].set(c_ctx)
    mod = _modulation(cc, w_mod, b_mod)

    wr = jnp.zeros((d, LANES), F32).at[:, :N_EXPERTS].set(w_router)
    wr_hi = wr.astype(BF16)
    wr = jnp.stack([wr_hi, (wr - wr_hi.astype(F32)).astype(BF16)])
    br = jnp.zeros((1, LANES), F32).at[0, :N_EXPERTS].set(b_router)
    row = lambda a: a.reshape(1, -1)
    n_main = 2 * d + 2 * dk + 2 * d
    all_layers = lambda w: w.astype(BF16).reshape(-1, w.shape[2], w.shape[3])
    wg_all, wu_all, wd_all = all_layers(w_exp_gate), all_layers(w_exp_up), all_layers(w_exp_down)

    out = None
    for l in range(depth):
        last = l == depth - 1
        mod3 = mod[l].reshape(MOD_ROWS, 1, 6 * d)
        wmain = w_in[l][:, :n_main].astype(BF16)
        wlr = w_in[l][:, n_main:n_main + 2 * rank_lr].astype(BF16)
        wgates = w_in[l][:, n_main + 2 * rank_lr:].astype(BF16)
        wg2 = jnp.zeros((2 * rank_lr, 2 * dk), F32)
        wg2 = wg2.at[:rank_lr, :dk].set(w_gate2[l, 0]).at[rank_lr:, dk:].set(w_gate2[l, 1]).astype(BF16)
        bg2 = b_gate2[l].reshape(1, 2 * dk)
        sgub = jnp.repeat(sgu_b[l].T, d // A_GROUPS, axis=1)

        yag, gb, q, k, v, sr, la = _inproj(
            *streams, nt, mod3, row(g_norm1[l]), wmain, wlr, wgates, wg2, bg2, row(sgu_ln_g[l]), row(sgu_ln_b[l]),
            sgu_w[l].astype(BF16), sgub, w_branch_a[l].astype(BF16), row(b_branch[l]), n_batch)
        ob = _gla_bwd(q, k, v, la, n_batch)
        x1, h2, route, route_t = _merge(
            *streams, nt, mod3, q, k, v, la, ob, yag, gb, sr, row(gla_norm_g[l]), w_branch_b[l].astype(BF16),
            w_out[l].astype(BF16), row(g_norm2[l]), wr, br, n_batch)
        slot_raw, block_e, cnt, pst = _plan(route_t, n_blocks)
        slot3, src3 = _sorted_rows(slot_raw, block_e, cnt, pst, nt, n_blocks)
        y_rows = _moe(block_e.reshape(-1), slot3, src3, h2, wg_all, wu_all, wd_all, l)
        res = _combine(y_rows, x1, route, mod3, row(g_final), n_batch, final_norm=last)
        if last:
            out = res.reshape(n_batch, seq, d)
        else:
            res = res.reshape(n_batch, -1, d)
            streams = (res, res, True)
    return out
```

```python
import functools

import jax
import jax.numpy as jnp
from jax import lax
from jax.experimental import pallas as pl
from jax.experimental.pallas import tpu as pltpu

F32 = jnp.float32
BF16 = jnp.bfloat16
HIGHEST = lax.Precision.HIGHEST

A_CHUNK = 128
A_GROUPS = 8
GLA_HEADS = 4
GLA_TAU = 16.0
GLA_CHUNK = 64
N_EXPERTS = 16
N_EXPERT_GROUPS = 4
EXPERTS_PER_GROUP = N_EXPERTS // N_EXPERT_GROUPS
TOP_K = 2
EPS = 1e-6

LANES = 128
SUBLANES = 8
TM = 256
TB = 256
BG = 2
MOD_ROWS = 16
VMEM_LIMIT = 56 * 1024 * 1024


def _dot(a, b):
    return jnp.dot(a, b, preferred_element_type=F32)


def _pack_halves(x):
    n = x.shape[1] // 2
    bits = lambda v: pltpu.bitcast(v.astype(BF16).astype(F32), jnp.uint32)
    return bits(x[:, :n]) | (bits(x[:, n:]) >> 16)


def _unpack_halves(p):
    hi = pltpu.bitcast(p & jnp.uint32(0xFFFF0000), F32)
    lo = pltpu.bitcast(p << 16, F32)
    return jnp.concatenate([hi, lo], axis=1)


def _const_spec(shape):
    nd = len(shape)
    return pl.BlockSpec(shape, lambda *_: (0,) * nd)


def _params(sem):
    return pltpu.CompilerParams(dimension_semantics=sem, vmem_limit_bytes=VMEM_LIMIT)


def _mod_kernel(cc_ref, w_ref, b_ref, o_ref):
    cc = cc_ref[...]
    s = cc * jax.nn.sigmoid(cc)
    o_ref[0] = jnp.dot(s, w_ref[0], preferred_element_type=F32, precision=HIGHEST) + b_ref[0]


def _modulation(cc, w_mod, b_mod):
    n_layer, d, six_d = w_mod.shape
    return pl.pallas_call(
        _mod_kernel,
        grid=(n_layer, six_d // d),
        in_specs=[
            pl.BlockSpec((MOD_ROWS, d), lambda l, j: (0, 0)),
            pl.BlockSpec((1, d, d), lambda l, j: (l, 0, j)),
            pl.BlockSpec((1, 1, d), lambda l, j: (l, 0, j)),
        ],
        out_specs=pl.BlockSpec((1, MOD_ROWS, d), lambda l, j: (l, 0, j)),
        out_shape=jax.ShapeDtypeStruct((n_layer, MOD_ROWS, six_d), F32),
        compiler_params=_params(("arbitrary", "arbitrary")),
        name="modulation",
    )(cc, w_mod, b_mod.reshape(n_layer, 1, six_d))


def _gelu_tanh(x):
    c = 0.7978845608028654
    return 0.5 * x * (1.0 + jnp.tanh(c * (x + 0.044715 * (x * x * x))))


def _log_sigmoid(z):
    return jnp.minimum(z, 0.0) - jnp.log1p(jnp.exp(-jnp.abs(z)))


def _inproj_stages(x_of, modc_ref, modl_ref, g1_ref, wmain_ref, wlr_ref, wgates_ref, wg2_ref, bg2_ref,
                   lng_ref, lnb_ref, sguw_ref, sgub_ref, wa_ref, bbr_ref,
                   yag_ref, gb_ref, q_ref, k_ref, v_ref, sr_ref, la_ref, sa_ref):
    first = pl.program_id(1) == 0
    n_g, rows, d = yag_ref.shape
    dk = q_ref.shape[2]
    gdim = d // A_GROUPS
    head_k = dk // GLA_HEADS
    c_q, c_v = 2 * d, 2 * d + 2 * dk
    hb, u, vnb, ya = {}, {}, {}, {}

    def norm(g):
        mod = jnp.where(first, modc_ref[0], modl_ref[g])
        x = x_of(g)
        h = x * lax.rsqrt(jnp.mean(x * x, axis=-1, keepdims=True) + EPS) * g1_ref[...]
        hb[g] = (h * (1.0 + mod[:, d:2 * d]) + mod[:, 0:d]).astype(BF16)

    def gate_u(g):
        u[g] = _gelu_tanh(_dot(hb[g], wmain_ref[:, 0:d]))

    def gate_v(g):
        vv = _gelu_tanh(_dot(hb[g], wmain_ref[:, d:2 * d]))
        vc = vv - jnp.mean(vv, axis=-1, keepdims=True)
        vn = vc * lax.rsqrt(jnp.mean(vc * vc, axis=-1, keepdims=True) + EPS) * lng_ref[...] + lnb_ref[...]
        vnb[g] = vn.astype(BF16)

    def spatial(g):
        for n in range(rows // A_CHUNK):
            rs = slice(n * A_CHUNK, (n + 1) * A_CHUNK)
            for a in range(A_GROUPS):
                cs = slice(a * gdim, (a + 1) * gdim)
                mixed = _dot(sguw_ref[a], vnb[g][rs, cs]) + sgub_ref[:, cs]
                sa_ref[g, rs, cs] = (u[g][rs, cs] * mixed).astype(BF16)

    def proj_a(g):
        ya[g] = _dot(sa_ref[g], wa_ref[...])

    def branch_gates(g):
        gates = _dot(hb[g], wgates_ref[...]) + bbr_ref[...]
        yag_ref[g] = (jax.nn.sigmoid(gates[:, 0:d]) * ya[g]).astype(BF16)
        gb_ref[g] = jax.nn.sigmoid(gates[:, d:2 * d]).astype(BF16)

    def qk(g):
        q_ref[g] = (_dot(hb[g], wmain_ref[:, c_q:c_q + dk]) * (head_k ** -0.5)).astype(BF16)
        k_ref[g] = _dot(hb[g], wmain_ref[:, c_q + dk:c_q + 2 * dk]).astype(BF16)

    def val(g):
        v_ref[g] = _dot(hb[g], wmain_ref[:, c_v:c_v + d]).astype(BF16)

    def out_gate(g):
        r = _dot(hb[g], wmain_ref[:, c_v + d:c_v + 2 * d])
        sr_ref[g] = (r * jax.nn.sigmoid(r)).astype(BF16)

    def decay(g):
        lr = _dot(hb[g], wlr_ref[...]).astype(BF16)
        z = _dot(lr, wg2_ref[...]) + bg2_ref[...]
        la_ref[g] = _log_sigmoid(z) * (1.0 / GLA_TAU)

    order = [norm, gate_u, decay, gate_v, qk, spatial, out_gate, proj_a, branch_gates, val]
    for stage in order:
        for g in range(n_g):
            stage(g)


def _inproj_kernel(xc_ref, xl_ref, *rest):
    first = pl.program_id(1) == 0
    _inproj_stages(lambda g: jnp.where(first, xc_ref[g], xl_ref[g]), *rest)


def _inproj_after_moe_kernel(x1_ref, y00_ref, y01_ref, y10_ref, y11_ref, route_ref, pmodc_ref, pmodl_ref, *rest):
    *rest, x2_ref, sa_ref = rest
    first = pl.program_id(1) == 0
    d = x1_ref.shape[2]
    y_refs = ((y00_ref, y10_ref), (y01_ref, y11_ref))

    def x_of(g):
        pmod = jnp.where(first, pmodc_ref[0], pmodl_ref[g])
        y = (route_ref[g][:, 2:3] * _unpack_halves(y_refs[g][0][...])
             + route_ref[g][:, 3:4] * _unpack_halves(y_refs[g][1][...]))
        x2 = x1_ref[g] + pmod[:, 5 * d:6 * d] * y
        x2_ref[g] = x2
        return x2

    _inproj_stages(x_of, *rest, sa_ref)


def _stream_specs(joined, ctx_row, six_d, d):
    lat_map = (lambda bp, j: (bp, jnp.maximum(j, 1), 0)) if joined else (lambda bp, j: (bp, jnp.maximum(j - 1, 0), 0))
    return [pl.BlockSpec((BG, TM, d), lambda bp, j: (bp, 0, 0)), pl.BlockSpec((BG, TM, d), lat_map),
            pl.BlockSpec((1, 1, six_d), lambda bp, j: (ctx_row, 0, 0)),
            pl.BlockSpec((BG, 1, six_d), lambda bp, j: (bp, 0, 0))]


def _resident_spec(shape):
    nd = len(shape)
    return pl.BlockSpec(shape, lambda *_: (0,) * nd, pipeline_mode=pl.Buffered(1))


def _inproj(xc, xl, pending, nt, mod3, g1, wmain, wlr, wgates, wg2, bg2, lng, lnb, sguw, sgub, wa, bbr, n_batch):
    assert BG == 2
    dk = wg2.shape[1] // 2
    t_all = nt // n_batch
    nj = t_all // TM
    six_d = mod3.shape[2]
    d = six_d // 6
    tile = lambda w: pl.BlockSpec((BG, TM, w), lambda bp, j: (bp, j, 0))
    outs = [(d, BF16), (d, BF16), (dk, BF16), (dk, BF16), (d, BF16), (d, BF16), (2 * dk, F32)]
    weights = (g1, wmain, wlr, wgates, wg2, bg2, lng, lnb, sguw, sgub, wa, bbr)
    mod_specs = lambda: [pl.BlockSpec((1, 1, six_d), lambda bp, j: (n_batch, 0, 0)),
                         pl.BlockSpec((BG, 1, six_d), lambda bp, j: (bp, 0, 0))]
    if pending is None:
        kern = _inproj_kernel
        operands = (xc, xl, mod3, mod3)
        in_specs = _stream_specs(False, n_batch, six_d, d)
    else:
        x1, y_rows, route, mod3_prev = pending
        kern = _inproj_after_moe_kernel
        outs = outs + [(d, F32)]
        y_spec = lambda slot, g: pl.BlockSpec(
            (TM, d // 2), lambda bp, j: (slot * (nt // TM) + (bp * BG + g) * nj + j, 0))
        operands = (x1.reshape(n_batch, t_all, d), y_rows, y_rows, y_rows, y_rows,
                    route.reshape(n_batch, t_all, LANES), mod3_prev, mod3_prev, mod3, mod3)
        in_specs = ([tile(d), y_spec(0, 0), y_spec(0, 1), y_spec(1, 0), y_spec(1, 1), tile(LANES)]
                    + mod_specs() + mod_specs())
    res = pl.pallas_call(
        kern,
        grid=(n_batch // BG, nj),
        in_specs=in_specs + [_resident_spec(a.shape) for a in weights],
        out_specs=[tile(w) for w, _ in outs],
        out_shape=[jax.ShapeDtypeStruct((n_batch, t_all, w), t) for w, t in outs],
        scratch_shapes=[pltpu.VMEM((BG, TM, d), BF16)],
        compiler_params=_params(("arbitrary", "arbitrary")),
        name="inproj",
    )(*operands, *weights)
    return [a.reshape(nt, a.shape[2]) for a in res]


def _gla_tiles(q_refs, k_refs, v_refs, la_refs, s_refs, o_refs, qin_sc, kv_sc, dm_sc, reverse):
    n_g = len(q_refs)
    rows, dk = q_refs[0].shape
    dv = v_refs[0].shape[1]
    hk = dk // GLA_HEADS
    hv = dv // GLA_HEADS
    c = GLA_CHUNK
    ri = lax.broadcasted_iota(jnp.int32, (c, c), 0)
    ci = lax.broadcasted_iota(jnp.int32, (c, c), 1)
    tri = (ci >= ri) if reverse else (ci <= ri)
    rt = lax.broadcasted_iota(jnp.int32, (rows, rows), 0)
    ct = lax.broadcasted_iota(jnp.int32, (rows, rows), 1)
    in_chunk = (rt // c) == (ct // c)
    tri_t = jnp.where(jnp.logical_and(in_chunk, (ct >= rt) if reverse else (ct <= rt)), 1.0, 0.0).astype(BF16)
    chunks = list(range(rows // c))
    for g in range(n_g):
        la = la_refs[g][...]
        la_hi = la.astype(BF16)
        la_lo = (la - la_hi.astype(F32)).astype(BF16)
        b_all = _dot(tri_t, la_hi) + _dot(tri_t, la_lo)
        for n in chunks:
            rs = slice(n * c, (n + 1) * c)
            b = b_all[rs, :]
            b_end = b[0:1, :] if reverse else b[c - 1:c, :]
            q = q_refs[g][rs, :].astype(F32)
            k = k_refs[g][rs, :].astype(F32)
            q_in = (q * jnp.exp(b)).astype(BF16)
            k_in = (k * jnp.exp(-b)).astype(BF16)
            k_st = (k * jnp.exp(b_end - b)).astype(BF16)
            decay = jnp.exp(b_end)
            qin_sc[g, rs, :] = q_in
            for h in range(GLA_HEADS):
                ks = slice(h * hk, (h + 1) * hk)
                vs = slice(h * hv, (h + 1) * hv)
                vh = v_refs[g][rs, vs]
                att = lax.dot_general(q_in[:, ks], k_in[:, ks], (((1,), (1,)), ((), ())),
                                      preferred_element_type=F32)
                o_refs[g][rs, vs] = _dot(jnp.where(tri, att, 0.0).astype(BF16), vh)
                kv_sc[g, n, h] = lax.dot_general(k_st[:, ks], vh, (((0,), (0,)), ((), ())),
                                                 preferred_element_type=F32)
                dm_sc[g, n, h] = jnp.transpose(jnp.broadcast_to(decay[:, ks], (hk, hk)))
    for n in (reversed(chunks) if reverse else chunks):
        rs = slice(n * c, (n + 1) * c)
        for g in range(n_g):
            for h in range(GLA_HEADS):
                ks = slice(h * hk, (h + 1) * hk)
                vs = slice(h * hv, (h + 1) * hv)
                state = s_refs[g][h]
                o_refs[g][rs, vs] += _dot(qin_sc[g, rs, ks], state.astype(BF16))
                dmat = jnp.concatenate([dm_sc[g, n, h]] * (hv // hk), axis=1)
                s_refs[g][h] = dmat * state + kv_sc[g, n, h]


def _gla_scratch(n_g, rows, dk, dv):
    hk, hv, n_chunks = dk // GLA_HEADS, dv // GLA_HEADS, rows // GLA_CHUNK
    return [pltpu.VMEM((n_g, GLA_HEADS, hk, hv), F32),
            pltpu.VMEM((n_g, rows, dv), F32),
            pltpu.VMEM((n_g, rows, dk), BF16),
            pltpu.VMEM((n_g, n_chunks, GLA_HEADS, hk, hv), F32),
            pltpu.VMEM((n_g, n_chunks, GLA_HEADS, hk, hk), F32)]


def _per_batch(ref):
    return [ref.at[g] for g in range(ref.shape[0])]


def _gla_bwd_kernel(q_ref, k_ref, v_ref, la_ref, ob_ref, s_ref, o_sc, qin_sc, kv_sc, dm_sc):
    @pl.when(pl.program_id(1) == 0)
    def _():
        s_ref[...] = jnp.zeros_like(s_ref)

    _gla_tiles(_per_batch(q_ref), _per_batch(k_ref), _per_batch(v_ref), _per_batch(la_ref),
               _per_batch(s_ref), _per_batch(o_sc), qin_sc, kv_sc, dm_sc, reverse=True)
    ob_ref[...] = o_sc[...].astype(BF16)


def _gla_bwd(q, k, v, la, n_batch):
    nt, dk = q.shape
    dv = v.shape[1]
    t_all = nt // n_batch
    nj = t_all // TM
    per_batch = lambda a: a.reshape(n_batch, t_all, a.shape[1])

    def tile(bp, jj):
        return (bp, jnp.where(jj == 0, 0, nj - jj), 0)

    def tile_la(bp, jj):
        return (bp, jnp.where(jj == 0, 0, nj - jj), 1)

    ob = pl.pallas_call(
        _gla_bwd_kernel,
        grid=(n_batch // BG, nj),
        in_specs=[pl.BlockSpec((BG, TM, dk), tile), pl.BlockSpec((BG, TM, dk), tile),
                  pl.BlockSpec((BG, TM, dv), tile), pl.BlockSpec((BG, TM, dk), tile_la)],
        out_specs=pl.BlockSpec((BG, TM, dv), tile),
        out_shape=jax.ShapeDtypeStruct((n_batch, t_all, dv), BF16),
        scratch_shapes=_gla_scratch(BG, TM, dk, dv),
        compiler_params=_params(("arbitrary", "arbitrary")),
        name="gla_bwd",
    )(per_batch(q), per_batch(k), per_batch(v), per_batch(la))
    return ob.reshape(nt, dv)


def _route(logits_t):
    n_tok = logits_t.shape[1]
    eid = lax.broadcasted_iota(jnp.int32, logits_t.shape, 0)
    ex = jnp.exp(logits_t - jnp.max(logits_t, axis=0, keepdims=True))
    p = ex / jnp.sum(ex, axis=0, keepdims=True)
    grp = eid // EXPERTS_PER_GROUP
    none = -1.0
    far = 2 * N_EXPERTS
    best = None
    for g in range(N_EXPERT_GROUPS):
        pg = jnp.where(grp == g, p, none)
        m1 = jnp.max(pg, axis=0, keepdims=True)
        i1 = jnp.min(jnp.where(pg == m1, eid, far), axis=0, keepdims=True)
        pg2 = jnp.where(eid == i1, none, pg)
        m2 = jnp.max(pg2, axis=0, keepdims=True)
        i2 = jnp.min(jnp.where(pg2 == m2, eid, far), axis=0, keepdims=True)
        cand = (m1 + m2, m1, i1, m2, i2)
        if best is None:
            best = cand
        else:
            better = cand[0] > best[0]
            best = tuple(jnp.where(better, c, o) for c, o in zip(cand, best))
    _, m1, i1, m2, i2 = best
    tot = m1 + m2
    sub = lax.broadcasted_iota(jnp.int32, (SUBLANES, n_tok), 0)
    out = jnp.where(sub == 0, i1.astype(F32), 0.0)
    out = jnp.where(sub == 1, i2.astype(F32), out)
    out = jnp.where(sub == 2, m1 / tot, out)
    out = jnp.where(sub == 3, m2 / tot, out)
    return out


def _merge_kernel(xc_ref, xl_ref, modc_ref, modl_ref, q_ref, k_ref, v_ref, la_ref, ob_ref, yag_ref, gb_ref,
                  sr_ref, glag_ref, wb_ref, wout_ref, g2_ref, wr_ref, br_ref,
                  x1_ref, h2_ref, route_ref, routet_ref, s_ref, o_sc, qin_sc, kv_sc, dm_sc):
    first = pl.program_id(1) == 0

    @pl.when(first)
    def _():
        s_ref[...] = jnp.zeros_like(s_ref)

    _gla_tiles(_per_batch(q_ref), _per_batch(k_ref), _per_batch(v_ref), _per_batch(la_ref),
               _per_batch(s_ref), _per_batch(o_sc), qin_sc, kv_sc, dm_sc, reverse=False)

    d = xc_ref.shape[2]
    hv = d // GLA_HEADS
    mod, yb_in, mix, h2, logits = {}, {}, {}, {}, {}

    def head_norm(g):
        mod[g] = jnp.where(first, modc_ref[0], modl_ref[g])
        o = o_sc[g] + ob_ref[g].astype(F32)
        parts = []
        for h in range(GLA_HEADS):
            oh = o[:, h * hv:(h + 1) * hv]
            parts.append(oh * lax.rsqrt(jnp.mean(oh * oh, axis=-1, keepdims=True) + EPS))
        on = jnp.concatenate(parts, axis=1) * glag_ref[...]
        yb_in[g] = (on * sr_ref[g].astype(F32)).astype(BF16)

    def branch_merge(g):
        yb = _dot(yb_in[g], wb_ref[...])
        mix[g] = (yag_ref[g].astype(F32) + gb_ref[g].astype(F32) * yb).astype(BF16)

    def residual(g):
        y = _dot(mix[g], wout_ref[...])
        x1 = jnp.where(first, xc_ref[g], xl_ref[g]) + mod[g][:, 2 * d:3 * d] * y
        x1_ref[g] = x1
        h = x1 * lax.rsqrt(jnp.mean(x1 * x1, axis=-1, keepdims=True) + EPS) * g2_ref[...]
        h2[g] = h * (1.0 + mod[g][:, 4 * d:5 * d]) + mod[g][:, 3 * d:4 * d]
        h2_ref[g] = _pack_halves(h2[g])

    def router_logits(g):
        h_hi = h2[g].astype(BF16)
        h_lo = (h2[g] - h_hi.astype(F32)).astype(BF16)
        logits[g] = _dot(h_hi, wr_ref[0]) + _dot(h_lo, wr_ref[0]) + _dot(h_hi, wr_ref[1]) + br_ref[...]

    def routing(g):
        rt = _route(jnp.transpose(logits[g])[0:N_EXPERTS, :])
        routet_ref[g, 0] = rt
        sub = lax.broadcasted_iota(jnp.int32, (LANES, rt.shape[1]), 0)
        padded = jnp.zeros((LANES, rt.shape[1]), F32)
        for r in range(4):
            padded = jnp.where(sub == r, rt[r:r + 1, :], padded)
        route_ref[g] = jnp.transpose(padded)

    for stage in (head_norm, branch_merge, residual, router_logits, routing):
        for g in range(xc_ref.shape[0]):
            stage(g)


def _merge(xc, xl, joined, nt, mod3, q, k, v, la, ob, yag, gb, sr, glag, wb, wout, g2, wr, br, n_batch):
    d = xc.shape[2]
    dk = q.shape[1]
    t_all = nt // n_batch
    nj = t_all // TM
    per_batch = lambda a: a.reshape(n_batch, t_all, a.shape[1])
    tile = lambda w: pl.BlockSpec((BG, TM, w), lambda bp, j: (bp, j, 0))
    x1, h2, route, route_t = pl.pallas_call(
        _merge_kernel,
        grid=(n_batch // BG, nj),
        in_specs=_stream_specs(joined, n_batch, mod3.shape[2], d)
        + [tile(dk), tile(dk), tile(d), tile(dk), tile(d), tile(d), tile(d), tile(d)]
        + [_const_spec(a.shape) for a in (glag, wb, wout, g2, wr, br)],
        out_specs=[tile(d), tile(d // 2), tile(LANES),
                   pl.BlockSpec((BG, 1, SUBLANES, TM), lambda bp, j: (bp, j, 0, 0))],
        out_shape=[jax.ShapeDtypeStruct((n_batch, t_all, d), F32),
                   jax.ShapeDtypeStruct((n_batch, t_all, d // 2), jnp.uint32),
                   jax.ShapeDtypeStruct((n_batch, t_all, LANES), F32),
                   jax.ShapeDtypeStruct((n_batch, nj, SUBLANES, TM), F32)],
        scratch_shapes=_gla_scratch(BG, TM, dk, d),
        compiler_params=_params(("arbitrary", "arbitrary")),
        name="merge",
    )(xc, xl, mod3, mod3, per_batch(q), per_batch(k), per_batch(v), per_batch(la), per_batch(ob),
      per_batch(yag), per_batch(gb), per_batch(sr), glag, wb, wout, g2, wr, br)
    return (x1.reshape(nt, d), h2.reshape(nt, d // 2), route.reshape(nt, LANES),
            route_t.reshape(nt // TM, SUBLANES, TM))


def _plan_kernel(rt_ref, slot_ref, be_ref, cnt_ref, pst_ref):
    n_tiles, _, rows = rt_ref.shape
    n_tok = n_tiles * rows
    n_slot = TOP_K * rows
    sub = lax.broadcasted_iota(jnp.int32, (N_EXPERTS, rows), 0).astype(F32)
    wide = lambda col: jnp.broadcast_to(col, (N_EXPERTS, LANES))

    def one_hots(i):
        rt = rt_ref[i]
        return jnp.where(sub == rt[0:1, :], 1.0, 0.0), jnp.where(sub == rt[1:2, :], 1.0, 0.0)

    def count(i, acc):
        oh0, oh1 = one_hots(i)
        return acc + jnp.sum(oh0 + oh1, axis=1, keepdims=True)

    cnt = wide(lax.fori_loop(0, n_tiles, count, jnp.zeros((N_EXPERTS, 1), F32)))
    padded = jnp.floor((cnt + (TB - 1)) * (1.0 / TB)) * TB
    ri = lax.broadcasted_iota(jnp.int32, (N_EXPERTS, N_EXPERTS), 0)
    ci = lax.broadcasted_iota(jnp.int32, (N_EXPERTS, N_EXPERTS), 1)
    p_end = jnp.dot(jnp.where(ci <= ri, 1.0, 0.0), padded, preferred_element_type=F32, precision=HIGHEST)
    p_start = p_end - padded
    cnt_ref[...] = cnt
    pst_ref[...] = p_start
    starts = lax.broadcasted_iota(jnp.int32, (N_EXPERTS, be_ref.shape[1]), 1).astype(F32) * TB
    done = jnp.sum(jnp.where(p_end[:, 0:1] <= starts, 1.0, 0.0), axis=0, keepdims=True)
    be_ref[...] = jnp.minimum(done, N_EXPERTS - 1.0).astype(jnp.int32)
    slot_ref[...] = jnp.full(slot_ref.shape, -1, jnp.int32)

    rr = lax.broadcasted_iota(jnp.int32, (rows, rows), 0)
    cc = lax.broadcasted_iota(jnp.int32, (rows, rows), 1)
    earlier = jnp.where(rr < cc, 1.0, 0.0).astype(BF16)
    before = jnp.where(ci < ri, 1.0, 0.0)
    pos = lax.broadcasted_iota(jnp.int32, (n_slot, n_slot), 0).astype(F32)
    s_idx = lax.broadcasted_iota(jnp.int32, (SUBLANES, n_slot), 1)
    s_sub = lax.broadcasted_iota(jnp.int32, (SUBLANES, n_slot), 0)
    tok_slot = jnp.where(s_sub == 0, s_idx % rows, jnp.where(s_sub == 1, s_idx // rows, 0)).astype(BF16)
    lane = lax.broadcasted_iota(jnp.int32, (1, TB), 1)
    p_start_i = p_start.astype(jnp.int32)

    n_win = -(-(TB + n_slot) // TB)
    spare = slot_ref.shape[0] - 1
    empty = jnp.full((1, TB), -1, jnp.int32)

    def sort_tile(i):
        oh0, oh1 = one_hots(i)
        c0 = _dot(oh0.astype(BF16), earlier)
        c1 = _dot(oh1.astype(BF16), earlier)
        tot0 = jnp.sum(oh0, axis=1, keepdims=True)
        tot = tot0 + jnp.sum(oh1, axis=1, keepdims=True)
        off = jnp.dot(before, wide(tot), preferred_element_type=F32, precision=HIGHEST)[:, 0:1]
        lp = jnp.concatenate([jnp.sum(oh0 * (c0 + off), axis=0, keepdims=True),
                              jnp.sum(oh1 * (c1 + off + tot0), axis=0, keepdims=True)], axis=1)
        perm = jnp.where(pos == lp, 1.0, 0.0).astype(BF16)
        srt = lax.dot_general(tok_slot, perm, (((1,), (1,)), ((), ())), preferred_element_type=F32)
        ids = (srt[0:1, :] + srt[1:2, :] * n_tok).astype(jnp.int32) + i * rows
        ids = jnp.broadcast_to(jnp.concatenate([ids, jnp.zeros_like(ids)], axis=1), (SUBLANES, 2 * n_slot))
        return ids, tot, off

    def place(sorted_tile, carry):
        ids, tot, off = sorted_tile
        run, open_rows = carry
        tot_i, run_i, off_i = tot.astype(jnp.int32), run.astype(jnp.int32), off.astype(jnp.int32)
        new_rows = []
        for e in range(N_EXPERTS):
            n_e = tot_i[e, 0]
            row0 = p_start_i[e, 0] + run_i[e, 0]
            blk = row0 // TB
            at = row0 - blk * TB
            end = at + n_e
            moved = pltpu.roll(ids, at - off_i[e, 0] + 2 * n_slot, axis=1)
            keep = open_rows[e]
            for w in range(n_win):
                here = jnp.logical_and(lane + w * TB >= at, lane + w * TB < end)
                row = jnp.where(here, moved[0:1, w * TB:(w + 1) * TB], open_rows[e] if w == 0 else empty)
                slot_ref[pl.ds(blk if w == 0 else jnp.where(end > w * TB, blk + w, spare), 1), :] = row
                if w > 0:
                    keep = jnp.where(end >= w * TB, row, keep)
                else:
                    keep = row
            new_rows.append(keep)
        return run + tot, tuple(new_rows)

    def two_tiles(j, carry):
        first, second = sort_tile(2 * j), sort_tile(2 * j + 1)
        return place(second, place(first, carry))

    lax.fori_loop(0, n_tiles // 2, two_tiles, (jnp.zeros((N_EXPERTS, 1), F32), (empty,) * N_EXPERTS))


def _plan(route_t, n_blocks):
    nbp = -(-(n_blocks + TOP_K * TM // TB) // LANES) * LANES
    assert route_t.shape[0] % 2 == 0
    small = jax.ShapeDtypeStruct((N_EXPERTS, LANES), F32)
    return pl.pallas_call(
        _plan_kernel,
        out_shape=[jax.ShapeDtypeStruct((nbp, TB), jnp.int32),
                   jax.ShapeDtypeStruct((1, nbp), jnp.int32), small, small],
        compiler_params=pltpu.CompilerParams(vmem_limit_bytes=VMEM_LIMIT),
        name="plan",
    )(route_t)


def _sorted_rows(slot_raw, block_e, cnt, pst, n_tok, n_blocks):
    cnt = cnt[:, 0].astype(jnp.int32)
    pst = pst[:, 0].astype(jnp.int32)
    pad = (cnt + TB - 1) // TB * TB - cnt
    pad_before = jnp.cumsum(pad) - pad
    be = block_e[0, :n_blocks]
    base = TOP_K * n_tok - pst[be] - cnt[be] + pad_before[be]
    pad_id = base[:, None] + jnp.arange(n_blocks * TB, dtype=jnp.int32).reshape(n_blocks, TB)
    raw = slot_raw[:n_blocks]
    slot = jnp.where(raw < 0, pad_id, raw)
    src = jnp.where(raw < 0, 0, raw % n_tok)
    return slot.reshape(n_blocks, 1, TB), src.reshape(n_blocks, 1, TB)


def _moe_kernel(be_ref, src_cur, src_nxt, src_nx2, slot_prv, slot_cur, h_hbm, wg_ref, wu_ref, wd_ref, y_hbm,
                x0, x1, x2, y0, y1, y2, gsem, ssem):
    del be_ref
    i = pl.program_id(0)
    last = pl.num_programs(0) - 1
    xs, ys = (x0, x1, x2), (y0, y1, y2)

    def gather_start(idx_ref, dst, sem):
        for r in range(TB):
            pltpu.make_async_copy(h_hbm.at[pl.ds(idx_ref[0, 0, r], 1)], dst.at[pl.ds(r, 1)], sem).start(
                priority=r % 2)

    def gather_wait(dst, sem):
        pltpu.make_async_copy(h_hbm.at[pl.ds(0, TB)], dst, sem).wait()

    def scatter_start(idx_ref, src, sem):
        for r in range(TB):
            pltpu.make_async_copy(src.at[pl.ds(r, 1)], y_hbm.at[pl.ds(idx_ref[0, 0, r], 1)], sem).start(
                priority=r % 2)

    def scatter_wait(src, sem):
        pltpu.make_async_copy(src, y_hbm.at[pl.ds(0, TB)], sem).wait()

    def experts(x_ref, y_ref):
        x = _unpack_halves(x_ref[...]).astype(BF16)
        g = _dot(x, wg_ref[0])
        u = _dot(x, wu_ref[0])
        a = (g * jax.nn.sigmoid(g) * u).astype(BF16)
        y_ref[...] = _pack_halves(_dot(a, wd_ref[0]))

    for p in range(3):
        @pl.when(jnp.logical_and(i > 0, lax.rem(i, 3) == p))
        def _(p=p):
            prv, nx2 = (p + 2) % 3, (p + 2) % 3
            gather_wait(xs[p], gsem.at[p])

            @pl.when(i >= 3)
            def _():
                scatter_wait(ys[p], ssem.at[p])

            gather_start(src_nx2, xs[nx2], gsem.at[nx2])
            scatter_start(slot_prv, ys[prv], ssem.at[prv])
            experts(xs[p], ys[p])

    for p in range(3):
        @pl.when(jnp.logical_and(i == last, lax.rem(i, 3) == p))
        def _(p=p):
            nxt, prv = (p + 1) % 3, (p + 2) % 3
            scatter_start(slot_cur, ys[p], ssem.at[p])
            scatter_wait(ys[nxt], ssem.at[nxt])
            scatter_wait(ys[prv], ssem.at[prv])
            scatter_wait(ys[p], ssem.at[p])
            gather_wait(xs[nxt], gsem.at[nxt])
            gather_wait(xs[prv], gsem.at[prv])

    @pl.when(i == 0)
    def _():
        gather_start(src_cur, x0, gsem.at[0])
        gather_start(src_nxt, x1, gsem.at[1])
        gather_wait(x0, gsem.at[0])
        gather_start(src_nx2, x2, gsem.at[2])
        experts(x0, y0)


def _moe(block_e, slot3, src3, h2, wg, wu, wd, layer):
    nb = slot3.shape[0]
    dh = h2.shape[1]
    d, de = wg.shape[1], wg.shape[2]
    expert = lambda i, be: (layer * N_EXPERTS + be[i], 0, 0)
    idx_spec = lambda f: pl.BlockSpec((1, 1, TB), f, memory_space=pltpu.SMEM)
    assert nb >= 3
    cur = lambda i, be: (i, 0, 0)
    nxt = lambda i, be: (jnp.minimum(i + 1, nb - 1), 0, 0)
    nx2 = lambda i, be: (jnp.minimum(i + 2, nb - 1), 0, 0)
    prv = lambda i, be: (jnp.maximum(i - 1, 0), 0, 0)
    vm = pltpu.VMEM((TB, dh), jnp.uint32)
    return pl.pallas_call(
        _moe_kernel,
        grid_spec=pltpu.PrefetchScalarGridSpec(
            num_scalar_prefetch=1,
            grid=(nb,),
            in_specs=[idx_spec(cur), idx_spec(nxt), idx_spec(nx2), idx_spec(prv), idx_spec(cur),
                      pl.BlockSpec(memory_space=pl.ANY),
                      pl.BlockSpec((1, d, de), expert),
                      pl.BlockSpec((1, d, de), expert),
                      pl.BlockSpec((1, de, d), expert)],
            out_specs=pl.BlockSpec(memory_space=pl.ANY),
            scratch_shapes=[vm] * 6 + [pltpu.SemaphoreType.DMA((3,)), pltpu.SemaphoreType.DMA((3,))],
        ),
        out_shape=jax.ShapeDtypeStruct((nb * TB, dh), jnp.uint32),
        compiler_params=_params(("arbitrary",)),
        name="moe",
    )(block_e, src3, src3, src3, slot3, slot3, h2, wg, wu, wd)


def _combine_kernel(y0_ref, y1_ref, x1_ref, route_ref, mod_ref, gf_ref, o_ref, *, final_norm):
    d = x1_ref.shape[1]
    y = route_ref[:, 2:3] * _unpack_halves(y0_ref[...]) + route_ref[:, 3:4] * _unpack_halves(y1_ref[...])
    x2 = x1_ref[...] + mod_ref[:, 5 * d:6 * d] * y
    if final_norm:
        x2 = x2 * lax.rsqrt(jnp.mean(x2 * x2, axis=-1, keepdims=True) + EPS) * gf_ref[...]
    o_ref[...] = x2


def _combine(y_rows, x1, route, mod3, gf, n_batch, final_norm):
    nt, d = x1.shape
    nj = nt // n_batch // TM
    ctx_row = n_batch

    def modrow(i):
        return (jnp.where(i % nj == 0, ctx_row, i // nj), 0, 0)

    if final_norm:
        out_rows = nt - n_batch * TM
        out_map = lambda i: ((i // nj) * (nj - 1) + jnp.maximum(i % nj - 1, 0), 0)
    else:
        out_rows = nt
        out_map = lambda i: (i, 0)
    return pl.pallas_call(
        functools.partial(_combine_kernel, final_norm=final_norm),
        grid=(nt // TM,),
        in_specs=[pl.BlockSpec((TM, d // 2), lambda i: (i, 0)),
                  pl.BlockSpec((TM, d // 2), lambda i: (nt // TM + i, 0)),
                  pl.BlockSpec((TM, d), lambda i: (i, 0)),
                  pl.BlockSpec((TM, LANES), lambda i: (i, 0)),
                  pl.BlockSpec((None, 1, mod3.shape[2]), modrow),
                  _const_spec(gf.shape)],
        out_specs=pl.BlockSpec((TM, d), out_map),
        out_shape=jax.ShapeDtypeStruct((out_rows, d), F32),
        compiler_params=_params(("arbitrary",)),
        name="combine",
    )(y_rows, y_rows, x1, route, mod3, gf)


def kernel(x, c, ctx, c_ctx, w_mod, b_mod, g_norm1, g_norm2, w_in, w_gate2, b_gate2, gla_norm_g,
           sgu_ln_g, sgu_ln_b, sgu_w, sgu_b, w_branch_a, w_branch_b, b_branch, w_out,
           w_router, b_router, w_exp_gate, w_exp_up, w_exp_down, g_final):
    n_batch, seq, d = x.shape
    ctx_len = ctx.shape[1]
    depth = w_mod.shape[0]
    dk = w_gate2.shape[3]
    rank_lr = w_gate2.shape[2]
    assert ctx_len == TM and seq % TM == 0 and n_batch < MOD_ROWS

    nt = n_batch * (ctx_len + seq)
    streams = (ctx, x, False)
    pending = None
    n_blocks = TOP_K * nt // TB + N_EXPERTS
    cc = jnp.zeros((MOD_ROWS, d), F32).at[:n_batch].set(c).at[n_batch].set(c_ctx)
    mod = _modulation(cc, w_mod, b_mod)

    wr = jnp.zeros((d, LANES), F32).at[:, :N_EXPERTS].set(w_router)
    wr_hi = wr.astype(BF16)
    wr = jnp.stack([wr_hi, (wr - wr_hi.astype(F32)).astype(BF16)])
    br = jnp.zeros((1, LANES), F32).at[0, :N_EXPERTS].set(b_router)
    row = lambda a: a.reshape(1, -1)
    n_main = 2 * d + 2 * dk + 2 * d
    all_layers = lambda w: w.astype(BF16).reshape(-1, w.shape[2], w.shape[3])
    wg_all, wu_all, wd_all = all_layers(w_exp_gate), all_layers(w_exp_up), all_layers(w_exp_down)

    out = None
    for l in range(depth):
        last = l == depth - 1
        mod3 = mod[l].reshape(MOD_ROWS, 1, 6 * d)
        wmain = w_in[l][:, :n_main].astype(BF16)
        wlr = w_in[l][:, n_main:n_main + 2 * rank_lr].astype(BF16)
        wgates = w_in[l][:, n_main + 2 * rank_lr:].astype(BF16)
        wg2 = jnp.zeros((2 * rank_lr, 2 * dk), F32)
        wg2 = wg2.at[:rank_lr, :dk].set(w_gate2[l, 0]).at[rank_lr:, dk:].set(w_gate2[l, 1]).astype(BF16)
        bg2 = b_gate2[l].reshape(1, 2 * dk)
        sgub = jnp.repeat(sgu_b[l].T, d // A_GROUPS, axis=1)

        yag, gb, q, k, v, sr, la, *x2 = _inproj(
            streams[0], streams[1], pending, nt, mod3, row(g_norm1[l]), wmain, wlr, wgates, wg2, bg2,
            row(sgu_ln_g[l]), row(sgu_ln_b[l]), sgu_w[l].astype(BF16), sgub, w_branch_a[l].astype(BF16),
            row(b_branch[l]), n_batch)
        if x2:
            x2 = x2[0].reshape(n_batch, -1, d)
            streams = (x2, x2, True)
        ob = _gla_bwd(q, k, v, la, n_batch)
        x1, h2, route, route_t = _merge(
            *streams, nt, mod3, q, k, v, la, ob, yag, gb, sr, row(gla_norm_g[l]), w_branch_b[l].astype(BF16),
            w_out[l].astype(BF16), row(g_norm2[l]), wr, br, n_batch)
        slot_raw, block_e, cnt, pst = _plan(route_t, n_blocks)
        slot3, src3 = _sorted_rows(slot_raw, block_e, cnt, pst, nt, n_blocks)
        y_rows = _moe(block_e.reshape(-1), slot3, src3, h2, wg_all, wu_all, wd_all, l)
        if last:
            out = _combine(y_rows, x1, route, mod3, row(g_final), n_batch, final_norm=True).reshape(n_batch, seq, d)
        else:
            pending = (x1, y_rows, route, mod3)
    return out
```

```python
import functools

import jax
import jax.numpy as jnp
from jax import lax
from jax.experimental import pallas as pl
from jax.experimental.pallas import tpu as pltpu

F32 = jnp.float32
BF16 = jnp.bfloat16
HIGHEST = lax.Precision.HIGHEST

A_CHUNK = 128
A_GROUPS = 8
GLA_HEADS = 4
GLA_TAU = 16.0
GLA_CHUNK = 64
N_EXPERTS = 16
N_EXPERT_GROUPS = 4
EXPERTS_PER_GROUP = N_EXPERTS // N_EXPERT_GROUPS
TOP_K = 2
EPS = 1e-6

LANES = 128
SUBLANES = 8
TM = 256
TB = 256
BG = 2
CAST_STEPS = 64
MOD_ROWS = 16
VMEM_LIMIT = 56 * 1024 * 1024


def _dot(a, b):
    return jnp.dot(a, b, preferred_element_type=F32)


def _pack_halves(x):
    n = x.shape[1] // 2
    bits = lambda v: pltpu.bitcast(v.astype(BF16).astype(F32), jnp.uint32)
    return bits(x[:, :n]) | (bits(x[:, n:]) >> 16)


def _unpack_halves(p):
    hi = pltpu.bitcast(p & jnp.uint32(0xFFFF0000), F32)
    lo = pltpu.bitcast(p << 16, F32)
    return jnp.concatenate([hi, lo], axis=1)


def _const_spec(shape):
    nd = len(shape)
    return pl.BlockSpec(shape, lambda *_: (0,) * nd)


def _params(sem):
    return pltpu.CompilerParams(dimension_semantics=sem, vmem_limit_bytes=VMEM_LIMIT)


def _mod_kernel(cc_ref, w_ref, b_ref, o_ref):
    cc = cc_ref[...]
    s = cc * jax.nn.sigmoid(cc)
    o_ref[0] = jnp.dot(s, w_ref[0], preferred_element_type=F32, precision=HIGHEST) + b_ref[0]


def _modulation(cc, w_mod, b_mod):
    n_layer, d, six_d = w_mod.shape
    return pl.pallas_call(
        _mod_kernel,
        grid=(n_layer, six_d // d),
        in_specs=[
            pl.BlockSpec((MOD_ROWS, d), lambda l, j: (0, 0)),
            pl.BlockSpec((1, d, d), lambda l, j: (l, 0, j)),
            pl.BlockSpec((1, 1, d), lambda l, j: (l, 0, j)),
        ],
        out_specs=pl.BlockSpec((1, MOD_ROWS, d), lambda l, j: (l, 0, j)),
        out_shape=jax.ShapeDtypeStruct((n_layer, MOD_ROWS, six_d), F32),
        compiler_params=_params(("arbitrary", "arbitrary")),
        name="modulation",
    )(cc, w_mod, b_mod.reshape(n_layer, 1, six_d))


def _gelu_tanh(x):
    c = 0.7978845608028654
    return 0.5 * x * (1.0 + jnp.tanh(c * (x + 0.044715 * (x * x * x))))


def _log_sigmoid(z):
    return jnp.minimum(z, 0.0) - jnp.log1p(jnp.exp(-jnp.abs(z)))


def _inproj_stages(x_of, modc_ref, modl_ref, g1_ref, wmain_ref, wlr_ref, wgates_ref, wg2_ref, bg2_ref,
                   lng_ref, lnb_ref, sguw_ref, sgub_ref, wa_ref, bbr_ref,
                   yag_ref, gb_ref, q_ref, k_ref, v_ref, sr_ref, la_ref, sa_ref):
    first = pl.program_id(1) == 0
    n_g, rows, d = yag_ref.shape
    dk = q_ref.shape[2]
    gdim = d // A_GROUPS
    head_k = dk // GLA_HEADS
    c_q, c_v = 2 * d, 2 * d + 2 * dk
    hb, u, vnb, ya = {}, {}, {}, {}

    def norm(g):
        mod = jnp.where(first, modc_ref[0], modl_ref[g])
        x = x_of(g)
        h = x * lax.rsqrt(jnp.mean(x * x, axis=-1, keepdims=True) + EPS) * g1_ref[...]
        hb[g] = (h * (1.0 + mod[:, d:2 * d]) + mod[:, 0:d]).astype(BF16)

    def gate_u(g):
        u[g] = _gelu_tanh(_dot(hb[g], wmain_ref[:, 0:d]))

    def gate_v(g):
        vv = _gelu_tanh(_dot(hb[g], wmain_ref[:, d:2 * d]))
        vc = vv - jnp.mean(vv, axis=-1, keepdims=True)
        vn = vc * lax.rsqrt(jnp.mean(vc * vc, axis=-1, keepdims=True) + EPS) * lng_ref[...] + lnb_ref[...]
        vnb[g] = vn.astype(BF16)

    def spatial(g):
        for n in range(rows // A_CHUNK):
            rs = slice(n * A_CHUNK, (n + 1) * A_CHUNK)
            for a in range(A_GROUPS):
                cs = slice(a * gdim, (a + 1) * gdim)
                mixed = _dot(sguw_ref[a], vnb[g][rs, cs]) + sgub_ref[:, cs]
                sa_ref[g, rs, cs] = (u[g][rs, cs] * mixed).astype(BF16)

    def proj_a(g):
        ya[g] = _dot(sa_ref[g], wa_ref[...])

    def branch_gates(g):
        gates = _dot(hb[g], wgates_ref[...]) + bbr_ref[...]
        yag_ref[g] = (jax.nn.sigmoid(gates[:, 0:d]) * ya[g]).astype(BF16)
        gb_ref[g] = jax.nn.sigmoid(gates[:, d:2 * d]).astype(BF16)

    def qk(g):
        q_ref[g] = (_dot(hb[g], wmain_ref[:, c_q:c_q + dk]) * (head_k ** -0.5)).astype(BF16)
        k_ref[g] = _dot(hb[g], wmain_ref[:, c_q + dk:c_q + 2 * dk]).astype(BF16)

    def val(g):
        v_ref[g] = _dot(hb[g], wmain_ref[:, c_v:c_v + d]).astype(BF16)

    def out_gate(g):
        r = _dot(hb[g], wmain_ref[:, c_v + d:c_v + 2 * d])
        sr_ref[g] = (r * jax.nn.sigmoid(r)).astype(BF16)

    def decay(g):
        lr = _dot(hb[g], wlr_ref[...]).astype(BF16)
        z = _dot(lr, wg2_ref[...]) + bg2_ref[...]
        la_ref[g] = _log_sigmoid(z) * (1.0 / GLA_TAU)

    order = [norm, gate_u, decay, gate_v, qk, spatial, out_gate, proj_a, branch_gates, val]
    for stage in order:
        for g in range(n_g):
            stage(g)


def _inproj_kernel(xc_ref, xl_ref, *rest):
    first = pl.program_id(1) == 0
    _inproj_stages(lambda g: jnp.where(first, xc_ref[g], xl_ref[g]), *rest)


def _inproj_after_moe_kernel(x1_ref, y00_ref, y01_ref, y10_ref, y11_ref, route_ref, pmodc_ref, pmodl_ref, *rest):
    *rest, x2_ref, sa_ref = rest
    first = pl.program_id(1) == 0
    d = x1_ref.shape[2]
    y_refs = ((y00_ref, y10_ref), (y01_ref, y11_ref))

    def x_of(g):
        pmod = jnp.where(first, pmodc_ref[0], pmodl_ref[g])
        y = (route_ref[g][:, 2:3] * _unpack_halves(y_refs[g][0][...])
             + route_ref[g][:, 3:4] * _unpack_halves(y_refs[g][1][...]))
        x2 = x1_ref[g] + pmod[:, 5 * d:6 * d] * y
        x2_ref[g] = x2
        return x2

    _inproj_stages(x_of, *rest, sa_ref)


def _stream_specs(joined, ctx_row, six_d, d):
    lat_map = (lambda bp, j: (bp, jnp.maximum(j, 1), 0)) if joined else (lambda bp, j: (bp, jnp.maximum(j - 1, 0), 0))
    return [pl.BlockSpec((BG, TM, d), lambda bp, j: (bp, 0, 0)), pl.BlockSpec((BG, TM, d), lat_map),
            pl.BlockSpec((1, 1, six_d), lambda bp, j: (ctx_row, 0, 0)),
            pl.BlockSpec((BG, 1, six_d), lambda bp, j: (bp, 0, 0))]


def _resident_spec(shape):
    nd = len(shape)
    return pl.BlockSpec(shape, lambda *_: (0,) * nd, pipeline_mode=pl.Buffered(1))


def _inproj(xc, xl, pending, nt, mod3, g1, wmain, wlr, wgates, wg2, bg2, lng, lnb, sguw, sgub, wa, bbr, n_batch):
    assert BG == 2
    dk = wg2.shape[1] // 2
    t_all = nt // n_batch
    nj = t_all // TM
    six_d = mod3.shape[2]
    d = six_d // 6
    tile = lambda w: pl.BlockSpec((BG, TM, w), lambda bp, j: (bp, j, 0))
    outs = [(d, BF16), (d, BF16), (dk, BF16), (dk, BF16), (d, BF16), (d, BF16), (2 * dk, F32)]
    weights = (g1, wmain, wlr, wgates, wg2, bg2, lng, lnb, sguw, sgub, wa, bbr)
    mod_specs = lambda: [pl.BlockSpec((1, 1, six_d), lambda bp, j: (n_batch, 0, 0)),
                         pl.BlockSpec((BG, 1, six_d), lambda bp, j: (bp, 0, 0))]
    if pending is None:
        kern = _inproj_kernel
        operands = (xc, xl, mod3, mod3)
        in_specs = _stream_specs(False, n_batch, six_d, d)
    else:
        x1, y_rows, route, mod3_prev = pending
        kern = _inproj_after_moe_kernel
        outs = outs + [(d, F32)]
        y_spec = lambda slot, g: pl.BlockSpec(
            (TM, d // 2), lambda bp, j: (slot * (nt // TM) + (bp * BG + g) * nj + j, 0))
        operands = (x1.reshape(n_batch, t_all, d), y_rows, y_rows, y_rows, y_rows,
                    route.reshape(n_batch, t_all, LANES), mod3_prev, mod3_prev, mod3, mod3)
        in_specs = ([tile(d), y_spec(0, 0), y_spec(0, 1), y_spec(1, 0), y_spec(1, 1), tile(LANES)]
                    + mod_specs() + mod_specs())
    res = pl.pallas_call(
        kern,
        grid=(n_batch // BG, nj),
        in_specs=in_specs + [_resident_spec(a.shape) for a in weights],
        out_specs=[tile(w) for w, _ in outs],
        out_shape=[jax.ShapeDtypeStruct((n_batch, t_all, w), t) for w, t in outs],
        scratch_shapes=[pltpu.VMEM((BG, TM, d), BF16)],
        compiler_params=_params(("arbitrary", "arbitrary")),
        name="inproj",
    )(*operands, *weights)
    return [a.reshape(nt, a.shape[2]) for a in res]


def _gla_tiles(q_refs, k_refs, v_refs, la_refs, s_refs, o_refs, qin_sc, kv_sc, dm_sc, reverse):
    n_g = len(q_refs)
    rows, dk = q_refs[0].shape
    dv = v_refs[0].shape[1]
    hk = dk // GLA_HEADS
    hv = dv // GLA_HEADS
    c = GLA_CHUNK
    ri = lax.broadcasted_iota(jnp.int32, (c, c), 0)
    ci = lax.broadcasted_iota(jnp.int32, (c, c), 1)
    tri = (ci >= ri) if reverse else (ci <= ri)
    rt = lax.broadcasted_iota(jnp.int32, (rows, rows), 0)
    ct = lax.broadcasted_iota(jnp.int32, (rows, rows), 1)
    in_chunk = (rt // c) == (ct // c)
    tri_t = jnp.where(jnp.logical_and(in_chunk, (ct >= rt) if reverse else (ct <= rt)), 1.0, 0.0).astype(BF16)
    chunks = list(range(rows // c))
    for g in range(n_g):
        la = la_refs[g][...]
        la_hi = la.astype(BF16)
        la_lo = (la - la_hi.astype(F32)).astype(BF16)
        b_all = _dot(tri_t, la_hi) + _dot(tri_t, la_lo)
        for n in chunks:
            rs = slice(n * c, (n + 1) * c)
            b = b_all[rs, :]
            b_end = b[0:1, :] if reverse else b[c - 1:c, :]
            q = q_refs[g][rs, :].astype(F32)
            k = k_refs[g][rs, :].astype(F32)
            q_in = (q * jnp.exp(b)).astype(BF16)
            k_in = (k * jnp.exp(-b)).astype(BF16)
            k_st = (k * jnp.exp(b_end - b)).astype(BF16)
            decay = jnp.exp(b_end)
            qin_sc[g, rs, :] = q_in
            for h in range(GLA_HEADS):
                ks = slice(h * hk, (h + 1) * hk)
                vs = slice(h * hv, (h + 1) * hv)
                vh = v_refs[g][rs, vs]
                att = lax.dot_general(q_in[:, ks], k_in[:, ks], (((1,), (1,)), ((), ())),
                                      preferred_element_type=F32)
                o_refs[g][rs, vs] = _dot(jnp.where(tri, att, 0.0).astype(BF16), vh)
                kv_sc[g, n, h] = lax.dot_general(k_st[:, ks], vh, (((0,), (0,)), ((), ())),
                                                 preferred_element_type=F32)
                dm_sc[g, n, h] = jnp.transpose(jnp.broadcast_to(decay[:, ks], (hk, hk)))
    for n in (reversed(chunks) if reverse else chunks):
        rs = slice(n * c, (n + 1) * c)
        for g in range(n_g):
            for h in range(GLA_HEADS):
                ks = slice(h * hk, (h + 1) * hk)
                vs = slice(h * hv, (h + 1) * hv)
                state = s_refs[g][h]
                o_refs[g][rs, vs] += _dot(qin_sc[g, rs, ks], state.astype(BF16))
                dmat = jnp.concatenate([dm_sc[g, n, h]] * (hv // hk), axis=1)
                s_refs[g][h] = dmat * state + kv_sc[g, n, h]


def _gla_scratch(n_g, rows, dk, dv):
    hk, hv, n_chunks = dk // GLA_HEADS, dv // GLA_HEADS, rows // GLA_CHUNK
    return [pltpu.VMEM((n_g, GLA_HEADS, hk, hv), F32),
            pltpu.VMEM((n_g, rows, dv), F32),
            pltpu.VMEM((n_g, rows, dk), BF16),
            pltpu.VMEM((n_g, n_chunks, GLA_HEADS, hk, hv), F32),
            pltpu.VMEM((n_g, n_chunks, GLA_HEADS, hk, hk), F32)]


def _per_batch(ref):
    return [ref.at[g] for g in range(ref.shape[0])]


def _gla_bwd_kernel(q_ref, k_ref, v_ref, la_ref, *rest, n_cast):
    if n_cast:
        wg_ref, wu_ref, wd_ref, ob_ref, wgb_ref, wub_ref, wdb_ref, *scratch = rest
    else:
        ob_ref, *scratch = rest
    s_ref, o_sc, qin_sc, kv_sc, dm_sc = scratch

    @pl.when(pl.program_id(1) == 0)
    def _():
        s_ref[...] = jnp.zeros_like(s_ref)

    _gla_tiles(_per_batch(q_ref), _per_batch(k_ref), _per_batch(v_ref), _per_batch(la_ref),
               _per_batch(s_ref), _per_batch(o_sc), qin_sc, kv_sc, dm_sc, reverse=True)
    ob_ref[...] = o_sc[...].astype(BF16)

    if n_cast:
        @pl.when(pl.program_id(0) * pl.num_programs(1) + pl.program_id(1) < n_cast)
        def _():
            wgb_ref[...] = wg_ref[...].astype(BF16)
            wub_ref[...] = wu_ref[...].astype(BF16)
            wdb_ref[...] = wd_ref[...].astype(BF16)


def _gla_bwd(q, k, v, la, n_batch, experts, layer):
    nt, dk = q.shape
    dv = v.shape[1]
    t_all = nt // n_batch
    nj = t_all // TM
    per_batch = lambda a: a.reshape(n_batch, t_all, a.shape[1])
    steps = n_batch // BG * nj
    n_cast = CAST_STEPS if steps >= CAST_STEPS else 0

    def tile(bp, jj):
        return (bp, jnp.where(jj == 0, 0, nj - jj), 0)

    def tile_la(bp, jj):
        return (bp, jnp.where(jj == 0, 0, nj - jj), 1)

    in_specs = [pl.BlockSpec((BG, TM, dk), tile), pl.BlockSpec((BG, TM, dk), tile),
                pl.BlockSpec((BG, TM, dv), tile), pl.BlockSpec((BG, TM, dk), tile_la)]
    out_specs = [pl.BlockSpec((BG, TM, dv), tile)]
    out_shape = [jax.ShapeDtypeStruct((n_batch, t_all, dv), BF16)]
    operands = [per_batch(q), per_batch(k), per_batch(v), per_batch(la)]
    for w in experts if n_cast else ():
        rows = w.shape[1] * w.shape[2]
        blk = (rows // n_cast, w.shape[3])
        operands.append(w.reshape(-1, w.shape[3]))
        in_specs.append(pl.BlockSpec(
            blk, lambda bp, jj: (layer * n_cast + jnp.minimum(bp * nj + jj, n_cast - 1), 0)))
        out_specs.append(pl.BlockSpec(blk, lambda bp, jj: (jnp.minimum(bp * nj + jj, n_cast - 1), 0)))
        out_shape.append(jax.ShapeDtypeStruct((rows, w.shape[3]), BF16))
    ob, *cast = pl.pallas_call(
        functools.partial(_gla_bwd_kernel, n_cast=n_cast),
        grid=(n_batch // BG, nj),
        in_specs=in_specs,
        out_specs=out_specs,
        out_shape=out_shape,
        scratch_shapes=_gla_scratch(BG, TM, dk, dv),
        compiler_params=_params(("arbitrary", "arbitrary")),
        name="gla_bwd",
    )(*operands)
    if n_cast:
        cast = [c.reshape(w.shape[1:]) for c, w in zip(cast, experts)]
    else:
        cast = [w[layer].astype(BF16) for w in experts]
    return ob.reshape(nt, dv), cast


def _route(logits_t):
    n_tok = logits_t.shape[1]
    eid = lax.broadcasted_iota(jnp.int32, logits_t.shape, 0)
    ex = jnp.exp(logits_t - jnp.max(logits_t, axis=0, keepdims=True))
    p = ex / jnp.sum(ex, axis=0, keepdims=True)
    grp = eid // EXPERTS_PER_GROUP
    none = -1.0
    far = 2 * N_EXPERTS
    best = None
    for g in range(N_EXPERT_GROUPS):
        pg = jnp.where(grp == g, p, none)
        m1 = jnp.max(pg, axis=0, keepdims=True)
        i1 = jnp.min(jnp.where(pg == m1, eid, far), axis=0, keepdims=True)
        pg2 = jnp.where(eid == i1, none, pg)
        m2 = jnp.max(pg2, axis=0, keepdims=True)
        i2 = jnp.min(jnp.where(pg2 == m2, eid, far), axis=0, keepdims=True)
        cand = (m1 + m2, m1, i1, m2, i2)
        if best is None:
            best = cand
        else:
            better = cand[0] > best[0]
            best = tuple(jnp.where(better, c, o) for c, o in zip(cand, best))
    _, m1, i1, m2, i2 = best
    tot = m1 + m2
    sub = lax.broadcasted_iota(jnp.int32, (SUBLANES, n_tok), 0)
    out = jnp.where(sub == 0, i1.astype(F32), 0.0)
    out = jnp.where(sub == 1, i2.astype(F32), out)
    out = jnp.where(sub == 2, m1 / tot, out)
    out = jnp.where(sub == 3, m2 / tot, out)
    return out


def _merge_kernel(xc_ref, xl_ref, modc_ref, modl_ref, q_ref, k_ref, v_ref, la_ref, ob_ref, yag_ref, gb_ref,
                  sr_ref, glag_ref, wb_ref, wout_ref, g2_ref, wr_ref, br_ref,
                  x1_ref, h2_ref, route_ref, routet_ref, s_ref, o_sc, qin_sc, kv_sc, dm_sc):
    first = pl.program_id(1) == 0

    @pl.when(first)
    def _():
        s_ref[...] = jnp.zeros_like(s_ref)

    _gla_tiles(_per_batch(q_ref), _per_batch(k_ref), _per_batch(v_ref), _per_batch(la_ref),
               _per_batch(s_ref), _per_batch(o_sc), qin_sc, kv_sc, dm_sc, reverse=False)

    d = xc_ref.shape[2]
    hv = d // GLA_HEADS
    mod, yb_in, mix, h2, logits = {}, {}, {}, {}, {}

    def head_norm(g):
        mod[g] = jnp.where(first, modc_ref[0], modl_ref[g])
        o = o_sc[g] + ob_ref[g].astype(F32)
        parts = []
        for h in range(GLA_HEADS):
            oh = o[:, h * hv:(h + 1) * hv]
            parts.append(oh * lax.rsqrt(jnp.mean(oh * oh, axis=-1, keepdims=True) + EPS))
        on = jnp.concatenate(parts, axis=1) * glag_ref[...]
        yb_in[g] = (on * sr_ref[g].astype(F32)).astype(BF16)

    def branch_merge(g):
        yb = _dot(yb_in[g], wb_ref[...])
        mix[g] = (yag_ref[g].astype(F32) + gb_ref[g].astype(F32) * yb).astype(BF16)

    def residual(g):
        y = _dot(mix[g], wout_ref[...])
        x1 = jnp.where(first, xc_ref[g], xl_ref[g]) + mod[g][:, 2 * d:3 * d] * y
        x1_ref[g] = x1
        h = x1 * lax.rsqrt(jnp.mean(x1 * x1, axis=-1, keepdims=True) + EPS) * g2_ref[...]
        h2[g] = h * (1.0 + mod[g][:, 4 * d:5 * d]) + mod[g][:, 3 * d:4 * d]
        h2_ref[g] = _pack_halves(h2[g])

    def router_logits(g):
        h_hi = h2[g].astype(BF16)
        h_lo = (h2[g] - h_hi.astype(F32)).astype(BF16)
        logits[g] = _dot(h_hi, wr_ref[0]) + _dot(h_lo, wr_ref[0]) + _dot(h_hi, wr_ref[1]) + br_ref[...]

    def routing(g):
        rt = _route(jnp.transpose(logits[g])[0:N_EXPERTS, :])
        routet_ref[g, 0] = rt
        sub = lax.broadcasted_iota(jnp.int32, (LANES, rt.shape[1]), 0)
        padded = jnp.zeros((LANES, rt.shape[1]), F32)
        for r in range(4):
            padded = jnp.where(sub == r, rt[r:r + 1, :], padded)
        route_ref[g] = jnp.transpose(padded)

    for stage in (head_norm, branch_merge, residual, router_logits, routing):
        for g in range(xc_ref.shape[0]):
            stage(g)


def _merge(xc, xl, joined, nt, mod3, q, k, v, la, ob, yag, gb, sr, glag, wb, wout, g2, wr, br, n_batch):
    d = xc.shape[2]
    dk = q.shape[1]
    t_all = nt // n_batch
    nj = t_all // TM
    per_batch = lambda a: a.reshape(n_batch, t_all, a.shape[1])
    tile = lambda w: pl.BlockSpec((BG, TM, w), lambda bp, j: (bp, j, 0))
    x1, h2, route, route_t = pl.pallas_call(
        _merge_kernel,
        grid=(n_batch // BG, nj),
        in_specs=_stream_specs(joined, n_batch, mod3.shape[2], d)
        + [tile(dk), tile(dk), tile(d), tile(dk), tile(d), tile(d), tile(d), tile(d)]
        + [_const_spec(a.shape) for a in (glag, wb, wout, g2, wr, br)],
        out_specs=[tile(d), tile(d // 2), tile(LANES),
                   pl.BlockSpec((BG, 1, SUBLANES, TM), lambda bp, j: (bp, j, 0, 0))],
        out_shape=[jax.ShapeDtypeStruct((n_batch, t_all, d), F32),
                   jax.ShapeDtypeStruct((n_batch, t_all, d // 2), jnp.uint32),
                   jax.ShapeDtypeStruct((n_batch, t_all, LANES), F32),
                   jax.ShapeDtypeStruct((n_batch, nj, SUBLANES, TM), F32)],
        scratch_shapes=_gla_scratch(BG, TM, dk, d),
        compiler_params=_params(("arbitrary", "arbitrary")),
        name="merge",
    )(xc, xl, mod3, mod3, per_batch(q), per_batch(k), per_batch(v), per_batch(la), per_batch(ob),
      per_batch(yag), per_batch(gb), per_batch(sr), glag, wb, wout, g2, wr, br)
    return (x1.reshape(nt, d), h2.reshape(nt, d // 2), route.reshape(nt, LANES),
            route_t.reshape(nt // TM, SUBLANES, TM))


def _plan_kernel(rt_ref, slot_ref, be_ref, cnt_ref, pst_ref):
    n_tiles, _, rows = rt_ref.shape
    n_tok = n_tiles * rows
    n_slot = TOP_K * rows
    sub = lax.broadcasted_iota(jnp.int32, (N_EXPERTS, rows), 0).astype(F32)
    wide = lambda col: jnp.broadcast_to(col, (N_EXPERTS, LANES))

    def one_hots(i):
        rt = rt_ref[i]
        return jnp.where(sub == rt[0:1, :], 1.0, 0.0), jnp.where(sub == rt[1:2, :], 1.0, 0.0)

    def count(i, acc):
        oh0, oh1 = one_hots(i)
        return acc + jnp.sum(oh0 + oh1, axis=1, keepdims=True)

    cnt = wide(lax.fori_loop(0, n_tiles, count, jnp.zeros((N_EXPERTS, 1), F32)))
    padded = jnp.floor((cnt + (TB - 1)) * (1.0 / TB)) * TB
    ri = lax.broadcasted_iota(jnp.int32, (N_EXPERTS, N_EXPERTS), 0)
    ci = lax.broadcasted_iota(jnp.int32, (N_EXPERTS, N_EXPERTS), 1)
    p_end = jnp.dot(jnp.where(ci <= ri, 1.0, 0.0), padded, preferred_element_type=F32, precision=HIGHEST)
    p_start = p_end - padded
    cnt_ref[...] = cnt
    pst_ref[...] = p_start
    starts = lax.broadcasted_iota(jnp.int32, (N_EXPERTS, be_ref.shape[1]), 1).astype(F32) * TB
    done = jnp.sum(jnp.where(p_end[:, 0:1] <= starts, 1.0, 0.0), axis=0, keepdims=True)
    be_ref[...] = jnp.minimum(done, N_EXPERTS - 1.0).astype(jnp.int32)
    slot_ref[...] = jnp.full(slot_ref.shape, -1, jnp.int32)

    rr = lax.broadcasted_iota(jnp.int32, (rows, rows), 0)
    cc = lax.broadcasted_iota(jnp.int32, (rows, rows), 1)
    earlier = jnp.where(rr < cc, 1.0, 0.0).astype(BF16)
    before = jnp.where(ci < ri, 1.0, 0.0)
    pos = lax.broadcasted_iota(jnp.int32, (n_slot, n_slot), 0).astype(F32)
    s_idx = lax.broadcasted_iota(jnp.int32, (SUBLANES, n_slot), 1)
    s_sub = lax.broadcasted_iota(jnp.int32, (SUBLANES, n_slot), 0)
    tok_slot = jnp.where(s_sub == 0, s_idx % rows, jnp.where(s_sub == 1, s_idx // rows, 0)).astype(BF16)
    lane = lax.broadcasted_iota(jnp.int32, (1, TB), 1)
    p_start_i = p_start.astype(jnp.int32)

    n_win = -(-(TB + n_slot) // TB)
    spare = slot_ref.shape[0] - 1
    empty = jnp.full((1, TB), -1, jnp.int32)

    def sort_tile(i):
        oh0, oh1 = one_hots(i)
        c0 = _dot(oh0.astype(BF16), earlier)
        c1 = _dot(oh1.astype(BF16), earlier)
        tot0 = jnp.sum(oh0, axis=1, keepdims=True)
        tot = tot0 + jnp.sum(oh1, axis=1, keepdims=True)
        off = jnp.dot(before, wide(tot), preferred_element_type=F32, precision=HIGHEST)[:, 0:1]
        lp = jnp.concatenate([jnp.sum(oh0 * (c0 + off), axis=0, keepdims=True),
                              jnp.sum(oh1 * (c1 + off + tot0), axis=0, keepdims=True)], axis=1)
        perm = jnp.where(pos == lp, 1.0, 0.0).astype(BF16)
        srt = lax.dot_general(tok_slot, perm, (((1,), (1,)), ((), ())), preferred_element_type=F32)
        ids = (srt[0:1, :] + srt[1:2, :] * n_tok).astype(jnp.int32) + i * rows
        ids = jnp.broadcast_to(jnp.concatenate([ids, jnp.zeros_like(ids)], axis=1), (SUBLANES, 2 * n_slot))
        return ids, tot, off

    def place(sorted_tile, carry):
        ids, tot, off = sorted_tile
        run, open_rows = carry
        tot_i, run_i, off_i = tot.astype(jnp.int32), run.astype(jnp.int32), off.astype(jnp.int32)
        new_rows = []
        for e in range(N_EXPERTS):
            n_e = tot_i[e, 0]
            row0 = p_start_i[e, 0] + run_i[e, 0]
            blk = row0 // TB
            at = row0 - blk * TB
            end = at + n_e
            moved = pltpu.roll(ids, at - off_i[e, 0] + 2 * n_slot, axis=1)
            keep = open_rows[e]
            for w in range(n_win):
                here = jnp.logical_and(lane + w * TB >= at, lane + w * TB < end)
                row = jnp.where(here, moved[0:1, w * TB:(w + 1) * TB], open_rows[e] if w == 0 else empty)
                slot_ref[pl.ds(blk if w == 0 else jnp.where(end > w * TB, blk + w, spare), 1), :] = row
                if w > 0:
                    keep = jnp.where(end >= w * TB, row, keep)
                else:
                    keep = row
            new_rows.append(keep)
        return run + tot, tuple(new_rows)

    def two_tiles(j, carry):
        first, second = sort_tile(2 * j), sort_tile(2 * j + 1)
        return place(second, place(first, carry))

    lax.fori_loop(0, n_tiles // 2, two_tiles, (jnp.zeros((N_EXPERTS, 1), F32), (empty,) * N_EXPERTS))


def _plan(route_t, n_blocks):
    nbp = -(-(n_blocks + TOP_K * TM // TB) // LANES) * LANES
    assert route_t.shape[0] % 2 == 0
    small = jax.ShapeDtypeStruct((N_EXPERTS, LANES), F32)
    return pl.pallas_call(
        _plan_kernel,
        out_shape=[jax.ShapeDtypeStruct((nbp, TB), jnp.int32),
                   jax.ShapeDtypeStruct((1, nbp), jnp.int32), small, small],
        compiler_params=pltpu.CompilerParams(vmem_limit_bytes=VMEM_LIMIT),
        name="plan",
    )(route_t)


def _sorted_rows(slot_raw, block_e, cnt, pst, n_tok, n_blocks):
    cnt = cnt[:, 0].astype(jnp.int32)
    pst = pst[:, 0].astype(jnp.int32)
    pad = (cnt + TB - 1) // TB * TB - cnt
    pad_before = jnp.cumsum(pad) - pad
    be = block_e[0, :n_blocks]
    base = TOP_K * n_tok - pst[be] - cnt[be] + pad_before[be]
    pad_id = base[:, None] + jnp.arange(n_blocks * TB, dtype=jnp.int32).reshape(n_blocks, TB)
    raw = slot_raw[:n_blocks]
    slot = jnp.where(raw < 0, pad_id, raw)
    src = jnp.where(raw < 0, 0, raw % n_tok)
    return slot.reshape(n_blocks, 1, TB), src.reshape(n_blocks, 1, TB)


def _moe_kernel(be_ref, src_cur, src_nxt, src_nx2, slot_prv, slot_cur, h_hbm, wg_ref, wu_ref, wd_ref, y_hbm,
                x0, x1, x2, y0, y1, y2, gsem, ssem):
    del be_ref
    i = pl.program_id(0)
    last = pl.num_programs(0) - 1
    xs, ys = (x0, x1, x2), (y0, y1, y2)

    def gather_start(idx_ref, dst, sem):
        for r in range(TB):
            pltpu.make_async_copy(h_hbm.at[pl.ds(idx_ref[0, 0, r], 1)], dst.at[pl.ds(r, 1)], sem).start(
                priority=r % 2)

    def gather_wait(dst, sem):
        pltpu.make_async_copy(h_hbm.at[pl.ds(0, TB)], dst, sem).wait()

    def scatter_start(idx_ref, src, sem):
        for r in range(TB):
            pltpu.make_async_copy(src.at[pl.ds(r, 1)], y_hbm.at[pl.ds(idx_ref[0, 0, r], 1)], sem).start(
                priority=r % 2)

    def scatter_wait(src, sem):
        pltpu.make_async_copy(src, y_hbm.at[pl.ds(0, TB)], sem).wait()

    def experts(x_ref, y_ref):
        x = _unpack_halves(x_ref[...]).astype(BF16)
        g = _dot(x, wg_ref[0])
        u = _dot(x, wu_ref[0])
        a = (g * jax.nn.sigmoid(g) * u).astype(BF16)
        y_ref[...] = _pack_halves(_dot(a, wd_ref[0]))

    for p in range(3):
        @pl.when(jnp.logical_and(i > 0, lax.rem(i, 3) == p))
        def _(p=p):
            prv, nx2 = (p + 2) % 3, (p + 2) % 3
            gather_wait(xs[p], gsem.at[p])

            @pl.when(i >= 3)
            def _():
                scatter_wait(ys[p], ssem.at[p])

            gather_start(src_nx2, xs[nx2], gsem.at[nx2])
            scatter_start(slot_prv, ys[prv], ssem.at[prv])
            experts(xs[p], ys[p])

    for p in range(3):
        @pl.when(jnp.logical_and(i == last, lax.rem(i, 3) == p))
        def _(p=p):
            nxt, prv = (p + 1) % 3, (p + 2) % 3
            scatter_start(slot_cur, ys[p], ssem.at[p])
            scatter_wait(ys[nxt], ssem.at[nxt])
            scatter_wait(ys[prv], ssem.at[prv])
            scatter_wait(ys[p], ssem.at[p])
            gather_wait(xs[nxt], gsem.at[nxt])
            gather_wait(xs[prv], gsem.at[prv])

    @pl.when(i == 0)
    def _():
        gather_start(src_cur, x0, gsem.at[0])
        gather_start(src_nxt, x1, gsem.at[1])
        gather_wait(x0, gsem.at[0])
        gather_start(src_nx2, x2, gsem.at[2])
        experts(x0, y0)


def _moe(block_e, slot3, src3, h2, wg, wu, wd):
    nb = slot3.shape[0]
    dh = h2.shape[1]
    d, de = wg.shape[1], wg.shape[2]
    expert = lambda i, be: (be[i], 0, 0)
    idx_spec = lambda f: pl.BlockSpec((1, 1, TB), f, memory_space=pltpu.SMEM)
    assert nb >= 3
    cur = lambda i, be: (i, 0, 0)
    nxt = lambda i, be: (jnp.minimum(i + 1, nb - 1), 0, 0)
    nx2 = lambda i, be: (jnp.minimum(i + 2, nb - 1), 0, 0)
    prv = lambda i, be: (jnp.maximum(i - 1, 0), 0, 0)
    vm = pltpu.VMEM((TB, dh), jnp.uint32)
    return pl.pallas_call(
        _moe_kernel,
        grid_spec=pltpu.PrefetchScalarGridSpec(
            num_scalar_prefetch=1,
            grid=(nb,),
            in_specs=[idx_spec(cur), idx_spec(nxt), idx_spec(nx2), idx_spec(prv), idx_spec(cur),
                      pl.BlockSpec(memory_space=pl.ANY),
                      pl.BlockSpec((1, d, de), expert),
                      pl.BlockSpec((1, d, de), expert),
                      pl.BlockSpec((1, de, d), expert)],
            out_specs=pl.BlockSpec(memory_space=pl.ANY),
            scratch_shapes=[vm] * 6 + [pltpu.SemaphoreType.DMA((3,)), pltpu.SemaphoreType.DMA((3,))],
        ),
        out_shape=jax.ShapeDtypeStruct((nb * TB, dh), jnp.uint32),
        compiler_params=_params(("arbitrary",)),
        name="moe",
    )(block_e, src3, src3, src3, slot3, slot3, h2, wg, wu, wd)


def _combine_kernel(y0_ref, y1_ref, x1_ref, route_ref, mod_ref, gf_ref, o_ref, *, final_norm):
    d = x1_ref.shape[1]
    y = route_ref[:, 2:3] * _unpack_halves(y0_ref[...]) + route_ref[:, 3:4] * _unpack_halves(y1_ref[...])
    x2 = x1_ref[...] + mod_ref[:, 5 * d:6 * d] * y
    if final_norm:
        x2 = x2 * lax.rsqrt(jnp.mean(x2 * x2, axis=-1, keepdims=True) + EPS) * gf_ref[...]
    o_ref[...] = x2


def _combine(y_rows, x1, route, mod3, gf, n_batch, final_norm):
    nt, d = x1.shape
    nj = nt // n_batch // TM
    ctx_row = n_batch

    def modrow(i):
        return (jnp.where(i % nj == 0, ctx_row, i // nj), 0, 0)

    if final_norm:
        out_rows = nt - n_batch * TM
        out_map = lambda i: ((i // nj) * (nj - 1) + jnp.maximum(i % nj - 1, 0), 0)
    else:
        out_rows = nt
        out_map = lambda i: (i, 0)
    return pl.pallas_call(
        functools.partial(_combine_kernel, final_norm=final_norm),
        grid=(nt // TM,),
        in_specs=[pl.BlockSpec((TM, d // 2), lambda i: (i, 0)),
                  pl.BlockSpec((TM, d // 2), lambda i: (nt // TM + i, 0)),
                  pl.BlockSpec((TM, d), lambda i: (i, 0)),
                  pl.BlockSpec((TM, LANES), lambda i: (i, 0)),
                  pl.BlockSpec((None, 1, mod3.shape[2]), modrow),
                  _const_spec(gf.shape)],
        out_specs=pl.BlockSpec((TM, d), out_map),
        out_shape=jax.ShapeDtypeStruct((out_rows, d), F32),
        compiler_params=_params(("arbitrary",)),
        name="combine",
    )(y_rows, y_rows, x1, route, mod3, gf)


def kernel(x, c, ctx, c_ctx, w_mod, b_mod, g_norm1, g_norm2, w_in, w_gate2, b_gate2, gla_norm_g,
           sgu_ln_g, sgu_ln_b, sgu_w, sgu_b, w_branch_a, w_branch_b, b_branch, w_out,
           w_router, b_router, w_exp_gate, w_exp_up, w_exp_down, g_final):
    n_batch, seq, d = x.shape
    ctx_len = ctx.shape[1]
    depth = w_mod.shape[0]
    dk = w_gate2.shape[3]
    rank_lr = w_gate2.shape[2]
    assert ctx_len == TM and seq % TM == 0 and n_batch < MOD_ROWS

    nt = n_batch * (ctx_len + seq)
    streams = (ctx, x, False)
    pending = None
    n_blocks = TOP_K * nt // TB + N_EXPERTS
    cc = jnp.zeros((MOD_ROWS, d), F32).at[:n_batch].set(c).at[n_batch].set(c_ctx)
    mod = _modulation(cc, w_mod, b_mod)

    wr = jnp.zeros((d, LANES), F32).at[:, :N_EXPERTS].set(w_router)
    wr_hi = wr.astype(BF16)
    wr = jnp.stack([wr_hi, (wr - wr_hi.astype(F32)).astype(BF16)])
    br = jnp.zeros((1, LANES), F32).at[0, :N_EXPERTS].set(b_router)
    row = lambda a: a.reshape(1, -1)
    n_main = 2 * d + 2 * dk + 2 * d

    out = None
    for l in range(depth):
        last = l == depth - 1
        mod3 = mod[l].reshape(MOD_ROWS, 1, 6 * d)
        wmain = w_in[l][:, :n_main].astype(BF16)
        wlr = w_in[l][:, n_main:n_main + 2 * rank_lr].astype(BF16)
        wgates = w_in[l][:, n_main + 2 * rank_lr:].astype(BF16)
        wg2 = jnp.zeros((2 * rank_lr, 2 * dk), F32)
        wg2 = wg2.at[:rank_lr, :dk].set(w_gate2[l, 0]).at[rank_lr:, dk:].set(w_gate2[l, 1]).astype(BF16)
        bg2 = b_gate2[l].reshape(1, 2 * dk)
        sgub = jnp.repeat(sgu_b[l].T, d // A_GROUPS, axis=1)

        yag, gb, q, k, v, sr, la, *x2 = _inproj(
            streams[0], streams[1], pending, nt, mod3, row(g_norm1[l]), wmain, wlr, wgates, wg2, bg2,
            row(sgu_ln_g[l]), row(sgu_ln_b[l]), sgu_w[l].astype(BF16), sgub, w_branch_a[l].astype(BF16),
            row(b_branch[l]), n_batch)
        if x2:
            x2 = x2[0].reshape(n_batch, -1, d)
            streams = (x2, x2, True)
        ob, (wg, wu, wd) = _gla_bwd(q, k, v, la, n_batch, (w_exp_gate, w_exp_up, w_exp_down), l)
        x1, h2, route, route_t = _merge(
            *streams, nt, mod3, q, k, v, la, ob, yag, gb, sr, row(gla_norm_g[l]), w_branch_b[l].astype(BF16),
            w_out[l].astype(BF16), row(g_norm2[l]), wr, br, n_batch)
        slot_raw, block_e, cnt, pst = _plan(route_t, n_blocks)
        slot3, src3 = _sorted_rows(slot_raw, block_e, cnt, pst, nt, n_blocks)
        y_rows = _moe(block_e.reshape(-1), slot3, src3, h2, wg, wu, wd)
        if last:
            out = _combine(y_rows, x1, route, mod3, row(g_final), n_batch, final_norm=True).reshape(n_batch, seq, d)
        else:
            pending = (x1, y_rows, route, mod3)
    return out
```

```python
import functools

import jax
import jax.numpy as jnp
from jax import lax
from jax.experimental import pallas as pl
from jax.experimental.pallas import tpu as pltpu

F32 = jnp.float32
BF16 = jnp.bfloat16
HIGHEST = lax.Precision.HIGHEST

A_CHUNK = 128
A_GROUPS = 8
GLA_HEADS = 4
GLA_TAU = 16.0
GLA_CHUNK = 64
N_EXPERTS = 16
N_EXPERT_GROUPS = 4
EXPERTS_PER_GROUP = N_EXPERTS // N_EXPERT_GROUPS
TOP_K = 2
EPS = 1e-6

LANES = 128
SUBLANES = 8
TM = 256
TB = 256
BG = 2
CAST_STEPS = 64
MOD_ROWS = 16
VMEM_LIMIT = 56 * 1024 * 1024


def _dot(a, b):
    return jnp.dot(a, b, preferred_element_type=F32)


def _pack_halves(x):
    n = x.shape[1] // 2
    bits = lambda v: pltpu.bitcast(v.astype(BF16).astype(F32), jnp.uint32)
    return bits(x[:, :n]) | (bits(x[:, n:]) >> 16)


def _unpack_halves(p):
    hi = pltpu.bitcast(p & jnp.uint32(0xFFFF0000), F32)
    lo = pltpu.bitcast(p << 16, F32)
    return jnp.concatenate([hi, lo], axis=1)


def _const_spec(shape):
    nd = len(shape)
    return pl.BlockSpec(shape, lambda *_: (0,) * nd)


def _params(sem):
    return pltpu.CompilerParams(dimension_semantics=sem, vmem_limit_bytes=VMEM_LIMIT)


def _mod_kernel(cc_ref, w_ref, b_ref, o_ref):
    cc = cc_ref[...]
    s = cc * jax.nn.sigmoid(cc)
    o_ref[0] = jnp.dot(s, w_ref[0], preferred_element_type=F32, precision=HIGHEST) + b_ref[0]


def _modulation(cc, w_mod, b_mod):
    n_layer, d, six_d = w_mod.shape
    return pl.pallas_call(
        _mod_kernel,
        grid=(n_layer, six_d // d),
        in_specs=[
            pl.BlockSpec((MOD_ROWS, d), lambda l, j: (0, 0)),
            pl.BlockSpec((1, d, d), lambda l, j: (l, 0, j)),
            pl.BlockSpec((1, 1, d), lambda l, j: (l, 0, j)),
        ],
        out_specs=pl.BlockSpec((1, MOD_ROWS, d), lambda l, j: (l, 0, j)),
        out_shape=jax.ShapeDtypeStruct((n_layer, MOD_ROWS, six_d), F32),
        compiler_params=_params(("arbitrary", "arbitrary")),
        name="modulation",
    )(cc, w_mod, b_mod.reshape(n_layer, 1, six_d))


def _gelu_tanh(x):
    c = 0.7978845608028654
    return 0.5 * x * (1.0 + jnp.tanh(c * (x + 0.044715 * (x * x * x))))


def _log_sigmoid(z):
    return jnp.minimum(z, 0.0) - jnp.log1p(jnp.exp(-jnp.abs(z)))


def _inproj_stages(x_of, modc_ref, modl_ref, g1_ref, wmain_ref, wlr_ref, wgates_ref, wg2_ref, bg2_ref,
                   lng_ref, lnb_ref, sguw_ref, sgub_ref, wa_ref, bbr_ref,
                   yag_ref, gb_ref, q_ref, k_ref, v_ref, sr_ref, la_ref, sa_ref):
    first = pl.program_id(1) == 0
    n_g, rows, d = yag_ref.shape
    dk = q_ref.shape[2]
    gdim = d // A_GROUPS
    head_k = dk // GLA_HEADS
    c_q, c_v = 2 * d, 2 * d + 2 * dk
    hb, u, vnb, ya = {}, {}, {}, {}

    def norm(g):
        mod = jnp.where(first, modc_ref[0], modl_ref[g])
        x = x_of(g)
        h = x * lax.rsqrt(jnp.mean(x * x, axis=-1, keepdims=True) + EPS) * g1_ref[...]
        hb[g] = (h * (1.0 + mod[:, d:2 * d]) + mod[:, 0:d]).astype(BF16)

    def gate_u(g):
        u[g] = _gelu_tanh(_dot(hb[g], wmain_ref[:, 0:d]))

    def gate_v(g):
        vv = _gelu_tanh(_dot(hb[g], wmain_ref[:, d:2 * d]))
        vc = vv - jnp.mean(vv, axis=-1, keepdims=True)
        vn = vc * lax.rsqrt(jnp.mean(vc * vc, axis=-1, keepdims=True) + EPS) * lng_ref[...] + lnb_ref[...]
        vnb[g] = vn.astype(BF16)

    def spatial(g):
        for n in range(rows // A_CHUNK):
            rs = slice(n * A_CHUNK, (n + 1) * A_CHUNK)
            for a in range(A_GROUPS):
                cs = slice(a * gdim, (a + 1) * gdim)
                mixed = _dot(sguw_ref[a], vnb[g][rs, cs]) + sgub_ref[:, cs]
                sa_ref[g, rs, cs] = (u[g][rs, cs] * mixed).astype(BF16)

    def proj_a(g):
        ya[g] = _dot(sa_ref[g], wa_ref[...])

    def branch_gates(g):
        gates = _dot(hb[g], wgates_ref[...]) + bbr_ref[...]
        yag_ref[g] = (jax.nn.sigmoid(gates[:, 0:d]) * ya[g]).astype(BF16)
        gb_ref[g] = jax.nn.sigmoid(gates[:, d:2 * d]).astype(BF16)

    def qk(g):
        q_ref[g] = (_dot(hb[g], wmain_ref[:, c_q:c_q + dk]) * (head_k ** -0.5)).astype(BF16)
        k_ref[g] = _dot(hb[g], wmain_ref[:, c_q + dk:c_q + 2 * dk]).astype(BF16)

    def val(g):
        v_ref[g] = _dot(hb[g], wmain_ref[:, c_v:c_v + d]).astype(BF16)

    def out_gate(g):
        r = _dot(hb[g], wmain_ref[:, c_v + d:c_v + 2 * d])
        sr_ref[g] = (r * jax.nn.sigmoid(r)).astype(BF16)

    def decay(g):
        lr = _dot(hb[g], wlr_ref[...]).astype(BF16)
        z = _dot(lr, wg2_ref[...]) + bg2_ref[...]
        la_ref[g] = _log_sigmoid(z) * (1.0 / GLA_TAU)

    order = [norm, gate_u, decay, gate_v, qk, spatial, out_gate, proj_a, branch_gates, val]
    for stage in order:
        for g in range(n_g):
            stage(g)


def _inproj_kernel(xc_ref, xl_ref, *rest):
    first = pl.program_id(1) == 0
    _inproj_stages(lambda g: jnp.where(first, xc_ref[g], xl_ref[g]), *rest)


def _inproj_after_moe_kernel(x1_ref, y00_ref, y01_ref, y10_ref, y11_ref, route_ref, pmodc_ref, pmodl_ref, *rest):
    *rest, x2_ref, sa_ref = rest
    first = pl.program_id(1) == 0
    d = x1_ref.shape[2]
    y_refs = ((y00_ref, y10_ref), (y01_ref, y11_ref))

    def x_of(g):
        pmod = jnp.where(first, pmodc_ref[0], pmodl_ref[g])
        y = (route_ref[g][:, 2:3] * _unpack_halves(y_refs[g][0][...])
             + route_ref[g][:, 3:4] * _unpack_halves(y_refs[g][1][...]))
        x2 = x1_ref[g] + pmod[:, 5 * d:6 * d] * y
        x2_ref[g] = x2
        return x2

    _inproj_stages(x_of, *rest, sa_ref)


def _stream_specs(joined, ctx_row, six_d, d):
    lat_map = (lambda bp, j: (bp, jnp.maximum(j, 1), 0)) if joined else (lambda bp, j: (bp, jnp.maximum(j - 1, 0), 0))
    return [pl.BlockSpec((BG, TM, d), lambda bp, j: (bp, 0, 0)), pl.BlockSpec((BG, TM, d), lat_map),
            pl.BlockSpec((1, 1, six_d), lambda bp, j: (ctx_row, 0, 0)),
            pl.BlockSpec((BG, 1, six_d), lambda bp, j: (bp, 0, 0))]


def _resident_spec(shape):
    nd = len(shape)
    return pl.BlockSpec(shape, lambda *_: (0,) * nd, pipeline_mode=pl.Buffered(1))


def _inproj(xc, xl, pending, nt, mod3, g1, wmain, wlr, wgates, wg2, bg2, lng, lnb, sguw, sgub, wa, bbr, n_batch):
    assert BG == 2
    dk = wg2.shape[1] // 2
    t_all = nt // n_batch
    nj = t_all // TM
    six_d = mod3.shape[2]
    d = six_d // 6
    tile = lambda w: pl.BlockSpec((BG, TM, w), lambda bp, j: (bp, j, 0))
    outs = [(d, BF16), (d, BF16), (dk, BF16), (dk, BF16), (d, BF16), (d, BF16), (2 * dk, F32)]
    weights = (g1, wmain, wlr, wgates, wg2, bg2, lng, lnb, sguw, sgub, wa, bbr)
    mod_specs = lambda: [pl.BlockSpec((1, 1, six_d), lambda bp, j: (n_batch, 0, 0)),
                         pl.BlockSpec((BG, 1, six_d), lambda bp, j: (bp, 0, 0))]
    if pending is None:
        kern = _inproj_kernel
        operands = (xc, xl, mod3, mod3)
        in_specs = _stream_specs(False, n_batch, six_d, d)
    else:
        x1, y_rows, route, mod3_prev = pending
        kern = _inproj_after_moe_kernel
        outs = outs + [(d, F32)]
        y_spec = lambda slot, g: pl.BlockSpec(
            (TM, d // 2), lambda bp, j: (slot * (nt // TM) + (bp * BG + g) * nj + j, 0))
        operands = (x1.reshape(n_batch, t_all, d), y_rows, y_rows, y_rows, y_rows,
                    route.reshape(n_batch, t_all, LANES), mod3_prev, mod3_prev, mod3, mod3)
        in_specs = ([tile(d), y_spec(0, 0), y_spec(0, 1), y_spec(1, 0), y_spec(1, 1), tile(LANES)]
                    + mod_specs() + mod_specs())
    res = pl.pallas_call(
        kern,
        grid=(n_batch // BG, nj),
        in_specs=in_specs + [_resident_spec(a.shape) for a in weights],
        out_specs=[tile(w) for w, _ in outs],
        out_shape=[jax.ShapeDtypeStruct((n_batch, t_all, w), t) for w, t in outs],
        scratch_shapes=[pltpu.VMEM((BG, TM, d), BF16)],
        compiler_params=_params(("arbitrary", "arbitrary")),
        name="inproj",
    )(*operands, *weights)
    return [a.reshape(nt, a.shape[2]) for a in res]


def _gla_tiles(q_refs, k_refs, v_refs, la_refs, s_refs, o_refs, qin_sc, kv_sc, dm_sc, reverse):
    n_g = len(q_refs)
    rows, dk = q_refs[0].shape
    dv = v_refs[0].shape[1]
    hk = dk // GLA_HEADS
    hv = dv // GLA_HEADS
    c = GLA_CHUNK
    ri = lax.broadcasted_iota(jnp.int32, (c, c), 0)
    ci = lax.broadcasted_iota(jnp.int32, (c, c), 1)
    tri = (ci >= ri) if reverse else (ci <= ri)
    rt = lax.broadcasted_iota(jnp.int32, (rows, rows), 0)
    ct = lax.broadcasted_iota(jnp.int32, (rows, rows), 1)
    in_chunk = (rt // c) == (ct // c)
    tri_t = jnp.where(jnp.logical_and(in_chunk, (ct >= rt) if reverse else (ct <= rt)), 1.0, 0.0).astype(BF16)
    chunks = list(range(rows // c))
    for g in range(n_g):
        la = la_refs[g][...]
        la_hi = la.astype(BF16)
        la_lo = (la - la_hi.astype(F32)).astype(BF16)
        b_all = _dot(tri_t, la_hi) + _dot(tri_t, la_lo)
        for n in chunks:
            rs = slice(n * c, (n + 1) * c)
            b = b_all[rs, :]
            b_end = b[0:1, :] if reverse else b[c - 1:c, :]
            q = q_refs[g][rs, :].astype(F32)
            k = k_refs[g][rs, :].astype(F32)
            q_in = (q * jnp.exp(b)).astype(BF16)
            k_in = (k * jnp.exp(-b)).astype(BF16)
            k_st = (k * jnp.exp(b_end - b)).astype(BF16)
            decay = jnp.exp(b_end)
            qin_sc[g, rs, :] = q_in
            for h in range(GLA_HEADS):
                ks = slice(h * hk, (h + 1) * hk)
                vs = slice(h * hv, (h + 1) * hv)
                vh = v_refs[g][rs, vs]
                att = lax.dot_general(q_in[:, ks], k_in[:, ks], (((1,), (1,)), ((), ())),
                                      preferred_element_type=F32)
                o_refs[g][rs, vs] = _dot(jnp.where(tri, att, 0.0).astype(BF16), vh)
                kv_sc[g, n, h] = lax.dot_general(k_st[:, ks], vh, (((0,), (0,)), ((), ())),
                                                 preferred_element_type=F32)
                dm_sc[g, n, h] = jnp.transpose(jnp.broadcast_to(decay[:, ks], (hk, hk)))
    for n in (reversed(chunks) if reverse else chunks):
        rs = slice(n * c, (n + 1) * c)
        for g in range(n_g):
            for h in range(GLA_HEADS):
                ks = slice(h * hk, (h + 1) * hk)
                vs = slice(h * hv, (h + 1) * hv)
                state = s_refs[g][h]
                o_refs[g][rs, vs] += _dot(qin_sc[g, rs, ks], state.astype(BF16))
                dmat = jnp.concatenate([dm_sc[g, n, h]] * (hv // hk), axis=1)
                s_refs[g][h] = dmat * state + kv_sc[g, n, h]


def _gla_scratch(n_g, rows, dk, dv):
    hk, hv, n_chunks = dk // GLA_HEADS, dv // GLA_HEADS, rows // GLA_CHUNK
    return [pltpu.VMEM((n_g, GLA_HEADS, hk, hv), F32),
            pltpu.VMEM((n_g, rows, dv), F32),
            pltpu.VMEM((n_g, rows, dk), BF16),
            pltpu.VMEM((n_g, n_chunks, GLA_HEADS, hk, hv), F32),
            pltpu.VMEM((n_g, n_chunks, GLA_HEADS, hk, hk), F32)]


def _per_batch(ref):
    return [ref.at[g] for g in range(ref.shape[0])]


def _gla_bwd_kernel(q_ref, k_ref, v_ref, la_ref, *rest, n_cast):
    if n_cast:
        wg_ref, wu_ref, wd_ref, ob_ref, wgb_ref, wub_ref, wdb_ref, *scratch = rest
    else:
        ob_ref, *scratch = rest
    s_ref, o_sc, qin_sc, kv_sc, dm_sc = scratch

    @pl.when(pl.program_id(1) == 0)
    def _():
        s_ref[...] = jnp.zeros_like(s_ref)

    _gla_tiles(_per_batch(q_ref), _per_batch(k_ref), _per_batch(v_ref), _per_batch(la_ref),
               _per_batch(s_ref), _per_batch(o_sc), qin_sc, kv_sc, dm_sc, reverse=True)
    ob_ref[...] = o_sc[...].astype(BF16)

    if n_cast:
        @pl.when(pl.program_id(0) * pl.num_programs(1) + pl.program_id(1) < n_cast)
        def _():
            wgb_ref[...] = wg_ref[...].astype(BF16)
            wub_ref[...] = wu_ref[...].astype(BF16)
            wdb_ref[...] = wd_ref[...].astype(BF16)


def _gla_bwd(q, k, v, la, n_batch, experts, layer):
    nt, dk = q.shape
    dv = v.shape[1]
    t_all = nt // n_batch
    nj = t_all // TM
    per_batch = lambda a: a.reshape(n_batch, t_all, a.shape[1])
    steps = n_batch // BG * nj
    n_cast = CAST_STEPS if steps >= CAST_STEPS else 0

    def tile(bp, jj):
        return (bp, jnp.where(jj == 0, 0, nj - jj), 0)

    def tile_la(bp, jj):
        return (bp, jnp.where(jj == 0, 0, nj - jj), 1)

    in_specs = [pl.BlockSpec((BG, TM, dk), tile), pl.BlockSpec((BG, TM, dk), tile),
                pl.BlockSpec((BG, TM, dv), tile), pl.BlockSpec((BG, TM, dk), tile_la)]
    out_specs = [pl.BlockSpec((BG, TM, dv), tile)]
    out_shape = [jax.ShapeDtypeStruct((n_batch, t_all, dv), BF16)]
    operands = [per_batch(q), per_batch(k), per_batch(v), per_batch(la)]
    for w in experts if n_cast else ():
        rows = w.shape[1] * w.shape[2]
        blk = (rows // n_cast, w.shape[3])
        operands.append(w.reshape(-1, w.shape[3]))
        in_specs.append(pl.BlockSpec(
            blk, lambda bp, jj: (layer * n_cast + jnp.minimum(bp * nj + jj, n_cast - 1), 0)))
        out_specs.append(pl.BlockSpec(blk, lambda bp, jj: (jnp.minimum(bp * nj + jj, n_cast - 1), 0)))
        out_shape.append(jax.ShapeDtypeStruct((rows, w.shape[3]), BF16))
    ob, *cast = pl.pallas_call(
        functools.partial(_gla_bwd_kernel, n_cast=n_cast),
        grid=(n_batch // BG, nj),
        in_specs=in_specs,
        out_specs=out_specs,
        out_shape=out_shape,
        scratch_shapes=_gla_scratch(BG, TM, dk, dv),
        compiler_params=_params(("arbitrary", "arbitrary")),
        name="gla_bwd",
    )(*operands)
    if n_cast:
        cast = [c.reshape(w.shape[1:]) for c, w in zip(cast, experts)]
    else:
        cast = [w[layer].astype(BF16) for w in experts]
    return ob.reshape(nt, dv), cast


def _route(logits_t):
    n_tok = logits_t.shape[1]
    eid = lax.broadcasted_iota(jnp.int32, logits_t.shape, 0)
    ex = jnp.exp(logits_t - jnp.max(logits_t, axis=0, keepdims=True))
    p = ex / jnp.sum(ex, axis=0, keepdims=True)
    grp = eid // EXPERTS_PER_GROUP
    none = -1.0
    far = 2 * N_EXPERTS
    best = None
    for g in range(N_EXPERT_GROUPS):
        pg = jnp.where(grp == g, p, none)
        m1 = jnp.max(pg, axis=0, keepdims=True)
        i1 = jnp.min(jnp.where(pg == m1, eid, far), axis=0, keepdims=True)
        pg2 = jnp.where(eid == i1, none, pg)
        m2 = jnp.max(pg2, axis=0, keepdims=True)
        i2 = jnp.min(jnp.where(pg2 == m2, eid, far), axis=0, keepdims=True)
        cand = (m1 + m2, m1, i1, m2, i2)
        if best is None:
            best = cand
        else:
            better = cand[0] > best[0]
            best = tuple(jnp.where(better, c, o) for c, o in zip(cand, best))
    _, m1, i1, m2, i2 = best
    tot = m1 + m2
    sub = lax.broadcasted_iota(jnp.int32, (SUBLANES, n_tok), 0)
    out = jnp.where(sub == 0, i1.astype(F32), 0.0)
    out = jnp.where(sub == 1, i2.astype(F32), out)
    out = jnp.where(sub == 2, m1 / tot, out)
    out = jnp.where(sub == 3, m2 / tot, out)
    return out


def _merge_kernel(xc_ref, xl_ref, modc_ref, modl_ref, q_ref, k_ref, v_ref, la_ref, ob_ref, yag_ref, gb_ref,
                  sr_ref, glag_ref, wb_ref, wout_ref, g2_ref, wr_ref, br_ref,
                  x1_ref, h2_ref, route_ref, routet_ref, s_ref, o_sc, qin_sc, kv_sc, dm_sc):
    first = pl.program_id(1) == 0

    @pl.when(first)
    def _():
        s_ref[...] = jnp.zeros_like(s_ref)

    _gla_tiles(_per_batch(q_ref), _per_batch(k_ref), _per_batch(v_ref), _per_batch(la_ref),
               _per_batch(s_ref), _per_batch(o_sc), qin_sc, kv_sc, dm_sc, reverse=False)

    d = xc_ref.shape[2]
    hv = d // GLA_HEADS
    mod, yb_in, mix, h2, logits = {}, {}, {}, {}, {}

    def head_norm(g):
        mod[g] = jnp.where(first, modc_ref[0], modl_ref[g])
        o = o_sc[g] + ob_ref[g].astype(F32)
        parts = []
        for h in range(GLA_HEADS):
            oh = o[:, h * hv:(h + 1) * hv]
            parts.append(oh * lax.rsqrt(jnp.mean(oh * oh, axis=-1, keepdims=True) + EPS))
        on = jnp.concatenate(parts, axis=1) * glag_ref[...]
        yb_in[g] = (on * sr_ref[g].astype(F32)).astype(BF16)

    def branch_merge(g):
        yb = _dot(yb_in[g], wb_ref[...])
        mix[g] = (yag_ref[g].astype(F32) + gb_ref[g].astype(F32) * yb).astype(BF16)

    def residual(g):
        y = _dot(mix[g], wout_ref[...])
        x1 = jnp.where(first, xc_ref[g], xl_ref[g]) + mod[g][:, 2 * d:3 * d] * y
        x1_ref[g] = x1
        h = x1 * lax.rsqrt(jnp.mean(x1 * x1, axis=-1, keepdims=True) + EPS) * g2_ref[...]
        h2[g] = h * (1.0 + mod[g][:, 4 * d:5 * d]) + mod[g][:, 3 * d:4 * d]
        h2_ref[g] = _pack_halves(h2[g])

    def router_logits(g):
        h_hi = h2[g].astype(BF16)
        h_lo = (h2[g] - h_hi.astype(F32)).astype(BF16)
        logits[g] = _dot(h_hi, wr_ref[0]) + _dot(h_lo, wr_ref[0]) + _dot(h_hi, wr_ref[1]) + br_ref[...]

    def routing(g):
        rt = _route(jnp.transpose(logits[g])[0:N_EXPERTS, :])
        routet_ref[g, 0] = rt
        sub = lax.broadcasted_iota(jnp.int32, (LANES, rt.shape[1]), 0)
        padded = jnp.zeros((LANES, rt.shape[1]), F32)
        for r in range(4):
            padded = jnp.where(sub == r, rt[r:r + 1, :], padded)
        route_ref[g] = jnp.transpose(padded)

    for stage in (head_norm, branch_merge, residual, router_logits, routing):
        for g in range(xc_ref.shape[0]):
            stage(g)


def _merge(xc, xl, joined, nt, mod3, q, k, v, la, ob, yag, gb, sr, glag, wb, wout, g2, wr, br, n_batch):
    d = xc.shape[2]
    dk = q.shape[1]
    t_all = nt // n_batch
    nj = t_all // TM
    per_batch = lambda a: a.reshape(n_batch, t_all, a.shape[1])
    tile = lambda w: pl.BlockSpec((BG, TM, w), lambda bp, j: (bp, j, 0))
    x1, h2, route, route_t = pl.pallas_call(
        _merge_kernel,
        grid=(n_batch // BG, nj),
        in_specs=_stream_specs(joined, n_batch, mod3.shape[2], d)
        + [tile(dk), tile(dk), tile(d), tile(dk), tile(d), tile(d), tile(d), tile(d)]
        + [_const_spec(a.shape) for a in (glag, wb, wout, g2, wr, br)],
        out_specs=[tile(d), tile(d // 2), tile(LANES),
                   pl.BlockSpec((BG, 1, SUBLANES, TM), lambda bp, j: (bp, j, 0, 0))],
        out_shape=[jax.ShapeDtypeStruct((n_batch, t_all, d), F32),
                   jax.ShapeDtypeStruct((n_batch, t_all, d // 2), jnp.uint32),
                   jax.ShapeDtypeStruct((n_batch, t_all, LANES), F32),
                   jax.ShapeDtypeStruct((n_batch, nj, SUBLANES, TM), F32)],
        scratch_shapes=_gla_scratch(BG, TM, dk, d),
        compiler_params=_params(("arbitrary", "arbitrary")),
        name="merge",
    )(xc, xl, mod3, mod3, per_batch(q), per_batch(k), per_batch(v), per_batch(la), per_batch(ob),
      per_batch(yag), per_batch(gb), per_batch(sr), glag, wb, wout, g2, wr, br)
    return (x1.reshape(nt, d), h2.reshape(nt, d // 2), route.reshape(nt, LANES),
            route_t.reshape(nt // TM, SUBLANES, TM))


def _plan_kernel(rt_ref, slot_ref, be_ref, cnt_ref, pst_ref):
    n_tiles, _, rows = rt_ref.shape
    n_tok = n_tiles * rows
    n_slot = TOP_K * rows
    sub = lax.broadcasted_iota(jnp.int32, (N_EXPERTS, rows), 0).astype(F32)
    wide = lambda col: jnp.broadcast_to(col, (N_EXPERTS, LANES))

    def one_hots(i):
        rt = rt_ref[i]
        return jnp.where(sub == rt[0:1, :], 1.0, 0.0), jnp.where(sub == rt[1:2, :], 1.0, 0.0)

    def count(i, acc):
        oh0, oh1 = one_hots(i)
        return acc + jnp.sum(oh0 + oh1, axis=1, keepdims=True)

    cnt = wide(lax.fori_loop(0, n_tiles, count, jnp.zeros((N_EXPERTS, 1), F32)))
    padded = jnp.floor((cnt + (TB - 1)) * (1.0 / TB)) * TB
    ri = lax.broadcasted_iota(jnp.int32, (N_EXPERTS, N_EXPERTS), 0)
    ci = lax.broadcasted_iota(jnp.int32, (N_EXPERTS, N_EXPERTS), 1)
    p_end = jnp.dot(jnp.where(ci <= ri, 1.0, 0.0), padded, preferred_element_type=F32, precision=HIGHEST)
    p_start = p_end - padded
    cnt_ref[...] = cnt
    pst_ref[...] = p_start
    starts = lax.broadcasted_iota(jnp.int32, (N_EXPERTS, be_ref.shape[1]), 1).astype(F32) * TB
    done = jnp.sum(jnp.where(p_end[:, 0:1] <= starts, 1.0, 0.0), axis=0, keepdims=True)
    be_ref[...] = jnp.minimum(done, N_EXPERTS - 1.0).astype(jnp.int32)
    slot_ref[...] = jnp.full(slot_ref.shape, -1, jnp.int32)

    rr = lax.broadcasted_iota(jnp.int32, (rows, rows), 0)
    cc = lax.broadcasted_iota(jnp.int32, (rows, rows), 1)
    earlier = jnp.where(rr < cc, 1.0, 0.0).astype(BF16)
    before = jnp.where(ci < ri, 1.0, 0.0)
    pos = lax.broadcasted_iota(jnp.int32, (n_slot, n_slot), 0).astype(F32)
    s_idx = lax.broadcasted_iota(jnp.int32, (SUBLANES, n_slot), 1)
    s_sub = lax.broadcasted_iota(jnp.int32, (SUBLANES, n_slot), 0)
    tok_slot = jnp.where(s_sub == 0, s_idx % rows, jnp.where(s_sub == 1, s_idx // rows, 0)).astype(BF16)
    lane = lax.broadcasted_iota(jnp.int32, (1, TB), 1)
    p_start_i = p_start.astype(jnp.int32)

    n_win = -(-(TB + n_slot) // TB)
    spare = slot_ref.shape[0] - 1
    empty = jnp.full((1, TB), -1, jnp.int32)

    def sort_tile(i):
        oh0, oh1 = one_hots(i)
        c0 = _dot(oh0.astype(BF16), earlier)
        c1 = _dot(oh1.astype(BF16), earlier)
        tot0 = jnp.sum(oh0, axis=1, keepdims=True)
        tot = tot0 + jnp.sum(oh1, axis=1, keepdims=True)
        off = jnp.dot(before, wide(tot), preferred_element_type=F32, precision=HIGHEST)[:, 0:1]
        lp = jnp.concatenate([jnp.sum(oh0 * (c0 + off), axis=0, keepdims=True),
                              jnp.sum(oh1 * (c1 + off + tot0), axis=0, keepdims=True)], axis=1)
        perm = jnp.where(pos == lp, 1.0, 0.0).astype(BF16)
        srt = lax.dot_general(tok_slot, perm, (((1,), (1,)), ((), ())), preferred_element_type=F32)
        ids = (srt[0:1, :] + srt[1:2, :] * n_tok).astype(jnp.int32) + i * rows
        ids = jnp.broadcast_to(jnp.concatenate([ids, jnp.zeros_like(ids)], axis=1), (SUBLANES, 2 * n_slot))
        return ids, tot, off

    def place(sorted_tile, carry):
        ids, tot, off = sorted_tile
        run, open_rows = carry
        tot_i, run_i, off_i = tot.astype(jnp.int32), run.astype(jnp.int32), off.astype(jnp.int32)
        new_rows = []
        for e in range(N_EXPERTS):
            n_e = tot_i[e, 0]
            row0 = p_start_i[e, 0] + run_i[e, 0]
            blk = row0 // TB
            at = row0 - blk * TB
            end = at + n_e
            moved = pltpu.roll(ids, at - off_i[e, 0] + 2 * n_slot, axis=1)
            keep = open_rows[e]
            for w in range(n_win):
                here = jnp.logical_and(lane + w * TB >= at, lane + w * TB < end)
                row = jnp.where(here, moved[0:1, w * TB:(w + 1) * TB], open_rows[e] if w == 0 else empty)
                slot_ref[pl.ds(blk if w == 0 else jnp.where(end > w * TB, blk + w, spare), 1), :] = row
                if w > 0:
                    keep = jnp.where(end >= w * TB, row, keep)
                else:
                    keep = row
            new_rows.append(keep)
        return run + tot, tuple(new_rows)

    def two_tiles(j, carry):
        first, second = sort_tile(2 * j), sort_tile(2 * j + 1)
        return place(second, place(first, carry))

    lax.fori_loop(0, n_tiles // 2, two_tiles, (jnp.zeros((N_EXPERTS, 1), F32), (empty,) * N_EXPERTS))


def _plan(route_t, n_blocks):
    nbp = -(-(n_blocks + TOP_K * TM // TB) // LANES) * LANES
    assert route_t.shape[0] % 2 == 0
    small = jax.ShapeDtypeStruct((N_EXPERTS, LANES), F32)
    return pl.pallas_call(
        _plan_kernel,
        out_shape=[jax.ShapeDtypeStruct((nbp, TB), jnp.int32),
                   jax.ShapeDtypeStruct((1, nbp), jnp.int32), small, small],
        compiler_params=pltpu.CompilerParams(vmem_limit_bytes=VMEM_LIMIT),
        name="plan",
    )(route_t)


def _sorted_rows(slot_raw, block_e, cnt, pst, n_tok, n_blocks):
    cnt = cnt[:, 0].astype(jnp.int32)
    pst = pst[:, 0].astype(jnp.int32)
    pad = (cnt + TB - 1) // TB * TB - cnt
    pad_before = jnp.cumsum(pad) - pad
    be = block_e[0, :n_blocks]
    base = TOP_K * n_tok - pst[be] - cnt[be] + pad_before[be]
    pad_id = base[:, None] + jnp.arange(n_blocks * TB, dtype=jnp.int32).reshape(n_blocks, TB)
    raw = slot_raw[:n_blocks]
    slot = jnp.where(raw < 0, pad_id, raw)
    src = jnp.where(raw < 0, 0, raw % n_tok)
    return slot.reshape(n_blocks, 1, TB), src.reshape(n_blocks, 1, TB)


def _moe_kernel(be_ref, src_cur, src_nxt, src_nx2, slot_prv, slot_cur, h_hbm, wg_ref, wu_ref, wd_ref, y_hbm,
                x0, x1, x2, y0, y1, y2, gsem, ssem):
    del be_ref
    i = pl.program_id(0)
    last = pl.num_programs(0) - 1
    xs, ys = (x0, x1, x2), (y0, y1, y2)

    def gather_start(idx_ref, dst, sem):
        for r in range(TB):
            pltpu.make_async_copy(h_hbm.at[pl.ds(idx_ref[0, 0, r], 1)], dst.at[pl.ds(r, 1)], sem).start(
                priority=r % 2)

    def gather_wait(dst, sem):
        pltpu.make_async_copy(h_hbm.at[pl.ds(0, TB)], dst, sem).wait()

    def scatter_start(idx_ref, src, sem):
        for r in range(TB):
            pltpu.make_async_copy(src.at[pl.ds(r, 1)], y_hbm.at[pl.ds(idx_ref[0, 0, r], 1)], sem).start(
                priority=r % 2)

    def scatter_wait(src, sem):
        pltpu.make_async_copy(src, y_hbm.at[pl.ds(0, TB)], sem).wait()

    def experts(x_ref, y_ref):
        x = _unpack_halves(x_ref[...]).astype(BF16)
        g = _dot(x, wg_ref[0])
        u = _dot(x, wu_ref[0])
        a = (g * jax.nn.sigmoid(g) * u).astype(BF16)
        y_ref[...] = _pack_halves(_dot(a, wd_ref[0]))

    for p in range(3):
        @pl.when(jnp.logical_and(i > 0, lax.rem(i, 3) == p))
        def _(p=p):
            prv, nx2 = (p + 2) % 3, (p + 2) % 3
            gather_wait(xs[p], gsem.at[p])

            @pl.when(i >= 3)
            def _():
                scatter_wait(ys[p], ssem.at[p])

            gather_start(src_nx2, xs[nx2], gsem.at[nx2])
            scatter_start(slot_prv, ys[prv], ssem.at[prv])
            experts(xs[p], ys[p])

    for p in range(3):
        @pl.when(jnp.logical_and(i == last, lax.rem(i, 3) == p))
        def _(p=p):
            nxt, prv = (p + 1) % 3, (p + 2) % 3
            scatter_start(slot_cur, ys[p], ssem.at[p])
            scatter_wait(ys[nxt], ssem.at[nxt])
            scatter_wait(ys[prv], ssem.at[prv])
            scatter_wait(ys[p], ssem.at[p])
            gather_wait(xs[nxt], gsem.at[nxt])
            gather_wait(xs[prv], gsem.at[prv])

    @pl.when(i == 0)
    def _():
        gather_start(src_cur, x0, gsem.at[0])
        gather_start(src_nxt, x1, gsem.at[1])
        gather_wait(x0, gsem.at[0])
        gather_start(src_nx2, x2, gsem.at[2])
        experts(x0, y0)


def _moe(block_e, slot3, src3, h2, wg, wu, wd):
    nb = slot3.shape[0]
    dh = h2.shape[1]
    d, de = wg.shape[1], wg.shape[2]
    expert = lambda i, be: (be[i], 0, 0)
    idx_spec = lambda f: pl.BlockSpec((1, 1, TB), f, memory_space=pltpu.SMEM)
    assert nb >= 3
    cur = lambda i, be: (i, 0, 0)
    nxt = lambda i, be: (jnp.minimum(i + 1, nb - 1), 0, 0)
    nx2 = lambda i, be: (jnp.minimum(i + 2, nb - 1), 0, 0)
    prv = lambda i, be: (jnp.maximum(i - 1, 0), 0, 0)
    vm = pltpu.VMEM((TB, dh), jnp.uint32)
    return pl.pallas_call(
        _moe_kernel,
        grid_spec=pltpu.PrefetchScalarGridSpec(
            num_scalar_prefetch=1,
            grid=(nb,),
            in_specs=[idx_spec(cur), idx_spec(nxt), idx_spec(nx2), idx_spec(prv), idx_spec(cur),
                      pl.BlockSpec(memory_space=pl.ANY),
                      pl.BlockSpec((1, d, de), expert),
                      pl.BlockSpec((1, d, de), expert),
                      pl.BlockSpec((1, de, d), expert)],
            out_specs=pl.BlockSpec(memory_space=pl.ANY),
            scratch_shapes=[vm] * 6 + [pltpu.SemaphoreType.DMA((3,)), pltpu.SemaphoreType.DMA((3,))],
        ),
        out_shape=jax.ShapeDtypeStruct((nb * TB, dh), jnp.uint32),
        compiler_params=_params(("arbitrary",)),
        name="moe",
    )(block_e, src3, src3, src3, slot3, slot3, h2, wg, wu, wd)


def _combine_kernel(y0_ref, y1_ref, x1_ref, route_ref, mod_ref, gf_ref, o_ref):
    d = x1_ref.shape[1]
    y = route_ref[:, 2:3] * _unpack_halves(y0_ref[...]) + route_ref[:, 3:4] * _unpack_halves(y1_ref[...])
    x2 = x1_ref[...] + mod_ref[:, 5 * d:6 * d] * y
    o_ref[...] = x2 * lax.rsqrt(jnp.mean(x2 * x2, axis=-1, keepdims=True) + EPS) * gf_ref[...]


def _combine(y_rows, x1, route, mod3, gf, n_batch):
    nt, d = x1.shape
    nj = nt // n_batch // TM
    n_lat = n_batch * (nj - 1)
    stream_tile = lambda i: ((i // (nj - 1)) * nj + 1 + i % (nj - 1), 0)
    return pl.pallas_call(
        _combine_kernel,
        grid=(n_lat,),
        in_specs=[pl.BlockSpec((TM, d // 2), lambda i: (i, 0)),
                  pl.BlockSpec((TM, d // 2), lambda i: (n_lat + i, 0)),
                  pl.BlockSpec((TM, d), stream_tile),
                  pl.BlockSpec((TM, LANES), stream_tile),
                  pl.BlockSpec((None, 1, mod3.shape[2]), lambda i: (i // (nj - 1), 0, 0)),
                  _const_spec(gf.shape)],
        out_specs=pl.BlockSpec((TM, d), lambda i: (i, 0)),
        out_shape=jax.ShapeDtypeStruct((n_lat * TM, d), F32),
        compiler_params=_params(("arbitrary",)),
        name="combine",
    )(y_rows, y_rows, x1, route, mod3, gf)


def kernel(x, c, ctx, c_ctx, w_mod, b_mod, g_norm1, g_norm2, w_in, w_gate2, b_gate2, gla_norm_g,
           sgu_ln_g, sgu_ln_b, sgu_w, sgu_b, w_branch_a, w_branch_b, b_branch, w_out,
           w_router, b_router, w_exp_gate, w_exp_up, w_exp_down, g_final):
    n_batch, seq, d = x.shape
    ctx_len = ctx.shape[1]
    depth = w_mod.shape[0]
    dk = w_gate2.shape[3]
    rank_lr = w_gate2.shape[2]
    assert ctx_len == TM and seq % TM == 0 and n_batch < MOD_ROWS

    nt = n_batch * (ctx_len + seq)
    streams = (ctx, x, False)
    pending = None
    cc = jnp.zeros((MOD_ROWS, d), F32).at[:n_batch].set(c).at[n_batch].set(c_ctx)
    mod = _modulation(cc, w_mod, b_mod)

    wr = jnp.zeros((d, LANES), F32).at[:, :N_EXPERTS].set(w_router)
    wr_hi = wr.astype(BF16)
    wr = jnp.stack([wr_hi, (wr - wr_hi.astype(F32)).astype(BF16)])
    br = jnp.zeros((1, LANES), F32).at[0, :N_EXPERTS].set(b_router)
    row = lambda a: a.reshape(1, -1)
    n_main = 2 * d + 2 * dk + 2 * d

    out = None
    for l in range(depth):
        last = l == depth - 1
        mod3 = mod[l].reshape(MOD_ROWS, 1, 6 * d)
        wmain = w_in[l][:, :n_main].astype(BF16)
        wlr = w_in[l][:, n_main:n_main + 2 * rank_lr].astype(BF16)
        wgates = w_in[l][:, n_main + 2 * rank_lr:].astype(BF16)
        wg2 = jnp.zeros((2 * rank_lr, 2 * dk), F32)
        wg2 = wg2.at[:rank_lr, :dk].set(w_gate2[l, 0]).at[rank_lr:, dk:].set(w_gate2[l, 1]).astype(BF16)
        bg2 = b_gate2[l].reshape(1, 2 * dk)
        sgub = jnp.repeat(sgu_b[l].T, d // A_GROUPS, axis=1)

        yag, gb, q, k, v, sr, la, *x2 = _inproj(
            streams[0], streams[1], pending, nt, mod3, row(g_norm1[l]), wmain, wlr, wgates, wg2, bg2,
            row(sgu_ln_g[l]), row(sgu_ln_b[l]), sgu_w[l].astype(BF16), sgub, w_branch_a[l].astype(BF16),
            row(b_branch[l]), n_batch)
        if x2:
            x2 = x2[0].reshape(n_batch, -1, d)
            streams = (x2, x2, True)
        ob, (wg, wu, wd) = _gla_bwd(q, k, v, la, n_batch, (w_exp_gate, w_exp_up, w_exp_down), l)
        x1, h2, route, route_t = _merge(
            *streams, nt, mod3, q, k, v, la, ob, yag, gb, sr, row(gla_norm_g[l]), w_branch_b[l].astype(BF16),
            w_out[l].astype(BF16), row(g_norm2[l]), wr, br, n_batch)
        if last:
            n_tok = n_batch * seq
            route_t = route_t.reshape(n_batch, -1, SUBLANES, TM)[:, ctx_len // TM:].reshape(-1, SUBLANES, TM)
        else:
            n_tok = nt
        n_blocks = TOP_K * n_tok // TB + N_EXPERTS
        slot_raw, block_e, cnt, pst = _plan(route_t, n_blocks)
        slot3, src3 = _sorted_rows(slot_raw, block_e, cnt, pst, n_tok, n_blocks)
        if last:
            src3 = src3 + (src3 // seq + 1) * ctx_len
        y_rows = _moe(block_e.reshape(-1), slot3, src3, h2, wg, wu, wd)
        if last:
            out = _combine(y_rows, x1, route, mod3, row(g_final), n_batch).reshape(n_batch, seq, d)
        else:
            pending = (x1, y_rows, route, mod3)
    return out
```

```python
import functools

import jax
import jax.numpy as jnp
from jax import lax
from jax.experimental import pallas as pl
from jax.experimental.pallas import tpu as pltpu

F32 = jnp.float32
BF16 = jnp.bfloat16
HIGHEST = lax.Precision.HIGHEST

A_CHUNK = 128
A_GROUPS = 8
GLA_HEADS = 4
GLA_TAU = 16.0
GLA_CHUNK = 64
N_EXPERTS = 16
N_EXPERT_GROUPS = 4
EXPERTS_PER_GROUP = N_EXPERTS // N_EXPERT_GROUPS
TOP_K = 2
EPS = 1e-6

LANES = 128
SUBLANES = 8
TM = 256
TB = 256
BG = 2
CAST_STEPS = 64
MOD_ROWS = 16
VMEM_LIMIT = 56 * 1024 * 1024


def _dot(a, b):
    return jnp.dot(a, b, preferred_element_type=F32)


def _pack_halves(x):
    n = x.shape[1] // 2
    bits = lambda v: pltpu.bitcast(v.astype(BF16).astype(F32), jnp.uint32)
    return bits(x[:, :n]) | (bits(x[:, n:]) >> 16)


def _unpack_halves(p):
    hi = pltpu.bitcast(p & jnp.uint32(0xFFFF0000), F32)
    lo = pltpu.bitcast(p << 16, F32)
    return jnp.concatenate([hi, lo], axis=1)


def _const_spec(shape):
    nd = len(shape)
    return pl.BlockSpec(shape, lambda *_: (0,) * nd)


def _params(sem):
    return pltpu.CompilerParams(dimension_semantics=sem, vmem_limit_bytes=VMEM_LIMIT)


def _mod_kernel(cc_ref, w_ref, b_ref, o_ref):
    cc = cc_ref[...]
    s = cc * jax.nn.sigmoid(cc)
    o_ref[0] = jnp.dot(s, w_ref[0], preferred_element_type=F32, precision=HIGHEST) + b_ref[0]


def _modulation(cc, w_mod, b_mod):
    n_layer, d, six_d = w_mod.shape
    return pl.pallas_call(
        _mod_kernel,
        grid=(n_layer, six_d // d),
        in_specs=[
            pl.BlockSpec((MOD_ROWS, d), lambda l, j: (0, 0)),
            pl.BlockSpec((1, d, d), lambda l, j: (l, 0, j)),
            pl.BlockSpec((1, 1, d), lambda l, j: (l, 0, j)),
        ],
        out_specs=pl.BlockSpec((1, MOD_ROWS, d), lambda l, j: (l, 0, j)),
        out_shape=jax.ShapeDtypeStruct((n_layer, MOD_ROWS, six_d), F32),
        compiler_params=_params(("arbitrary", "arbitrary")),
        name="modulation",
    )(cc, w_mod, b_mod.reshape(n_layer, 1, six_d))


def _gelu_tanh(x):
    c = 0.7978845608028654
    return 0.5 * x * (1.0 + jnp.tanh(c * (x + 0.044715 * (x * x * x))))


def _log_sigmoid(z):
    return jnp.minimum(z, 0.0) - jnp.log1p(jnp.exp(-jnp.abs(z)))


def _inproj_stages(x_of, modc_ref, modl_ref, g1_ref, wmain_ref, wlr_ref, wgates_ref, wg2_ref, bg2_ref,
                   lng_ref, lnb_ref, sguw_ref, sgub_ref, wa_ref, bbr_ref,
                   yag_ref, gb_ref, q_ref, k_ref, v_ref, sr_ref, la_ref, sa_ref):
    first = pl.program_id(1) == 0
    n_g, rows, d = yag_ref.shape
    dk = q_ref.shape[2]
    gdim = d // A_GROUPS
    head_k = dk // GLA_HEADS
    c_q, c_v = 2 * d, 2 * d + 2 * dk
    hb, u, vnb, ya = {}, {}, {}, {}

    def norm(g):
        mod = jnp.where(first, modc_ref[0], modl_ref[g])
        x = x_of(g)
        h = x * lax.rsqrt(jnp.mean(x * x, axis=-1, keepdims=True) + EPS) * g1_ref[...]
        hb[g] = (h * (1.0 + mod[:, d:2 * d]) + mod[:, 0:d]).astype(BF16)

    def gate_u(g):
        u[g] = _gelu_tanh(_dot(hb[g], wmain_ref[:, 0:d]))

    def gate_v(g):
        vv = _gelu_tanh(_dot(hb[g], wmain_ref[:, d:2 * d]))
        vc = vv - jnp.mean(vv, axis=-1, keepdims=True)
        vn = vc * lax.rsqrt(jnp.mean(vc * vc, axis=-1, keepdims=True) + EPS) * lng_ref[...] + lnb_ref[...]
        vnb[g] = vn.astype(BF16)

    def spatial(g):
        for n in range(rows // A_CHUNK):
            rs = slice(n * A_CHUNK, (n + 1) * A_CHUNK)
            for a in range(A_GROUPS):
                cs = slice(a * gdim, (a + 1) * gdim)
                mixed = _dot(sguw_ref[a], vnb[g][rs, cs]) + sgub_ref[:, cs]
                sa_ref[g, rs, cs] = (u[g][rs, cs] * mixed).astype(BF16)

    def proj_a(g):
        ya[g] = _dot(sa_ref[g], wa_ref[...])

    def branch_gates(g):
        gates = _dot(hb[g], wgates_ref[...]) + bbr_ref[...]
        yag_ref[g] = (jax.nn.sigmoid(gates[:, 0:d]) * ya[g]).astype(BF16)
        gb_ref[g] = jax.nn.sigmoid(gates[:, d:2 * d]).astype(BF16)

    def qk(g):
        q_ref[g] = (_dot(hb[g], wmain_ref[:, c_q:c_q + dk]) * (head_k ** -0.5)).astype(BF16)
        k_ref[g] = _dot(hb[g], wmain_ref[:, c_q + dk:c_q + 2 * dk]).astype(BF16)

    def val(g):
        v_ref[g] = _dot(hb[g], wmain_ref[:, c_v:c_v + d]).astype(BF16)

    def out_gate(g):
        r = _dot(hb[g], wmain_ref[:, c_v + d:c_v + 2 * d])
        sr_ref[g] = (r * jax.nn.sigmoid(r)).astype(BF16)

    def decay(g):
        lr = _dot(hb[g], wlr_ref[...]).astype(BF16)
        z = _dot(lr, wg2_ref[...]) + bg2_ref[...]
        la_ref[g] = _log_sigmoid(z) * (1.0 / GLA_TAU)

    order = [norm, gate_u, decay, gate_v, qk, spatial, out_gate, proj_a, branch_gates, val]
    for stage in order:
        for g in range(n_g):
            stage(g)


def _inproj_kernel(xc_ref, xl_ref, *rest):
    first = pl.program_id(1) == 0
    _inproj_stages(lambda g: jnp.where(first, xc_ref[g], xl_ref[g]), *rest)


def _inproj_after_moe_kernel(x1_ref, y00_ref, y01_ref, y10_ref, y11_ref, route_ref, pmodc_ref, pmodl_ref, *rest):
    *rest, x2_ref, sa_ref = rest
    first = pl.program_id(1) == 0
    d = x1_ref.shape[2]
    y_refs = ((y00_ref, y10_ref), (y01_ref, y11_ref))

    def x_of(g):
        pmod = jnp.where(first, pmodc_ref[0], pmodl_ref[g])
        y = (route_ref[g][:, 2:3] * _unpack_halves(y_refs[g][0][...])
             + route_ref[g][:, 3:4] * _unpack_halves(y_refs[g][1][...]))
        x2 = x1_ref[g] + pmod[:, 5 * d:6 * d] * y
        x2_ref[g] = x2
        return x2

    _inproj_stages(x_of, *rest, sa_ref)


def _stream_specs(joined, ctx_row, six_d, d):
    lat_map = (lambda bp, j: (bp, jnp.maximum(j, 1), 0)) if joined else (lambda bp, j: (bp, jnp.maximum(j - 1, 0), 0))
    return [pl.BlockSpec((BG, TM, d), lambda bp, j: (bp, 0, 0)), pl.BlockSpec((BG, TM, d), lat_map),
            pl.BlockSpec((1, 1, six_d), lambda bp, j: (ctx_row, 0, 0)),
            pl.BlockSpec((BG, 1, six_d), lambda bp, j: (bp, 0, 0))]


def _resident_spec(shape):
    nd = len(shape)
    return pl.BlockSpec(shape, lambda *_: (0,) * nd, pipeline_mode=pl.Buffered(1))


def _inproj(xc, xl, pending, nt, mod3, g1, wmain, wlr, wgates, wg2, bg2, lng, lnb, sguw, sgub, wa, bbr, n_batch):
    assert BG == 2
    dk = wg2.shape[1] // 2
    t_all = nt // n_batch
    nj = t_all // TM
    six_d = mod3.shape[2]
    d = six_d // 6
    tile = lambda w: pl.BlockSpec((BG, TM, w), lambda bp, j: (bp, j, 0))
    outs = [(d, BF16), (d, BF16), (dk, BF16), (dk, BF16), (d, BF16), (d, BF16), (2 * dk, F32)]
    weights = (g1, wmain, wlr, wgates, wg2, bg2, lng, lnb, sguw, sgub, wa, bbr)
    mod_specs = lambda: [pl.BlockSpec((1, 1, six_d), lambda bp, j: (n_batch, 0, 0)),
                         pl.BlockSpec((BG, 1, six_d), lambda bp, j: (bp, 0, 0))]
    if pending is None:
        kern = _inproj_kernel
        operands = (xc, xl, mod3, mod3)
        in_specs = _stream_specs(False, n_batch, six_d, d)
    else:
        x1, y_rows, route, mod3_prev = pending
        kern = _inproj_after_moe_kernel
        outs = outs + [(d, F32)]
        y_spec = lambda slot, g: pl.BlockSpec(
            (TM, d // 2), lambda bp, j: (slot * (nt // TM) + (bp * BG + g) * nj + j, 0))
        operands = (x1.reshape(n_batch, t_all, d), y_rows, y_rows, y_rows, y_rows,
                    route.reshape(n_batch, t_all, LANES), mod3_prev, mod3_prev, mod3, mod3)
        in_specs = ([tile(d), y_spec(0, 0), y_spec(0, 1), y_spec(1, 0), y_spec(1, 1), tile(LANES)]
                    + mod_specs() + mod_specs())
    res = pl.pallas_call(
        kern,
        grid=(n_batch // BG, nj),
        in_specs=in_specs + [_resident_spec(a.shape) for a in weights],
        out_specs=[tile(w) for w, _ in outs],
        out_shape=[jax.ShapeDtypeStruct((n_batch, t_all, w), t) for w, t in outs],
        scratch_shapes=[pltpu.VMEM((BG, TM, d), BF16)],
        compiler_params=_params(("arbitrary", "arbitrary")),
        name="inproj",
    )(*operands, *weights)
    return [a.reshape(nt, a.shape[2]) for a in res]


def _gla_tiles(q_refs, k_refs, v_refs, la_refs, s_refs, o_refs, qin_sc, att_sc, kv_sc, dm_sc, reverse):
    n_g = len(q_refs)
    rows, dk = q_refs[0].shape
    dv = v_refs[0].shape[1]
    hk = dk // GLA_HEADS
    hv = dv // GLA_HEADS
    c = GLA_CHUNK
    ri = lax.broadcasted_iota(jnp.int32, (c, c), 0)
    ci = lax.broadcasted_iota(jnp.int32, (c, c), 1)
    tri = (ci >= ri) if reverse else (ci <= ri)
    rt = lax.broadcasted_iota(jnp.int32, (rows, rows), 0)
    ct = lax.broadcasted_iota(jnp.int32, (rows, rows), 1)
    in_chunk = (rt // c) == (ct // c)
    tri_t = jnp.where(jnp.logical_and(in_chunk, (ct >= rt) if reverse else (ct <= rt)), 1.0, 0.0).astype(BF16)
    chunks = list(range(rows // c))
    for g in range(n_g):
        la = la_refs[g][...]
        la_hi = la.astype(BF16)
        la_lo = (la - la_hi.astype(F32)).astype(BF16)
        b_all = _dot(tri_t, la_hi) + _dot(tri_t, la_lo)
        for n in chunks:
            rs = slice(n * c, (n + 1) * c)
            b = b_all[rs, :]
            b_end = b[0:1, :] if reverse else b[c - 1:c, :]
            q = q_refs[g][rs, :].astype(F32)
            k = k_refs[g][rs, :].astype(F32)
            q_in = (q * jnp.exp(b)).astype(BF16)
            k_in = (k * jnp.exp(-b)).astype(BF16)
            k_st = (k * jnp.exp(b_end - b)).astype(BF16)
            decay = jnp.exp(b_end)
            qin_sc[g, rs, :] = q_in
            for h in range(GLA_HEADS):
                ks = slice(h * hk, (h + 1) * hk)
                vs = slice(h * hv, (h + 1) * hv)
                vh = v_refs[g][rs, vs]
                att = lax.dot_general(q_in[:, ks], k_in[:, ks], (((1,), (1,)), ((), ())),
                                      preferred_element_type=F32)
                att_sc[g, rs, h * LANES:h * LANES + c] = jnp.where(tri, att, 0.0).astype(BF16)
                kv_sc[g, n, h] = lax.dot_general(k_st[:, ks], vh, (((0,), (0,)), ((), ())),
                                                 preferred_element_type=F32)
                dm_sc[g, n, h] = jnp.transpose(jnp.broadcast_to(decay[:, ks], (hk, hk)))
    for n in (reversed(chunks) if reverse else chunks):
        rs = slice(n * c, (n + 1) * c)
        for g in range(n_g):
            for h in range(GLA_HEADS):
                ks = slice(h * hk, (h + 1) * hk)
                vs = slice(h * hv, (h + 1) * hv)
                state = s_refs[g][h]
                lhs = jnp.concatenate([qin_sc[g, rs, ks], att_sc[g, rs, h * LANES:h * LANES + c]], axis=1)
                o_refs[g][rs, vs] = _dot(lhs, jnp.concatenate([state.astype(BF16), v_refs[g][rs, vs]], axis=0))
                dmat = jnp.concatenate([dm_sc[g, n, h]] * (hv // hk), axis=1)
                s_refs[g][h] = dmat * state + kv_sc[g, n, h]


def _gla_scratch(n_g, rows, dk, dv):
    hk, hv, n_chunks = dk // GLA_HEADS, dv // GLA_HEADS, rows // GLA_CHUNK
    return [pltpu.VMEM((n_g, GLA_HEADS, hk, hv), F32),
            pltpu.VMEM((n_g, rows, dv), F32),
            pltpu.VMEM((n_g, rows, dk), BF16),
            pltpu.VMEM((n_g, rows, GLA_HEADS * LANES), BF16),
            pltpu.VMEM((n_g, n_chunks, GLA_HEADS, hk, hv), F32),
            pltpu.VMEM((n_g, n_chunks, GLA_HEADS, hk, hk), F32)]


def _per_batch(ref):
    return [ref.at[g] for g in range(ref.shape[0])]


def _gla_bwd_kernel(q_ref, k_ref, v_ref, la_ref, *rest, n_cast):
    if n_cast:
        wg_ref, wu_ref, wd_ref, ob_ref, wgb_ref, wub_ref, wdb_ref, *scratch = rest
    else:
        ob_ref, *scratch = rest
    s_ref, o_sc, qin_sc, att_sc, kv_sc, dm_sc = scratch

    @pl.when(pl.program_id(1) == 0)
    def _():
        s_ref[...] = jnp.zeros_like(s_ref)

    _gla_tiles(_per_batch(q_ref), _per_batch(k_ref), _per_batch(v_ref), _per_batch(la_ref),
               _per_batch(s_ref), _per_batch(o_sc), qin_sc, att_sc, kv_sc, dm_sc, reverse=True)
    ob_ref[...] = o_sc[...].astype(BF16)

    if n_cast:
        @pl.when(pl.program_id(0) * pl.num_programs(1) + pl.program_id(1) < n_cast)
        def _():
            wgb_ref[...] = wg_ref[...].astype(BF16)
            wub_ref[...] = wu_ref[...].astype(BF16)
            wdb_ref[...] = wd_ref[...].astype(BF16)


def _gla_bwd(q, k, v, la, n_batch, experts, layer):
    nt, dk = q.shape
    dv = v.shape[1]
    t_all = nt // n_batch
    nj = t_all // TM
    per_batch = lambda a: a.reshape(n_batch, t_all, a.shape[1])
    steps = n_batch // BG * nj
    n_cast = CAST_STEPS if steps >= CAST_STEPS else 0

    def tile(bp, jj):
        return (bp, jnp.where(jj == 0, 0, nj - jj), 0)

    def tile_la(bp, jj):
        return (bp, jnp.where(jj == 0, 0, nj - jj), 1)

    in_specs = [pl.BlockSpec((BG, TM, dk), tile), pl.BlockSpec((BG, TM, dk), tile),
                pl.BlockSpec((BG, TM, dv), tile), pl.BlockSpec((BG, TM, dk), tile_la)]
    out_specs = [pl.BlockSpec((BG, TM, dv), tile)]
    out_shape = [jax.ShapeDtypeStruct((n_batch, t_all, dv), BF16)]
    operands = [per_batch(q), per_batch(k), per_batch(v), per_batch(la)]
    for w in experts if n_cast else ():
        rows = w.shape[1] * w.shape[2]
        blk = (rows // n_cast, w.shape[3])
        operands.append(w.reshape(-1, w.shape[3]))
        in_specs.append(pl.BlockSpec(
            blk, lambda bp, jj: (layer * n_cast + jnp.minimum(bp * nj + jj, n_cast - 1), 0)))
        out_specs.append(pl.BlockSpec(blk, lambda bp, jj: (jnp.minimum(bp * nj + jj, n_cast - 1), 0)))
        out_shape.append(jax.ShapeDtypeStruct((rows, w.shape[3]), BF16))
    ob, *cast = pl.pallas_call(
        functools.partial(_gla_bwd_kernel, n_cast=n_cast),
        grid=(n_batch // BG, nj),
        in_specs=in_specs,
        out_specs=out_specs,
        out_shape=out_shape,
        scratch_shapes=_gla_scratch(BG, TM, dk, dv),
        compiler_params=_params(("arbitrary", "arbitrary")),
        name="gla_bwd",
    )(*operands)
    if n_cast:
        cast = [c.reshape(w.shape[1:]) for c, w in zip(cast, experts)]
    else:
        cast = [w[layer].astype(BF16) for w in experts]
    return ob.reshape(nt, dv), cast


def _route(logits_t):
    n_tok = logits_t.shape[1]
    eid = lax.broadcasted_iota(jnp.int32, logits_t.shape, 0)
    ex = jnp.exp(logits_t - jnp.max(logits_t, axis=0, keepdims=True))
    p = ex / jnp.sum(ex, axis=0, keepdims=True)
    grp = eid // EXPERTS_PER_GROUP
    none = -1.0
    far = 2 * N_EXPERTS
    best = None
    for g in range(N_EXPERT_GROUPS):
        pg = jnp.where(grp == g, p, none)
        m1 = jnp.max(pg, axis=0, keepdims=True)
        i1 = jnp.min(jnp.where(pg == m1, eid, far), axis=0, keepdims=True)
        pg2 = jnp.where(eid == i1, none, pg)
        m2 = jnp.max(pg2, axis=0, keepdims=True)
        i2 = jnp.min(jnp.where(pg2 == m2, eid, far), axis=0, keepdims=True)
        cand = (m1 + m2, m1, i1, m2, i2)
        if best is None:
            best = cand
        else:
            better = cand[0] > best[0]
            best = tuple(jnp.where(better, c, o) for c, o in zip(cand, best))
    _, m1, i1, m2, i2 = best
    tot = m1 + m2
    sub = lax.broadcasted_iota(jnp.int32, (SUBLANES, n_tok), 0)
    out = jnp.where(sub == 0, i1.astype(F32), 0.0)
    out = jnp.where(sub == 1, i2.astype(F32), out)
    out = jnp.where(sub == 2, m1 / tot, out)
    out = jnp.where(sub == 3, m2 / tot, out)
    return out


def _merge_kernel(xc_ref, xl_ref, modc_ref, modl_ref, q_ref, k_ref, v_ref, la_ref, ob_ref, yag_ref, gb_ref,
                  sr_ref, glag_ref, wb_ref, wout_ref, g2_ref, wr_ref, br_ref,
                  x1_ref, h2_ref, route_ref, routet_ref, s_ref, o_sc, qin_sc, att_sc, kv_sc, dm_sc):
    first = pl.program_id(1) == 0

    @pl.when(first)
    def _():
        s_ref[...] = jnp.zeros_like(s_ref)

    _gla_tiles(_per_batch(q_ref), _per_batch(k_ref), _per_batch(v_ref), _per_batch(la_ref),
               _per_batch(s_ref), _per_batch(o_sc), qin_sc, att_sc, kv_sc, dm_sc, reverse=False)

    d = xc_ref.shape[2]
    hv = d // GLA_HEADS
    mod, yb_in, mix, h2, logits = {}, {}, {}, {}, {}

    def head_norm(g):
        mod[g] = jnp.where(first, modc_ref[0], modl_ref[g])
        o = o_sc[g] + ob_ref[g].astype(F32)
        parts = []
        for h in range(GLA_HEADS):
            oh = o[:, h * hv:(h + 1) * hv]
            parts.append(oh * lax.rsqrt(jnp.mean(oh * oh, axis=-1, keepdims=True) + EPS))
        on = jnp.concatenate(parts, axis=1) * glag_ref[...]
        yb_in[g] = (on * sr_ref[g].astype(F32)).astype(BF16)

    def branch_merge(g):
        yb = _dot(yb_in[g], wb_ref[...])
        mix[g] = (yag_ref[g].astype(F32) + gb_ref[g].astype(F32) * yb).astype(BF16)

    def residual(g):
        y = _dot(mix[g], wout_ref[...])
        x1 = jnp.where(first, xc_ref[g], xl_ref[g]) + mod[g][:, 2 * d:3 * d] * y
        x1_ref[g] = x1
        h = x1 * lax.rsqrt(jnp.mean(x1 * x1, axis=-1, keepdims=True) + EPS) * g2_ref[...]
        h2[g] = h * (1.0 + mod[g][:, 4 * d:5 * d]) + mod[g][:, 3 * d:4 * d]
        h2_ref[g] = _pack_halves(h2[g])

    def router_logits(g):
        h_hi = h2[g].astype(BF16)
        h_lo = (h2[g] - h_hi.astype(F32)).astype(BF16)
        logits[g] = _dot(h_hi, wr_ref[0]) + _dot(h_lo, wr_ref[0]) + _dot(h_hi, wr_ref[1]) + br_ref[...]

    def routing(g):
        rt = _route(jnp.transpose(logits[g])[0:N_EXPERTS, :])
        routet_ref[g, 0] = rt
        sub = lax.broadcasted_iota(jnp.int32, (LANES, rt.shape[1]), 0)
        padded = jnp.zeros((LANES, rt.shape[1]), F32)
        for r in range(4):
            padded = jnp.where(sub == r, rt[r:r + 1, :], padded)
        route_ref[g] = jnp.transpose(padded)

    for stage in (head_norm, branch_merge, residual, router_logits, routing):
        for g in range(xc_ref.shape[0]):
            stage(g)


def _merge(xc, xl, joined, nt, mod3, q, k, v, la, ob, yag, gb, sr, glag, wb, wout, g2, wr, br, n_batch):
    d = xc.shape[2]
    dk = q.shape[1]
    t_all = nt // n_batch
    nj = t_all // TM
    per_batch = lambda a: a.reshape(n_batch, t_all, a.shape[1])
    tile = lambda w: pl.BlockSpec((BG, TM, w), lambda bp, j: (bp, j, 0))
    x1, h2, route, route_t = pl.pallas_call(
        _merge_kernel,
        grid=(n_batch // BG, nj),
        in_specs=_stream_specs(joined, n_batch, mod3.shape[2], d)
        + [tile(dk), tile(dk), tile(d), tile(dk), tile(d), tile(d), tile(d), tile(d)]
        + [_const_spec(a.shape) for a in (glag, wb, wout, g2, wr, br)],
        out_specs=[tile(d), tile(d // 2), tile(LANES),
                   pl.BlockSpec((BG, 1, SUBLANES, TM), lambda bp, j: (bp, j, 0, 0))],
        out_shape=[jax.ShapeDtypeStruct((n_batch, t_all, d), F32),
                   jax.ShapeDtypeStruct((n_batch, t_all, d // 2), jnp.uint32),
                   jax.ShapeDtypeStruct((n_batch, t_all, LANES), F32),
                   jax.ShapeDtypeStruct((n_batch, nj, SUBLANES, TM), F32)],
        scratch_shapes=_gla_scratch(BG, TM, dk, d),
        compiler_params=_params(("arbitrary", "arbitrary")),
        name="merge",
    )(xc, xl, mod3, mod3, per_batch(q), per_batch(k), per_batch(v), per_batch(la), per_batch(ob),
      per_batch(yag), per_batch(gb), per_batch(sr), glag, wb, wout, g2, wr, br)
    return (x1.reshape(nt, d), h2.reshape(nt, d // 2), route.reshape(nt, LANES),
            route_t.reshape(nt // TM, SUBLANES, TM))


def _plan_kernel(rt_ref, slot_ref, be_ref, cnt_ref, pst_ref):
    n_tiles, _, rows = rt_ref.shape
    n_tok = n_tiles * rows
    n_slot = TOP_K * rows
    sub = lax.broadcasted_iota(jnp.int32, (N_EXPERTS, rows), 0).astype(F32)
    wide = lambda col: jnp.broadcast_to(col, (N_EXPERTS, LANES))

    def one_hots(i):
        rt = rt_ref[i]
        return jnp.where(sub == rt[0:1, :], 1.0, 0.0), jnp.where(sub == rt[1:2, :], 1.0, 0.0)

    def count(i, acc):
        oh0, oh1 = one_hots(i)
        return acc + jnp.sum(oh0 + oh1, axis=1, keepdims=True)

    cnt = wide(lax.fori_loop(0, n_tiles, count, jnp.zeros((N_EXPERTS, 1), F32)))
    padded = jnp.floor((cnt + (TB - 1)) * (1.0 / TB)) * TB
    ri = lax.broadcasted_iota(jnp.int32, (N_EXPERTS, N_EXPERTS), 0)
    ci = lax.broadcasted_iota(jnp.int32, (N_EXPERTS, N_EXPERTS), 1)
    p_end = jnp.dot(jnp.where(ci <= ri, 1.0, 0.0), padded, preferred_element_type=F32, precision=HIGHEST)
    p_start = p_end - padded
    cnt_ref[...] = cnt
    pst_ref[...] = p_start
    starts = lax.broadcasted_iota(jnp.int32, (N_EXPERTS, be_ref.shape[1]), 1).astype(F32) * TB
    done = jnp.sum(jnp.where(p_end[:, 0:1] <= starts, 1.0, 0.0), axis=0, keepdims=True)
    be_ref[...] = jnp.minimum(done, N_EXPERTS - 1.0).astype(jnp.int32)
    slot_ref[...] = jnp.full(slot_ref.shape, -1, jnp.int32)

    rr = lax.broadcasted_iota(jnp.int32, (rows, rows), 0)
    cc = lax.broadcasted_iota(jnp.int32, (rows, rows), 1)
    earlier = jnp.where(rr < cc, 1.0, 0.0).astype(BF16)
    before = jnp.where(ci < ri, 1.0, 0.0)
    pos = lax.broadcasted_iota(jnp.int32, (n_slot, n_slot), 0).astype(F32)
    s_idx = lax.broadcasted_iota(jnp.int32, (SUBLANES, n_slot), 1)
    s_sub = lax.broadcasted_iota(jnp.int32, (SUBLANES, n_slot), 0)
    tok_slot = jnp.where(s_sub == 0, s_idx % rows, jnp.where(s_sub == 1, s_idx // rows, 0)).astype(BF16)
    lane = lax.broadcasted_iota(jnp.int32, (1, TB), 1)
    p_start_i = p_start.astype(jnp.int32)

    n_win = -(-(TB + n_slot) // TB)
    spare = slot_ref.shape[0] - 1
    empty = jnp.full((1, TB), -1, jnp.int32)

    def sort_tile(i):
        oh0, oh1 = one_hots(i)
        c0 = _dot(oh0.astype(BF16), earlier)
        c1 = _dot(oh1.astype(BF16), earlier)
        tot0 = jnp.sum(oh0, axis=1, keepdims=True)
        tot = tot0 + jnp.sum(oh1, axis=1, keepdims=True)
        off = jnp.dot(before, wide(tot), preferred_element_type=F32, precision=HIGHEST)[:, 0:1]
        lp = jnp.concatenate([jnp.sum(oh0 * (c0 + off), axis=0, keepdims=True),
                              jnp.sum(oh1 * (c1 + off + tot0), axis=0, keepdims=True)], axis=1)
        perm = jnp.where(pos == lp, 1.0, 0.0).astype(BF16)
        srt = lax.dot_general(tok_slot, perm, (((1,), (1,)), ((), ())), preferred_element_type=F32)
        ids = (srt[0:1, :] + srt[1:2, :] * n_tok).astype(jnp.int32) + i * rows
        ids = jnp.broadcast_to(jnp.concatenate([ids, jnp.zeros_like(ids)], axis=1), (SUBLANES, 2 * n_slot))
        return ids, tot, off

    def place(sorted_tile, carry):
        ids, tot, off = sorted_tile
        run, open_rows = carry
        tot_i, run_i, off_i = tot.astype(jnp.int32), run.astype(jnp.int32), off.astype(jnp.int32)
        new_rows = []
        for e in range(N_EXPERTS):
            n_e = tot_i[e, 0]
            row0 = p_start_i[e, 0] + run_i[e, 0]
            blk = row0 // TB
            at = row0 - blk * TB
            end = at + n_e
            moved = pltpu.roll(ids, at - off_i[e, 0] + 2 * n_slot, axis=1)
            keep = open_rows[e]
            for w in range(n_win):
                here = jnp.logical_and(lane + w * TB >= at, lane + w * TB < end)
                row = jnp.where(here, moved[0:1, w * TB:(w + 1) * TB], open_rows[e] if w == 0 else empty)
                slot_ref[pl.ds(blk if w == 0 else jnp.where(end > w * TB, blk + w, spare), 1), :] = row
                if w > 0:
                    keep = jnp.where(end >= w * TB, row, keep)
                else:
                    keep = row
            new_rows.append(keep)
        return run + tot, tuple(new_rows)

    def two_tiles(j, carry):
        first, second = sort_tile(2 * j), sort_tile(2 * j + 1)
        return place(second, place(first, carry))

    lax.fori_loop(0, n_tiles // 2, two_tiles, (jnp.zeros((N_EXPERTS, 1), F32), (empty,) * N_EXPERTS))


def _plan(route_t, n_blocks):
    nbp = -(-(n_blocks + TOP_K * TM // TB) // LANES) * LANES
    assert route_t.shape[0] % 2 == 0
    small = jax.ShapeDtypeStruct((N_EXPERTS, LANES), F32)
    return pl.pallas_call(
        _plan_kernel,
        out_shape=[jax.ShapeDtypeStruct((nbp, TB), jnp.int32),
                   jax.ShapeDtypeStruct((1, nbp), jnp.int32), small, small],
        compiler_params=pltpu.CompilerParams(vmem_limit_bytes=VMEM_LIMIT),
        name="plan",
    )(route_t)


def _sorted_rows(slot_raw, block_e, cnt, pst, n_tok, n_blocks):
    cnt = cnt[:, 0].astype(jnp.int32)
    pst = pst[:, 0].astype(jnp.int32)
    pad = (cnt + TB - 1) // TB * TB - cnt
    pad_before = jnp.cumsum(pad) - pad
    be = block_e[0, :n_blocks]
    base = TOP_K * n_tok - pst[be] - cnt[be] + pad_before[be]
    pad_id = base[:, None] + jnp.arange(n_blocks * TB, dtype=jnp.int32).reshape(n_blocks, TB)
    raw = slot_raw[:n_blocks]
    slot = jnp.where(raw < 0, pad_id, raw)
    src = jnp.where(raw < 0, 0, raw % n_tok)
    return slot.reshape(n_blocks, 1, TB), src.reshape(n_blocks, 1, TB)


def _moe_kernel(be_ref, src_cur, src_nxt, src_nx2, slot_prv, slot_cur, h_hbm, wg_ref, wu_ref, wd_ref, y_hbm,
                x0, x1, x2, y0, y1, y2, gsem, ssem):
    del be_ref
    i = pl.program_id(0)
    last = pl.num_programs(0) - 1
    xs, ys = (x0, x1, x2), (y0, y1, y2)

    def gather_start(idx_ref, dst, sem):
        for r in range(TB):
            pltpu.make_async_copy(h_hbm.at[pl.ds(idx_ref[0, 0, r], 1)], dst.at[pl.ds(r, 1)], sem).start(
                priority=r % 2)

    def gather_wait(dst, sem):
        pltpu.make_async_copy(h_hbm.at[pl.ds(0, TB)], dst, sem).wait()

    def scatter_start(idx_ref, src, sem):
        for r in range(TB):
            pltpu.make_async_copy(src.at[pl.ds(r, 1)], y_hbm.at[pl.ds(idx_ref[0, 0, r], 1)], sem).start(
                priority=r % 2)

    def scatter_wait(src, sem):
        pltpu.make_async_copy(src, y_hbm.at[pl.ds(0, TB)], sem).wait()

    def experts(x_ref, y_ref):
        x = _unpack_halves(x_ref[...]).astype(BF16)
        g = _dot(x, wg_ref[0])
        u = _dot(x, wu_ref[0])
        a = (g * jax.nn.sigmoid(g) * u).astype(BF16)
        y_ref[...] = _pack_halves(_dot(a, wd_ref[0]))

    for p in range(3):
        @pl.when(jnp.logical_and(i > 0, lax.rem(i, 3) == p))
        def _(p=p):
            prv, nx2 = (p + 2) % 3, (p + 2) % 3
            gather_wait(xs[p], gsem.at[p])

            @pl.when(i >= 3)
            def _():
                scatter_wait(ys[p], ssem.at[p])

            gather_start(src_nx2, xs[nx2], gsem.at[nx2])
            scatter_start(slot_prv, ys[prv], ssem.at[prv])
            experts(xs[p], ys[p])

    for p in range(3):
        @pl.when(jnp.logical_and(i == last, lax.rem(i, 3) == p))
        def _(p=p):
            nxt, prv = (p + 1) % 3, (p + 2) % 3
            scatter_start(slot_cur, ys[p], ssem.at[p])
            scatter_wait(ys[nxt], ssem.at[nxt])
            scatter_wait(ys[prv], ssem.at[prv])
            scatter_wait(ys[p], ssem.at[p])
            gather_wait(xs[nxt], gsem.at[nxt])
            gather_wait(xs[prv], gsem.at[prv])

    @pl.when(i == 0)
    def _():
        gather_start(src_cur, x0, gsem.at[0])
        gather_start(src_nxt, x1, gsem.at[1])
        gather_wait(x0, gsem.at[0])
        gather_start(src_nx2, x2, gsem.at[2])
        experts(x0, y0)


def _moe(block_e, slot3, src3, h2, wg, wu, wd):
    nb = slot3.shape[0]
    dh = h2.shape[1]
    d, de = wg.shape[1], wg.shape[2]
    expert = lambda i, be: (be[i], 0, 0)
    idx_spec = lambda f: pl.BlockSpec((1, 1, TB), f, memory_space=pltpu.SMEM)
    assert nb >= 3
    cur = lambda i, be: (i, 0, 0)
    nxt = lambda i, be: (jnp.minimum(i + 1, nb - 1), 0, 0)
    nx2 = lambda i, be: (jnp.minimum(i + 2, nb - 1), 0, 0)
    prv = lambda i, be: (jnp.maximum(i - 1, 0), 0, 0)
    vm = pltpu.VMEM((TB, dh), jnp.uint32)
    return pl.pallas_call(
        _moe_kernel,
        grid_spec=pltpu.PrefetchScalarGridSpec(
            num_scalar_prefetch=1,
            grid=(nb,),
            in_specs=[idx_spec(cur), idx_spec(nxt), idx_spec(nx2), idx_spec(prv), idx_spec(cur),
                      pl.BlockSpec(memory_space=pl.ANY),
                      pl.BlockSpec((1, d, de), expert),
                      pl.BlockSpec((1, d, de), expert),
                      pl.BlockSpec((1, de, d), expert)],
            out_specs=pl.BlockSpec(memory_space=pl.ANY),
            scratch_shapes=[vm] * 6 + [pltpu.SemaphoreType.DMA((3,)), pltpu.SemaphoreType.DMA((3,))],
        ),
        out_shape=jax.ShapeDtypeStruct((nb * TB, dh), jnp.uint32),
        compiler_params=_params(("arbitrary",)),
        name="moe",
    )(block_e, src3, src3, src3, slot3, slot3, h2, wg, wu, wd)


def _combine_kernel(y0_ref, y1_ref, x1_ref, route_ref, mod_ref, gf_ref, o_ref):
    d = x1_ref.shape[1]
    y = route_ref[:, 2:3] * _unpack_halves(y0_ref[...]) + route_ref[:, 3:4] * _unpack_halves(y1_ref[...])
    x2 = x1_ref[...] + mod_ref[:, 5 * d:6 * d] * y
    o_ref[...] = x2 * lax.rsqrt(jnp.mean(x2 * x2, axis=-1, keepdims=True) + EPS) * gf_ref[...]


def _combine(y_rows, x1, route, mod3, gf, n_batch):
    nt, d = x1.shape
    nj = nt // n_batch // TM
    n_lat = n_batch * (nj - 1)
    stream_tile = lambda i: ((i // (nj - 1)) * nj + 1 + i % (nj - 1), 0)
    return pl.pallas_call(
        _combine_kernel,
        grid=(n_lat,),
        in_specs=[pl.BlockSpec((TM, d // 2), lambda i: (i, 0)),
                  pl.BlockSpec((TM, d // 2), lambda i: (n_lat + i, 0)),
                  pl.BlockSpec((TM, d), stream_tile),
                  pl.BlockSpec((TM, LANES), stream_tile),
                  pl.BlockSpec((None, 1, mod3.shape[2]), lambda i: (i // (nj - 1), 0, 0)),
                  _const_spec(gf.shape)],
        out_specs=pl.BlockSpec((TM, d), lambda i: (i, 0)),
        out_shape=jax.ShapeDtypeStruct((n_lat * TM, d), F32),
        compiler_params=_params(("arbitrary",)),
        name="combine",
    )(y_rows, y_rows, x1, route, mod3, gf)


def kernel(x, c, ctx, c_ctx, w_mod, b_mod, g_norm1, g_norm2, w_in, w_gate2, b_gate2, gla_norm_g,
           sgu_ln_g, sgu_ln_b, sgu_w, sgu_b, w_branch_a, w_branch_b, b_branch, w_out,
           w_router, b_router, w_exp_gate, w_exp_up, w_exp_down, g_final):
    n_batch, seq, d = x.shape
    ctx_len = ctx.shape[1]
    depth = w_mod.shape[0]
    dk = w_gate2.shape[3]
    rank_lr = w_gate2.shape[2]
    assert ctx_len == TM and seq % TM == 0 and n_batch < MOD_ROWS

    nt = n_batch * (ctx_len + seq)
    streams = (ctx, x, False)
    pending = None
    cc = jnp.zeros((MOD_ROWS, d), F32).at[:n_batch].set(c).at[n_batch].set(c_ctx)
    mod = _modulation(cc, w_mod, b_mod)

    wr = jnp.zeros((d, LANES), F32).at[:, :N_EXPERTS].set(w_router)
    wr_hi = wr.astype(BF16)
    wr = jnp.stack([wr_hi, (wr - wr_hi.astype(F32)).astype(BF16)])
    br = jnp.zeros((1, LANES), F32).at[0, :N_EXPERTS].set(b_router)
    row = lambda a: a.reshape(1, -1)
    n_main = 2 * d + 2 * dk + 2 * d

    out = None
    for l in range(depth):
        last = l == depth - 1
        mod3 = mod[l].reshape(MOD_ROWS, 1, 6 * d)
        wmain = w_in[l][:, :n_main].astype(BF16)
        wlr = w_in[l][:, n_main:n_main + 2 * rank_lr].astype(BF16)
        wgates = w_in[l][:, n_main + 2 * rank_lr:].astype(BF16)
        wg2 = jnp.zeros((2 * rank_lr, 2 * dk), F32)
        wg2 = wg2.at[:rank_lr, :dk].set(w_gate2[l, 0]).at[rank_lr:, dk:].set(w_gate2[l, 1]).astype(BF16)
        bg2 = b_gate2[l].reshape(1, 2 * dk)
        sgub = jnp.repeat(sgu_b[l].T, d // A_GROUPS, axis=1)

        yag, gb, q, k, v, sr, la, *x2 = _inproj(
            streams[0], streams[1], pending, nt, mod3, row(g_norm1[l]), wmain, wlr, wgates, wg2, bg2,
            row(sgu_ln_g[l]), row(sgu_ln_b[l]), sgu_w[l].astype(BF16), sgub, w_branch_a[l].astype(BF16),
            row(b_branch[l]), n_batch)
        if x2:
            x2 = x2[0].reshape(n_batch, -1, d)
            streams = (x2, x2, True)
        ob, (wg, wu, wd) = _gla_bwd(q, k, v, la, n_batch, (w_exp_gate, w_exp_up, w_exp_down), l)
        x1, h2, route, route_t = _merge(
            *streams, nt, mod3, q, k, v, la, ob, yag, gb, sr, row(gla_norm_g[l]), w_branch_b[l].astype(BF16),
            w_out[l].astype(BF16), row(g_norm2[l]), wr, br, n_batch)
        if last:
            n_tok = n_batch * seq
            route_t = route_t.reshape(n_batch, -1, SUBLANES, TM)[:, ctx_len // TM:].reshape(-1, SUBLANES, TM)
        else:
            n_tok = nt
        n_blocks = TOP_K * n_tok // TB + N_EXPERTS
        slot_raw, block_e, cnt, pst = _plan(route_t, n_blocks)
        slot3, src3 = _sorted_rows(slot_raw, block_e, cnt, pst, n_tok, n_blocks)
        if last:
            src3 = src3 + (src3 // seq + 1) * ctx_len
        y_rows = _moe(block_e.reshape(-1), slot3, src3, h2, wg, wu, wd)
        if last:
            out = _combine(y_rows, x1, route, mod3, row(g_final), n_batch).reshape(n_batch, seq, d)
        else:
            pending = (x1, y_rows, route, mod3)
    return out
```
